```python
import jax
import jax.numpy as jnp
from jax import lax
import numpy as np

D_MODEL = 1024
BATCH = 2
SEQ = 8192
DEPTH = 4

N_EVEN = (DEPTH + 1) // 2
N_ODD = DEPTH // 2

NSA_HEADS = 8
NSA_KV_GROUPS = 2
NSA_QPG = NSA_HEADS // NSA_KV_GROUPS
NSA_HD = 64
NSA_W = NSA_HEADS * NSA_HD
NSA_KV_W = NSA_KV_GROUPS * NSA_HD
CMP_BLOCK = 32
CMP_STRIDE = 16
SLC_BLOCK = 64
N_SELECT = 16
WINDOW = 512
Q_BLOCK = 128
CMP_RATIO = CMP_BLOCK // CMP_STRIDE
SEL_RATIO = SLC_BLOCK // CMP_STRIDE
SEL_AGG_W = tuple(float(w) for w in np.convolve(np.ones(SEL_RATIO), np.ones(CMP_RATIO)))

HG_HEADS = 4
HG_DK = 128
HG_DV = 128
HG_W = HG_HEADS * HG_DV
HG_CHUNK = 64

MIX_W = NSA_W + HG_W
IN_SIZES = (NSA_W,) + (NSA_KV_W,) * 6 + (NSA_HEADS * 3, HG_HEADS * HG_DK, HG_HEADS * HG_DK, HG_W, HG_W)
N_IN = sum(IN_SIZES)
IN_OFFSETS = tuple(int(v) for v in np.cumsum(IN_SIZES)[:-1])

CONV_W = 31
D_CONV = D_MODEL

MOE_GROUPS = 4
MOE_EPG = 8
MOE_EXPERTS = MOE_GROUPS * MOE_EPG
MOE_TOPK = 2
EXPERT_FF = 512
MOE_BLOCK = 128

EPS = 1e-6
NEG = -1e30
FORCE = 1e4

kernel_name = 'hybrid_nsa_hgrn2_conformer_hmoe'


def rms_norm(x, g):
    xf = x.astype(jnp.float32)
    y = xf * lax.rsqrt(jnp.mean(xf * xf, axis=-1, keepdims=True) + EPS)
    return (y * g.astype(jnp.float32)).astype(x.dtype)


def layer_norm(x, g, b):
    xf = x.astype(jnp.float32)
    mu = jnp.mean(xf, axis=-1, keepdims=True)
    var = jnp.mean(jnp.square(xf - mu), axis=-1, keepdims=True)
    y = (xf - mu) * lax.rsqrt(var + EPS) * g.astype(jnp.float32) + b.astype(jnp.float32)
    return y.astype(x.dtype)


def masked_softmax(s, valid):
    p = jax.nn.softmax(jnp.where(valid, s, NEG), axis=-1)
    return p * valid


def nsa_mixer(q, k_cmp, v_cmp, k_slc, v_slc, k_win, v_win, gate_logits, w_ck, w_cv, cmp_pe, q_gain, k_gain):
    B, S = q.shape[0], q.shape[1]
    G, J, HD = NSA_KV_GROUPS, NSA_QPG, NSA_HD
    scale = HD ** -0.5
    out_dtype = q.dtype
    qn = rms_norm(q.reshape(B, S, G, J, HD), q_gain)
    gates = jax.nn.sigmoid(gate_logits.astype(jnp.float32)).reshape(B, S, G, J, 3)

    def kv(t):
        return t.reshape(B, S, G, HD)

    def compress(t, w):
        c = kv(t).reshape(B, S // CMP_STRIDE, CMP_STRIDE, G, HD)
        n = S // CMP_STRIDE - CMP_RATIO + 1
        blocks = jnp.concatenate([c[:, r:r + n] for r in range(CMP_RATIO)], axis=2) + cmp_pe[:, None, :]
        return jnp.einsum('bnlgd,lde->bnge', blocks, w.reshape(CMP_BLOCK, HD, HD))

    kc = rms_norm(compress(k_cmp, w_ck), k_gain[0])
    vc = compress(v_cmp, w_cv)
    n_cmp = kc.shape[1]
    cmp_end = jnp.arange(n_cmp) * CMP_STRIDE + CMP_BLOCK - 1
    n_slc = S // SLC_BLOCK
    n_sel = min(N_SELECT, n_slc)
    ks_blocks = rms_norm(kv(k_slc), k_gain[1]).reshape(B, n_slc, SLC_BLOCK, G, HD).transpose(0, 3, 1, 2, 4)
    vs_blocks = kv(v_slc).reshape(B, n_slc, SLC_BLOCK, G, HD).transpose(0, 3, 1, 2, 4)
    pad_w = ((0, 0), (WINDOW, 0), (0, 0), (0, 0))
    kw_pad = jnp.pad(rms_norm(kv(k_win), k_gain[2]), pad_w)
    vw_pad = jnp.pad(kv(v_win), pad_w)
    b_idx = jnp.arange(B)[:, None, None, None]
    g_idx = jnp.arange(G)[None, :, None, None]
    slc_ids = jnp.arange(n_slc)[None, :]

    def block_fn(i):
        t0 = i * Q_BLOCK
        pos = t0 + jnp.arange(Q_BLOCK)
        qb = lax.dynamic_slice_in_dim(qn, t0, Q_BLOCK, axis=1)
        gb = lax.dynamic_slice_in_dim(gates, t0, Q_BLOCK, axis=1)
        s_c = jnp.einsum('bqgjd,bngd->bgjqn', qb, kc).astype(jnp.float32) * scale
        p_c = masked_softmax(s_c, cmp_end[None, :] <= pos[:, None])
        o_c = jnp.einsum('bgjqn,bngd->bqgjd', p_c.astype(vc.dtype), vc)
        imp = p_c.sum(axis=2)
        imp_pad = jnp.pad(imp, ((0, 0), (0, 0), (0, 0), (CMP_RATIO - 1, SEL_RATIO)))
        imp_s = sum(w * imp_pad[..., o:o + SEL_RATIO * n_slc:SEL_RATIO] for o, w in enumerate(SEL_AGG_W))
        q_blk = (pos // SLC_BLOCK)[:, None]
        forced = (slc_ids == 0) | (slc_ids == q_blk) | (slc_ids == q_blk - 1)
        score = jnp.where(slc_ids > q_blk, NEG, jnp.where(forced, FORCE, imp_s))
        top_val, top_idx = lax.top_k(score, n_sel)
        k_sel = ks_blocks[b_idx, g_idx, top_idx]
        v_sel = vs_blocks[b_idx, g_idx, top_idx]
        key_pos = top_idx[..., None] * SLC_BLOCK + jnp.arange(SLC_BLOCK)
        valid_s = (top_val[..., None] > NEG / 2) & (key_pos <= pos[None, None, :, None, None])
        s_s = jnp.einsum('bqgjd,bgqnld->bgjqnl', qb, k_sel).astype(jnp.float32) * scale
        n_keys = n_sel * SLC_BLOCK
        p_s = masked_softmax(s_s.reshape(B, G, J, Q_BLOCK, n_keys), valid_s.reshape(B, G, 1, Q_BLOCK, n_keys))
        o_s = jnp.einsum('bgjqk,bgqkd->bqgjd', p_s.astype(v_sel.dtype), v_sel.reshape(B, G, Q_BLOCK, n_keys, HD))
        kwb = lax.dynamic_slice_in_dim(kw_pad, t0, Q_BLOCK + WINDOW, axis=1)
        vwb = lax.dynamic_slice_in_dim(vw_pad, t0, Q_BLOCK + WINDOW, axis=1)
        kpos = (t0 - WINDOW + jnp.arange(Q_BLOCK + WINDOW))[None, :]
        valid_w = (kpos >= 0) & (kpos <= pos[:, None]) & (kpos > pos[:, None] - WINDOW)
        s_w = jnp.einsum('bqgjd,bkgd->bgjqk', qb, kwb).astype(jnp.float32) * scale
        p_w = masked_softmax(s_w, valid_w)
        o_w = jnp.einsum('bgjqk,bkgd->bqgjd', p_w.astype(vwb.dtype), vwb)
        return (gb[..., 0:1] * o_c + gb[..., 1:2] * o_s + gb[..., 2:3] * o_w).astype(out_dtype)

    o = lax.map(block_fn, jnp.arange(S // Q_BLOCK))
    return o.transpose(1, 0, 2, 3, 4, 5).reshape(B, S, NSA_W)


def hgrn2_mixer(hq, hf, hi, hg, lb, o_gain):
    B, S = hq.shape[0], hq.shape[1]
    H, DK, DV, C = HG_HEADS, HG_DK, HG_DV, HG_CHUNK
    nc = S // C
    f32 = jnp.float32
    lbh = lb.astype(f32).reshape(H, DK)
    q = jax.nn.silu(hq.astype(f32)).reshape(B, S, H, DK)
    f = lbh + (1.0 - lbh) * jax.nn.sigmoid(hf.astype(f32).reshape(B, S, H, DK))
    log_f = jnp.log(f)
    k = 1.0 - f
    v = hi.astype(f32).reshape(B, S, H, DV)

    def to_chunks(t):
        return t.reshape(B, nc, C, H, t.shape[-1]).transpose(1, 0, 3, 2, 4)

    causal = jnp.tril(jnp.ones((C, C), bool))[:, :, None]

    def chunk_step(state, inp):
        qc, kc, vc, lfc = inp
        b = jnp.cumsum(lfc, axis=2)
        decay = jnp.exp(jnp.where(causal, b[:, :, :, None, :] - b[:, :, None, :, :], -jnp.inf))
        attn = jnp.einsum('bhtd,bhsd,bhtsd->bhts', qc, kc, decay)
        o = attn @ vc + jnp.einsum('bhtd,bhde->bhte', qc * jnp.exp(b), state)
        b_end = b[:, :, -1]
        state = jnp.exp(b_end)[..., None] * state + jnp.einsum('bhsd,bhse->bhde', kc * jnp.exp(b_end[:, :, None] - b), vc)
        return state, o

    state0 = jnp.zeros((B, H, DK, DV), f32)
    _, o = lax.scan(chunk_step, state0, (to_chunks(q), to_chunks(k), to_chunks(v), to_chunks(log_f)))
    o = o.transpose(1, 0, 3, 2, 4).reshape(B, S, H, DV)
    o = rms_norm(o, o_gain) * jax.nn.silu(hg.astype(f32).reshape(B, S, H, DV))
    return o.reshape(B, S, HG_W).astype(hq.dtype)


def even_mixer(h, w_in, w_out, w_ck, w_cv, cmp_pe, q_gain, k_gain, lb, o_gain):
    z = h @ w_in
    q, k_c, v_c, k_s, v_s, k_w, v_w, g_nsa, hq, hf, hi, hg = jnp.split(z, IN_OFFSETS, axis=-1)
    a = nsa_mixer(q, k_c, v_c, k_s, v_s, k_w, v_w, g_nsa, w_ck, w_cv, cmp_pe, q_gain, k_gain)
    r = hgrn2_mixer(hq, hf, hi, hg, lb, o_gain)
    return jnp.concatenate([a, r], axis=-1) @ w_out


def conformer_conv(h, w_pw1, dw, dw_b, ln_g, ln_b, w_pw2):
    u = h @ w_pw1
    a, g = jnp.split(u, 2, axis=-1)
    u = a * jax.nn.sigmoid(g)
    u = lax.conv_general_dilated(u, dw[:, None, :], window_strides=(1,), padding=((CONV_W - 1, 0),),
                                 dimension_numbers=('NWC', 'WIO', 'NWC'), feature_group_count=D_CONV) + dw_b
    u = jax.nn.silu(layer_norm(u, ln_g, ln_b))
    return u @ w_pw2


def hier_moe(h, w_group, w_expert, w1, w3, w2):
    B, S, D = h.shape
    T = B * S
    x = h.reshape(T, D)
    f32 = jnp.float32
    g_logits = (x @ w_group).astype(f32)
    g_sel = jnp.argmax(g_logits, axis=-1)
    g_gate = jnp.take_along_axis(jax.nn.softmax(g_logits, axis=-1), g_sel[:, None], axis=1)
    e_logits = (x @ w_expert).astype(f32).reshape(T, MOE_GROUPS, MOE_EPG)
    e_logits = jnp.take_along_axis(e_logits, g_sel[:, None, None], axis=1)[:, 0]
    e_val, e_loc = lax.top_k(jax.nn.softmax(e_logits, axis=-1), MOE_TOPK)
    gate = g_gate * e_val / jnp.sum(e_val, axis=-1, keepdims=True)
    expert = g_sel[:, None] * MOE_EPG + e_loc
    A = T * MOE_TOPK
    e_flat = expert.reshape(A)
    w_flat = gate.reshape(A)
    tok_flat = jnp.repeat(jnp.arange(T, dtype=jnp.int32), MOE_TOPK)
    order = jnp.argsort(e_flat)
    e_sorted = e_flat[order]
    counts = jnp.bincount(e_flat, length=MOE_EXPERTS)
    starts = jnp.cumsum(counts) - counts
    padded = (counts + MOE_BLOCK - 1) // MOE_BLOCK * MOE_BLOCK
    pends = jnp.cumsum(padded)
    pstarts = pends - padded
    dest = pstarts[e_sorted] + jnp.arange(A) - starts[e_sorted]
    n_rows = A + MOE_EXPERTS * MOE_BLOCK
    n_blocks = n_rows // MOE_BLOCK
    row_tok = jnp.full((n_rows,), T, jnp.int32).at[dest].set(tok_flat[order])
    row_w = jnp.zeros((n_rows,), f32).at[dest].set(w_flat[order])
    blk_exp = jnp.minimum(jnp.searchsorted(pends, jnp.arange(n_blocks) * MOE_BLOCK, side='right'), MOE_EXPERTS - 1)
    x_pad = jnp.concatenate([x, jnp.zeros((1, D), x.dtype)], axis=0)

    def expert_block(args):
        tok, e = args
        xb = x_pad[tok]
        return (jax.nn.silu(xb @ w1[e]) * (xb @ w3[e])) @ w2[e]

    y = lax.map(expert_block, (row_tok.reshape(n_blocks, MOE_BLOCK), blk_exp))
    y = y.reshape(n_rows, D) * row_w[:, None].astype(y.dtype)
    out = jnp.zeros((T + 1, D), y.dtype).at[row_tok].add(y)[:T]
    return out.reshape(B, S, D).astype(h.dtype)


def setup_inputs(seed: int = 0) -> dict:
    key = jax.random.key(seed)
    ks = jax.random.split(key, 26)
    D = D_MODEL

    def nrm(k, shape, scale):
        return jax.random.normal(k, shape, jnp.float32) * scale

    return {
        'x': nrm(ks[0], (BATCH, SEQ, D), 1.0),
        'c': nrm(ks[1], (BATCH, D), 1.0),
        'ada_w': nrm(ks[2], (DEPTH, D, 6 * D), 0.5 * D ** -0.5),
        'ada_b': nrm(ks[3], (DEPTH, 6 * D), 0.02),
        'norm_mix': 1.0 + nrm(ks[4], (DEPTH, D), 0.05),
        'norm_ffn': 1.0 + nrm(ks[5], (DEPTH, D), 0.05),
        'mix_w_in': nrm(ks[6], (N_EVEN, D, N_IN), D ** -0.5),
        'mix_w_out': nrm(ks[7], (N_EVEN, MIX_W, D), MIX_W ** -0.5),
        'nsa_cmp_wk': nrm(ks[8], (N_EVEN, CMP_BLOCK * NSA_HD, NSA_HD), (CMP_BLOCK * NSA_HD) ** -0.5),
        'nsa_cmp_wv': nrm(ks[9], (N_EVEN, CMP_BLOCK * NSA_HD, NSA_HD), (CMP_BLOCK * NSA_HD) ** -0.5),
        'nsa_cmp_pe': nrm(ks[10], (N_EVEN, CMP_BLOCK, NSA_HD), 0.1),
        'nsa_q_gain': 1.0 + nrm(ks[11], (N_EVEN, NSA_HD), 0.05),
        'nsa_k_gain': 1.0 + nrm(ks[12], (N_EVEN, 3, NSA_HD), 0.05),
        'hgrn_lb_logits': nrm(ks[13], (N_EVEN, HG_HEADS * HG_DK), 1.0),
        'hgrn_o_gain': 1.0 + nrm(ks[14], (N_EVEN, HG_DV), 0.05),
        'conv_w_pw1': nrm(ks[15], (N_ODD, D, 2 * D_CONV), D ** -0.5),
        'conv_dw': nrm(ks[16], (N_ODD, CONV_W, D_CONV), CONV_W ** -0.5),
        'conv_dw_b': nrm(ks[17], (N_ODD, D_CONV), 0.02),
        'conv_ln_g': 1.0 + nrm(ks[18], (N_ODD, D_CONV), 0.05),
        'conv_ln_b': nrm(ks[19], (N_ODD, D_CONV), 0.02),
        'conv_w_pw2': nrm(ks[20], (N_ODD, D_CONV, D), D_CONV ** -0.5),
        'moe_w_group': nrm(ks[21], (DEPTH, D, MOE_GROUPS), D ** -0.5),
        'moe_w_expert': nrm(ks[22], (DEPTH, D, MOE_EXPERTS), D ** -0.5),
        'moe_w1': nrm(ks[23], (DEPTH, MOE_EXPERTS, D, EXPERT_FF), D ** -0.5),
        'moe_w3': nrm(ks[24], (DEPTH, MOE_EXPERTS, D, EXPERT_FF), D ** -0.5),
        'moe_w2': nrm(ks[25], (DEPTH, MOE_EXPERTS, EXPERT_FF, D), EXPERT_FF ** -0.5),
    }


def reference(x, c, ada_w, ada_b, norm_mix, norm_ffn, mix_w_in, mix_w_out, nsa_cmp_wk, nsa_cmp_wv, nsa_cmp_pe,
              nsa_q_gain, nsa_k_gain, hgrn_lb_logits, hgrn_o_gain, conv_w_pw1, conv_dw, conv_dw_b, conv_ln_g,
              conv_ln_b, conv_w_pw2, moe_w_group, moe_w_expert, moe_w1, moe_w3, moe_w2):
    lb_p = jax.nn.softmax(hgrn_lb_logits.astype(jnp.float32), axis=0)
    lower_bounds = jnp.cumsum(lb_p, axis=0) - lb_p[0]
    c_act = jax.nn.silu(c)
    for layer in range(DEPTH):
        mod = (c_act @ ada_w[layer] + ada_b[layer])[:, None, :]
        sh_m, sc_m, g_m, sh_f, sc_f, g_f = jnp.split(mod, 6, axis=-1)
        h = rms_norm(x, norm_mix[layer]) * (1.0 + sc_m) + sh_m
        i = layer // 2
        if layer % 2 == 0:
            y = even_mixer(h, mix_w_in[i], mix_w_out[i], nsa_cmp_wk[i], nsa_cmp_wv[i], nsa_cmp_pe[i],
                           nsa_q_gain[i], nsa_k_gain[i], lower_bounds[i], hgrn_o_gain[i])
        else:
            y = conformer_conv(h, conv_w_pw1[i], conv_dw[i], conv_dw_b[i], conv_ln_g[i], conv_ln_b[i], conv_w_pw2[i])
        x = x + g_m * y
        h = rms_norm(x, norm_ffn[layer]) * (1.0 + sc_f) + sh_f
        x = x + g_f * hier_moe(h, moe_w_group[layer], moe_w_expert[layer], moe_w1[layer], moe_w3[layer], moe_w2[layer])
    return x
```

```python
import functools

import jax
import jax.numpy as jnp
from jax import lax
from jax.experimental import pallas as pl
from jax.experimental.pallas import tpu as pltpu

F32 = jnp.float32
BF16 = jnp.bfloat16
HIGHEST = lax.Precision.HIGHEST

D_MODEL = 1024
DEPTH = 4

NSA_HEADS = 8
NSA_KV_GROUPS = 2
NSA_QPG = NSA_HEADS // NSA_KV_GROUPS
NSA_HD = 64
NSA_W = NSA_HEADS * NSA_HD
NSA_KV_W = NSA_KV_GROUPS * NSA_HD
CMP_BLOCK = 32
CMP_STRIDE = 16
SLC_BLOCK = 64
N_SELECT = 16
WINDOW = 512
Q_BLOCK = 128
SEL_AGG_W = (1.0, 2.0, 2.0, 2.0, 1.0)

HG_HEADS = 4
HG_DK = 128
HG_DV = 128
HG_W = HG_HEADS * HG_DV
HG_CHUNK = 64
HG_SUB = 16

CONV_W = 31
MOE_GROUPS = 4
MOE_EPG = 8
MOE_EXPERTS = MOE_GROUPS * MOE_EPG
EXPERT_FF = 512

EPS = 1e-6
NEG = -1e30
FORCE = 1e4

LANES = 128
VMEM_LIMIT = 56 * 1024 * 1024

Z_Q = 0
Z_KC = 512
Z_VC = 640
Z_KS = 768
Z_VS = 896
Z_KW = 1024
Z_VW = 1152
Z_GATE = 1280
Z_HQ = Z_GATE + NSA_KV_GROUPS * LANES
Z_HF = Z_HQ + HG_W
Z_HI = Z_HF + HG_W
Z_HG = Z_HI + HG_W
Z_N = Z_HG + HG_W

TM = 256
MOE_BM = 256


def _cparams(sem):
    return pltpu.CompilerParams(dimension_semantics=sem, vmem_limit_bytes=VMEM_LIMIT)


def _sigmoid(x):
    return 1.0 / (1.0 + jnp.exp(-x))


def _silu(x):
    return x * _sigmoid(x)


def _mod_kernel(c_ref, w_ref, b_ref, o_ref):
    c = c_ref[...]
    o_ref[0] = jnp.dot(_silu(c), w_ref[0], precision=HIGHEST, preferred_element_type=F32) + b_ref[0]


def _modulation(c, ada_w, ada_b):
    b = c.shape[0]
    rows = 8
    c_pad = jnp.zeros((rows, D_MODEL), F32).at[:b].set(c)
    n = 6 * D_MODEL
    tn = 1536
    out = pl.pallas_call(
        _mod_kernel,
        grid=(DEPTH, n // tn),
        in_specs=[
            pl.BlockSpec((rows, D_MODEL), lambda l, j: (0, 0)),
            pl.BlockSpec((1, D_MODEL, tn), lambda l, j: (l, 0, j)),
            pl.BlockSpec((1, 1, tn), lambda l, j: (l, 0, j)),
        ],
        out_specs=pl.BlockSpec((1, rows, tn), lambda l, j: (l, 0, j)),
        out_shape=jax.ShapeDtypeStruct((DEPTH, rows, n), F32),
        compiler_params=_cparams(("arbitrary", "arbitrary")),
        name="adaln_mod",
    )(c_pad, ada_w, ada_b.reshape(DEPTH, 1, n))
    return out[:, :b].reshape(DEPTH, b, 6, 1, D_MODEL)


def _norm_mod(x, g, sc, sh):
    y = x * lax.rsqrt(jnp.mean(x * x, axis=-1, keepdims=True) + EPS)
    return (y * g) * (1.0 + sc) + sh


def _row_spec(width):
    return pl.BlockSpec((TM, width), lambda i: (i, 0))


def _full_spec(shape):
    nd = len(shape)
    return pl.BlockSpec(shape, lambda i: (0,) * nd)


def _batch_spec(seq):
    return pl.BlockSpec((1, 1, D_MODEL), lambda i: ((i * TM) // seq, 0, 0))


def _in_proj_kernel(x_ref, g_ref, sc_ref, sh_ref, w_ref, o_ref):
    h = _norm_mod(x_ref[...], g_ref[...], sc_ref[0], sh_ref[0]).astype(BF16)
    tn = Z_N // 4
    for j in range(4):
        o_ref[:, j * tn:(j + 1) * tn] = jnp.dot(h, w_ref[:, j * tn:(j + 1) * tn], preferred_element_type=F32)


def _in_proj(x2, seq, g, sc, sh, w_bf):
    t = x2.shape[0]
    return pl.pallas_call(
        _in_proj_kernel,
        grid=(t // TM,),
        in_specs=[_row_spec(D_MODEL), _full_spec((1, D_MODEL)), _batch_spec(seq), _batch_spec(seq),
                  _full_spec((D_MODEL, Z_N))],
        out_specs=_row_spec(Z_N),
        out_shape=jax.ShapeDtypeStruct((t, Z_N), F32),
        compiler_params=_cparams(("arbitrary",)),
        name="in_proj",
    )(x2, g, sc, sh, w_bf)


def _out_proj_kernel(x_ref, a_ref, r_ref, gm_ref, w_ref, o_ref):
    y = jnp.dot(a_ref[...].astype(BF16), w_ref[:NSA_W, :], preferred_element_type=F32)
    y = y + jnp.dot(r_ref[...].astype(BF16), w_ref[NSA_W:, :], preferred_element_type=F32)
    o_ref[...] = x_ref[...] + gm_ref[0] * y


def _out_proj(x2, seq, a, r, gm, w_bf):
    t = x2.shape[0]
    return pl.pallas_call(
        _out_proj_kernel,
        grid=(t // TM,),
        in_specs=[_row_spec(D_MODEL), _row_spec(NSA_W), _row_spec(HG_W), _batch_spec(seq),
                  _full_spec((NSA_W + HG_W, D_MODEL))],
        out_specs=_row_spec(D_MODEL),
        out_shape=jax.ShapeDtypeStruct((t, D_MODEL), F32),
        compiler_params=_cparams(("arbitrary",)),
        name="out_proj",
    )(x2, a, r, gm, w_bf)


def _glu_kernel(x_ref, g_ref, sc_ref, sh_ref, w_ref, o_ref):
    h = _norm_mod(x_ref[...], g_ref[...], sc_ref[0], sh_ref[0]).astype(BF16)
    tn = 512
    for j in range(D_MODEL // tn):
        a = jnp.dot(h, w_ref[:, j * tn:(j + 1) * tn], preferred_element_type=F32)
        gate = jnp.dot(h, w_ref[:, D_MODEL + j * tn:D_MODEL + (j + 1) * tn], preferred_element_type=F32)
        o_ref[:, j * tn:(j + 1) * tn] = a * _sigmoid(gate)


def _glu_proj(x2, seq, g, sc, sh, w_bf):
    t = x2.shape[0]
    return pl.pallas_call(
        _glu_kernel,
        grid=(t // TM,),
        in_specs=[_row_spec(D_MODEL), _full_spec((1, D_MODEL)), _batch_spec(seq), _batch_spec(seq),
                  _full_spec((D_MODEL, 2 * D_MODEL))],
        out_specs=_row_spec(D_MODEL),
        out_shape=jax.ShapeDtypeStruct((t, D_MODEL), F32),
        compiler_params=_cparams(("arbitrary",)),
        name="conv_glu",
    )(x2, g, sc, sh, w_bf)


CONV_HALO = 32
CONV_RB = 64


def _conv_kernel(seq, x_ref, u_ref, halo_ref, dw_ref, dwb_ref, lng_ref, lnb_ref, gm_ref, w_ref, o_ref,
                 ext_ref, act_ref):
    i = pl.program_id(0)
    first = (i * TM) % seq == 0
    ext_ref[:CONV_HALO, :] = jnp.where(first, 0.0, halo_ref[...])
    ext_ref[CONV_HALO:, :] = u_ref[...]
    off = CONV_HALO - (CONV_W - 1)
    for r in range(TM // CONV_RB):
        for c in range(D_MODEL // LANES):
            cs = slice(c * LANES, (c + 1) * LANES)
            acc = jnp.zeros((CONV_RB, LANES), F32)
            for k in range(CONV_W):
                acc = acc + ext_ref[r * CONV_RB + off + k:r * CONV_RB + off + k + CONV_RB, cs] * dw_ref[k:k + 1, cs]
            act_ref[r * CONV_RB:(r + 1) * CONV_RB, cs] = acc + dwb_ref[:, cs]
    v = act_ref[...]
    mu = jnp.mean(v, axis=-1, keepdims=True)
    var = jnp.mean(jnp.square(v - mu), axis=-1, keepdims=True)
    y = (v - mu) * lax.rsqrt(var + EPS) * lng_ref[...] + lnb_ref[...]
    y = _silu(y).astype(BF16)
    o_ref[...] = x_ref[...] + gm_ref[0] * jnp.dot(y, w_ref[...], preferred_element_type=F32)


def _conv_module(x2, seq, u, dw, dwb, lng, lnb, gm, w_bf):
    t = x2.shape[0]
    per = TM // CONV_HALO
    return pl.pallas_call(
        functools.partial(_conv_kernel, seq),
        grid=(t // TM,),
        in_specs=[_row_spec(D_MODEL), _row_spec(D_MODEL),
                  pl.BlockSpec((CONV_HALO, D_MODEL), lambda i: (jnp.maximum(i * per - 1, 0), 0)),
                  _full_spec((CONV_HALO, D_MODEL)), _full_spec((1, D_MODEL)), _full_spec((1, D_MODEL)),
                  _full_spec((1, D_MODEL)), _batch_spec(seq), _full_spec((D_MODEL, D_MODEL))],
        out_specs=_row_spec(D_MODEL),
        out_shape=jax.ShapeDtypeStruct((t, D_MODEL), F32),
        scratch_shapes=[pltpu.VMEM((TM + CONV_HALO, D_MODEL), F32), pltpu.VMEM((TM, D_MODEL), F32)],
        compiler_params=_cparams(("arbitrary",)),
        name="conv_module",
    )(x2, u, u, dw, dwb, lng, lnb, gm, w_bf)


def _first_index_of_max(v, lane):
    m = jnp.max(v, axis=-1, keepdims=True)
    idx = jnp.min(jnp.where(v == m, lane, LANES), axis=-1, keepdims=True)
    return m, idx


def _router_kernel(x_ref, g_ref, sc_ref, sh_ref, wr_ref, h_ref, ids_ref, gates_ref, cnt_ref, carry_ref):
    i = pl.program_id(0)

    @pl.when(i == 0)
    def _():
        carry_ref[...] = jnp.zeros_like(carry_ref)

    h = _norm_mod(x_ref[...], g_ref[...], sc_ref[0], sh_ref[0])
    h_ref[...] = h
    logits = jnp.dot(h, wr_ref[...], precision=HIGHEST, preferred_element_type=F32)
    lane = lax.broadcasted_iota(jnp.int32, logits.shape, 1)
    gl = jnp.where(lane < MOE_GROUPS, logits, -jnp.inf)
    gmax, gsel = _first_index_of_max(gl, lane)
    g_gate = 1.0 / jnp.sum(jnp.exp(gl - gmax), axis=-1, keepdims=True)
    lo = MOE_GROUPS + MOE_EPG * gsel
    in_grp = (lane >= lo) & (lane < lo + MOE_EPG)
    el = jnp.where(in_grp, logits, -jnp.inf)
    emax = jnp.max(el, axis=-1, keepdims=True)
    p = jnp.exp(el - emax)
    prob = p / jnp.sum(p, axis=-1, keepdims=True)
    cand = jnp.where(in_grp, prob, -1.0)
    v1, i1 = _first_index_of_max(cand, lane)
    cand2 = jnp.where(lane == i1, -1.0, cand)
    v2, i2 = _first_index_of_max(cand2, lane)
    den = v1 + v2
    w1 = g_gate * v1 / den
    w2 = g_gate * v2 / den
    e1 = i1 - MOE_GROUPS
    e2 = i2 - MOE_GROUPS
    oh1 = (lane == e1).astype(F32)
    oh2 = (lane == e2).astype(F32)
    rows = lax.broadcasted_iota(jnp.int32, (TM, TM), 0)
    cols = lax.broadcasted_iota(jnp.int32, (TM, TM), 1)
    before = (cols < rows).astype(BF16)
    prior = jnp.dot(before, (oh1 + oh2).astype(BF16), preferred_element_type=F32) + carry_ref[...]
    r1 = jnp.sum(oh1 * prior, axis=-1, keepdims=True)
    r2 = jnp.sum(oh2 * prior, axis=-1, keepdims=True)
    carry_ref[...] = carry_ref[...] + jnp.sum(oh1 + oh2, axis=0, keepdims=True)
    cnt_ref[...] = jnp.broadcast_to(carry_ref[...], cnt_ref.shape)
    r1 = r1.astype(jnp.int32)
    r2 = r2.astype(jnp.int32)
    ids_ref[...] = jnp.where(lane == 0, e1, jnp.where(lane == 1, e2, jnp.where(lane == 2, r1, jnp.where(lane == 3, r2, 0))))
    gates_ref[...] = jnp.where(lane == 0, w1, jnp.where(lane == 1, w2, 0.0))


def _router(x2, seq, g, sc, sh, wr):
    t = x2.shape[0]
    return pl.pallas_call(
        _router_kernel,
        grid=(t // TM,),
        in_specs=[_row_spec(D_MODEL), _full_spec((1, D_MODEL)), _batch_spec(seq), _batch_spec(seq),
                  _full_spec((D_MODEL, LANES))],
        out_specs=[_row_spec(D_MODEL), _row_spec(LANES), _row_spec(LANES), _full_spec((8, LANES))],
        out_shape=[jax.ShapeDtypeStruct((t, D_MODEL), F32), jax.ShapeDtypeStruct((t, LANES), jnp.int32),
                   jax.ShapeDtypeStruct((t, LANES), F32), jax.ShapeDtypeStruct((8, LANES), F32)],
        scratch_shapes=[pltpu.VMEM((1, LANES), F32)],
        compiler_params=_cparams(("arbitrary",)),
        name="moe_router",
    )(x2, g, sc, sh, wr)


def _row_copy(src_ref, src_row, dst_ref, dst_row, sem):
    return pltpu.make_async_copy(src_ref.at[pl.ds(src_row, 1)], dst_ref.at[pl.ds(dst_row, 1)], sem)


def _dispatch_kernel(dest_ref, h_ref, xs_in_ref, xs_ref, buf_ref, sem_ref):
    del xs_in_ref
    i = pl.program_id(0)
    n = pl.num_programs(0)
    slot = i % 2

    def wait_slot(s):
        def body(r, c):
            _row_copy(buf_ref.at[s], r, xs_ref, 0, sem_ref.at[s]).wait()
            _row_copy(buf_ref.at[s], r, xs_ref, 0, sem_ref.at[s]).wait()
            return c
        lax.fori_loop(0, TM, body, 0)

    @pl.when(i >= 2)
    def _():
        wait_slot(slot)

    buf_ref[slot] = h_ref[...]

    def issue(r, c):
        a = 2 * (i * TM + r)
        _row_copy(buf_ref.at[slot], r, xs_ref, dest_ref[a], sem_ref.at[slot]).start()
        _row_copy(buf_ref.at[slot], r, xs_ref, dest_ref[a + 1], sem_ref.at[slot]).start()
        return c
    lax.fori_loop(0, TM, issue, 0)

    @pl.when(i == n - 1)
    def _():
        wait_slot(slot)

        @pl.when(n >= 2)
        def _():
            wait_slot(1 - slot)


def _dispatch(dest, h, n_rows):
    t = h.shape[0]
    xs0 = jnp.zeros((n_rows, D_MODEL), F32)
    return pl.pallas_call(
        _dispatch_kernel,
        grid_spec=pltpu.PrefetchScalarGridSpec(
            num_scalar_prefetch=1,
            grid=(t // TM,),
            in_specs=[pl.BlockSpec((TM, D_MODEL), lambda i, d: (i, 0)), pl.BlockSpec(memory_space=pl.ANY)],
            out_specs=pl.BlockSpec(memory_space=pl.ANY),
            scratch_shapes=[pltpu.VMEM((2, TM, D_MODEL), F32), pltpu.SemaphoreType.DMA((2,))],
        ),
        out_shape=jax.ShapeDtypeStruct((n_rows, D_MODEL), F32),
        input_output_aliases={2: 0},
        compiler_params=_cparams(("arbitrary",)),
        name="moe_dispatch",
    )(dest, h, xs0)


def _expert_kernel(be_ref, nb_ref, xs_ref, w1_ref, w3_ref, w2_ref, y_ref):
    i = pl.program_id(0)

    @pl.when(i < nb_ref[0])
    def _():
        xb = xs_ref[...].astype(BF16)
        a = jnp.dot(xb, w1_ref[0], preferred_element_type=F32)
        b = jnp.dot(xb, w3_ref[0], preferred_element_type=F32)
        hmid = (_silu(a) * b).astype(BF16)
        y_ref[...] = jnp.dot(hmid, w2_ref[0], preferred_element_type=F32)

    @pl.when(i >= nb_ref[0])
    def _():
        y_ref[...] = jnp.zeros_like(y_ref)


def _experts(blk_exp, n_used, xs, w1_bf, w3_bf, w2_bf):
    n_rows = xs.shape[0]
    nb = n_rows // MOE_BM

    def row_map(i, be, nu):
        return (jnp.minimum(i, nu[0] - 1), 0)

    def w_map(i, be, nu):
        return (be[jnp.minimum(i, nu[0] - 1)], 0, 0)

    return pl.pallas_call(
        _expert_kernel,
        grid_spec=pltpu.PrefetchScalarGridSpec(
            num_scalar_prefetch=2,
            grid=(nb,),
            in_specs=[pl.BlockSpec((MOE_BM, D_MODEL), row_map),
                      pl.BlockSpec((1, D_MODEL, EXPERT_FF), w_map),
                      pl.BlockSpec((1, D_MODEL, EXPERT_FF), w_map),
                      pl.BlockSpec((1, EXPERT_FF, D_MODEL), w_map)],
            out_specs=pl.BlockSpec((MOE_BM, D_MODEL), lambda i, be, nu: (i, 0)),
        ),
        out_shape=jax.ShapeDtypeStruct((n_rows, D_MODEL), F32),
        compiler_params=_cparams(("arbitrary",)),
        name="moe_experts",
    )(blk_exp, n_used, xs, w1_bf, w3_bf, w2_bf)


def _combine_kernel(dest_ref, x_ref, gates_ref, gf_ref, y_ref, o_ref, buf_ref, sem_ref):
    i = pl.program_id(0)
    n = pl.num_programs(0)
    slot = i % 2

    def issue(tile, s):
        def body(r, c):
            a = 2 * (tile * TM + r)
            _row_copy(y_ref, dest_ref[a], buf_ref.at[s, 0], r, sem_ref.at[s]).start()
            _row_copy(y_ref, dest_ref[a + 1], buf_ref.at[s, 1], r, sem_ref.at[s]).start()
            return c
        lax.fori_loop(0, TM, body, 0)

    @pl.when(i == 0)
    def _():
        issue(0, 0)

    @pl.when(i + 1 < n)
    def _():
        issue(i + 1, 1 - slot)

    def wait_body(r, c):
        _row_copy(y_ref, 0, buf_ref.at[slot, 0], r, sem_ref.at[slot]).wait()
        _row_copy(y_ref, 0, buf_ref.at[slot, 1], r, sem_ref.at[slot]).wait()
        return c
    lax.fori_loop(0, TM, wait_body, 0)

    w0 = gates_ref[:, 0:1]
    w1 = gates_ref[:, 1:2]
    o_ref[...] = x_ref[...] + gf_ref[0] * (buf_ref[slot, 0] * w0 + buf_ref[slot, 1] * w1)


def _combine(dest, x2, seq, gates, gf, y):
    t = x2.shape[0]
    return pl.pallas_call(
        _combine_kernel,
        grid_spec=pltpu.PrefetchScalarGridSpec(
            num_scalar_prefetch=1,
            grid=(t // TM,),
            in_specs=[pl.BlockSpec((TM, D_MODEL), lambda i, d: (i, 0)),
                      pl.BlockSpec((TM, LANES), lambda i, d: (i, 0)),
                      pl.BlockSpec((1, 1, D_MODEL), lambda i, d: ((i * TM) // seq, 0, 0)),
                      pl.BlockSpec(memory_space=pl.ANY)],
            out_specs=pl.BlockSpec((TM, D_MODEL), lambda i, d: (i, 0)),
            scratch_shapes=[pltpu.VMEM((2, 2, TM, D_MODEL), F32), pltpu.SemaphoreType.DMA((2,))],
        ),
        out_shape=jax.ShapeDtypeStruct((t, D_MODEL), F32),
        compiler_params=_cparams(("arbitrary",)),
        name="moe_combine",
    )(dest, x2, gates, gf, y)


def _moe_layer(x2, seq, g, sc, sh, gf, wr, w1_bf, w3_bf, w2_bf):
    t = x2.shape[0]
    h, ids, gates, cnt = _router(x2, seq, g, sc, sh, wr)
    counts = cnt[0, :MOE_EXPERTS].astype(jnp.int32)
    padded = (counts + MOE_BM - 1) // MOE_BM * MOE_BM
    pends = jnp.cumsum(padded)
    pstarts = pends - padded
    dest = (pstarts[ids[:, 0:2]] + ids[:, 2:4]).reshape(2 * t)
    n_rows = 2 * t + MOE_EXPERTS * MOE_BM
    nb = n_rows // MOE_BM
    blk_exp = jnp.minimum(jnp.searchsorted(pends, jnp.arange(nb, dtype=jnp.int32) * MOE_BM, side='right'),
                          MOE_EXPERTS - 1).astype(jnp.int32)
    n_used = (pends[-1:] // MOE_BM).astype(jnp.int32)
    xs = _dispatch(dest, h, n_rows)
    y = _experts(blk_exp, n_used, xs, w1_bf, w3_bf, w2_bf)
    return _combine(dest, x2, seq, gates, gf, y)


HG_LT = 256


def _sublane_roll(x, shift):
    return pltpu.roll(x, shift % x.shape[0], 0)


def _hgrn_chunk(hq, hf, hi, lb, state_t):
    c, sub = HG_CHUNK, HG_SUB
    q = _silu(hq)
    f = lb + (1.0 - lb) * _sigmoid(hf)
    lf = jnp.log(f)
    k = 1.0 - f
    v = hi
    row = lax.broadcasted_iota(jnp.int32, (c, c), 0)
    col = lax.broadcasted_iota(jnp.int32, (c, c), 1)
    start = (row // sub) * sub
    tri_local = ((col >= start) & (col <= row)).astype(F32)
    tri_before = (col < start).astype(F32)
    bloc = jnp.dot(tri_local, lf, precision=HIGHEST, preferred_element_type=F32)
    rref = jnp.dot(tri_before, lf, precision=HIGHEST, preferred_element_type=F32)
    b = rref + bloc
    qt = q * jnp.exp(bloc)
    ridx = lax.broadcasted_iota(jnp.int32, (c, HG_DK), 0)

    blocks = [jnp.zeros((sub, c), F32)]
    for i in range(1, c // sub):
        r_i = rref[i * sub:i * sub + 1, :]
        live = ridx < i * sub
        k_i = jnp.where(live, k * jnp.exp(jnp.where(live, r_i - b, 0.0)), 0.0)
        blocks.append(lax.dot_general(qt[i * sub:(i + 1) * sub].astype(BF16), k_i.astype(BF16),
                                      (((1,), (1,)), ((), ())), preferred_element_type=F32))
    attn_off = jnp.concatenate(blocks, axis=0)
    o = jnp.dot(attn_off.astype(BF16), v.astype(BF16), preferred_element_type=F32)

    pos = ridx % sub
    for d in range(sub):
        qd = q if d == 0 else _sublane_roll(q, -d)
        bd = bloc if d == 0 else _sublane_roll(bloc, -d)
        ok = pos + d < sub
        e = qd * k * jnp.exp(jnp.where(ok, bd - bloc, -jnp.inf))
        w = jnp.sum(e, axis=-1, keepdims=True) * v
        o = o + (w if d == 0 else _sublane_roll(w, d))

    qe = qt * jnp.exp(rref)
    o = o + lax.dot_general(qe.astype(BF16), state_t.astype(BF16), (((1,), (1,)), ((), ())),
                            preferred_element_type=F32)
    b_end = b[c - 1:c, :]
    kd = k * jnp.exp(b_end - b)
    new_state = state_t * jnp.exp(b_end) + lax.dot_general(v.astype(BF16), kd.astype(BF16), (((0,), (0,)), ((), ())),
                                                           preferred_element_type=F32)
    return o, new_state


def _hgrn_kernel(hq_ref, hf_ref, hi_ref, hg_ref, lb_ref, og_ref, o_ref, state_ref):
    @pl.when(pl.program_id(2) == 0)
    def _():
        state_ref[...] = jnp.zeros_like(state_ref)

    lb = lb_ref[...]
    og = og_ref[...]

    def body(ci, carry):
        rows = pl.ds(pl.multiple_of(ci * HG_CHUNK, HG_CHUNK), HG_CHUNK)
        o, st = _hgrn_chunk(hq_ref[rows, :], hf_ref[rows, :], hi_ref[rows, :], lb, state_ref[...])
        state_ref[...] = st
        on = o * lax.rsqrt(jnp.mean(o * o, axis=-1, keepdims=True) + EPS) * og
        o_ref[rows, :] = on * _silu(hg_ref[rows, :])
        return carry
    lax.fori_loop(0, HG_LT // HG_CHUNK, body, 0)


def _hgrn(z, batch, seq, lb, o_gain):
    t = z.shape[0]
    per = seq // HG_LT

    def col(base):
        return pl.BlockSpec((HG_LT, HG_DK), lambda b, h, l: (b * per + l, base // HG_DK + h))

    return pl.pallas_call(
        _hgrn_kernel,
        grid=(batch, HG_HEADS, per),
        in_specs=[col(Z_HQ), col(Z_HF), col(Z_HI), col(Z_HG),
                  pl.BlockSpec((1, HG_DK), lambda b, h, l: (0, h)),
                  pl.BlockSpec((1, HG_DV), lambda b, h, l: (0, 0))],
        out_specs=pl.BlockSpec((HG_LT, HG_DV), lambda b, h, l: (b * per + l, h)),
        out_shape=jax.ShapeDtypeStruct((t, HG_W), F32),
        scratch_shapes=[pltpu.VMEM((HG_DV, HG_DK), F32)],
        compiler_params=_cparams(("arbitrary", "arbitrary", "arbitrary")),
        name="hgrn2",
    )(z, z, z, z, lb.reshape(1, -1), o_gain.reshape(1, -1))


def _rms64(v, gain):
    return v * lax.rsqrt(jnp.mean(v * v, axis=-1, keepdims=True) + EPS) * gain


def _nsa_prep_kernel(kc_ref, vc_ref, ks_ref, vs_ref, kw_ref, vw_ref, kg_ref,
                     kcr_ref, vcr_ref, ksn_ref, vsb_ref, kwn_ref, vwb_ref):
    for g in range(NSA_KV_GROUPS):
        cs = slice(g * NSA_HD, (g + 1) * NSA_HD)
        kcr_ref[0, g] = kc_ref[:, cs]
        vcr_ref[0, g] = vc_ref[:, cs]
        ksn_ref[0, g] = _rms64(ks_ref[:, cs], kg_ref[1:2, :]).astype(BF16)
        vsb_ref[0, g] = vs_ref[:, cs].astype(BF16)
        kwn_ref[0, g] = _rms64(kw_ref[:, cs], kg_ref[2:3, :]).astype(BF16)
        vwb_ref[0, g] = vw_ref[:, cs].astype(BF16)


def _nsa_prep(z, batch, seq, k_gain):
    per = seq // TM

    def col(base):
        return pl.BlockSpec((TM, NSA_KV_W), lambda i: (i, base // NSA_KV_W))

    def out(dtype):
        return (pl.BlockSpec((1, NSA_KV_GROUPS, TM, NSA_HD), lambda i: (i // per, 0, i % per, 0)),
                jax.ShapeDtypeStruct((batch, NSA_KV_GROUPS, seq, NSA_HD), dtype))

    outs = [out(F32), out(F32), out(BF16), out(BF16), out(BF16), out(BF16)]
    return pl.pallas_call(
        _nsa_prep_kernel,
        grid=(batch * per,),
        in_specs=[col(Z_KC), col(Z_VC), col(Z_KS), col(Z_VS), col(Z_KW), col(Z_VW), _full_spec((3, NSA_HD))],
        out_specs=[o[0] for o in outs],
        out_shape=[o[1] for o in outs],
        compiler_params=_cparams(("arbitrary",)),
        name="nsa_prep",
    )(z, z, z, z, z, z, k_gain)


def _nsa_compress_kernel(ak_ref, av_ref, wk_ref, wv_ref, pe_ref, kg_ref, kc_ref, vc_ref):
    half = CMP_STRIDE * NSA_HD

    def compress(a, w_ref):
        top = jnp.dot(a, w_ref[:half, :], precision=HIGHEST, preferred_element_type=F32)
        bot = jnp.dot(a, w_ref[half:, :], precision=HIGHEST, preferred_element_type=F32)
        pe = jnp.dot(pe_ref[...], w_ref[...], precision=HIGHEST, preferred_element_type=F32)
        return top + _sublane_roll(bot, -1) + pe[0:1]

    kc_ref[0, 0] = _rms64(compress(ak_ref[0, 0], wk_ref), kg_ref[0:1, :])
    vc_ref[0, 0] = compress(av_ref[0, 0], wv_ref)


def _nsa_compress(kcr, vcr, w_ck, w_cv, cmp_pe, k_gain):
    batch, groups, seq, _ = kcr.shape
    nc = seq // CMP_STRIDE
    wide = CMP_STRIDE * NSA_HD
    spec = pl.BlockSpec((1, 1, nc, wide), lambda b, g: (b, g, 0, 0))
    ospec = pl.BlockSpec((1, 1, nc, NSA_HD), lambda b, g: (b, g, 0, 0))
    oshape = jax.ShapeDtypeStruct((batch, groups, nc, NSA_HD), F32)
    full2 = lambda shape: pl.BlockSpec(shape, lambda b, g: (0, 0))
    return pl.pallas_call(
        _nsa_compress_kernel,
        grid=(batch, groups),
        in_specs=[spec, spec, full2((2 * wide, NSA_HD)), full2((2 * wide, NSA_HD)), full2((8, 2 * wide)),
                  full2((3, NSA_HD))],
        out_specs=[ospec, ospec],
        out_shape=[oshape, oshape],
        compiler_params=_cparams(("arbitrary", "arbitrary")),
        name="nsa_compress",
    )(kcr.reshape(batch, groups, nc, wide), vcr.reshape(batch, groups, nc, wide), w_ck, w_cv,
      jnp.broadcast_to(cmp_pe.reshape(1, 2 * wide), (8, 2 * wide)), k_gain)


def _masked_softmax(s, valid):
    sm = jnp.where(valid, s, NEG)
    e = jnp.exp(sm - jnp.max(sm, axis=-1, keepdims=True))
    return e / jnp.sum(e, axis=-1, keepdims=True) * valid.astype(F32)


def _nsa_kernel(seq, kchunk, q_ref, gate_ref, kc_ref, vc_ref, ks_ref, vs_ref, kw_ref, vw_ref, qg_ref, agg_ref,
                o_ref):
    qi = pl.program_id(2)
    t0 = qi * Q_BLOCK
    jn = NSA_QPG
    nc = seq // CMP_STRIDE
    n_slc = seq // SLC_BLOCK
    n_sel = min(N_SELECT, n_slc)
    rows = jn * Q_BLOCK

    qraw = q_ref[...]
    qn = jnp.concatenate([_rms64(qraw[:, j * NSA_HD:(j + 1) * NSA_HD], qg_ref[...]) for j in range(jn)], axis=0)
    qs = qn * (NSA_HD ** -0.5)
    qb = qs.astype(BF16)

    def pos_of(shape):
        return t0 + lax.broadcasted_iota(jnp.int32, shape, 0)

    def tile_heads(m):
        return jnp.concatenate([m] * jn, axis=0)

    s_c = lax.dot_general(qs, kc_ref[0, 0], (((1,), (1,)), ((), ())), precision=HIGHEST, preferred_element_type=F32)
    cmp_end = lax.broadcasted_iota(jnp.int32, (Q_BLOCK, nc), 1) * CMP_STRIDE + (CMP_BLOCK - 1)
    valid_c = cmp_end <= pos_of((Q_BLOCK, nc))
    p_c = _masked_softmax(s_c, tile_heads(valid_c))
    o_c = jnp.dot(p_c.astype(BF16), vc_ref[0, 0].astype(BF16), preferred_element_type=F32)

    imp = p_c[0:Q_BLOCK]
    for j in range(1, jn):
        imp = imp + p_c[j * Q_BLOCK:(j + 1) * Q_BLOCK]
    imp_s = jnp.dot(imp, agg_ref[...], precision=HIGHEST, preferred_element_type=F32)
    ids = lax.broadcasted_iota(jnp.int32, (Q_BLOCK, n_slc), 1)
    q_blk = pos_of((Q_BLOCK, n_slc)) // SLC_BLOCK
    forced = (ids == 0) | (ids == q_blk) | (ids == q_blk - 1)
    score = jnp.where(ids > q_blk, NEG, jnp.where(forced, FORCE, imp_s))
    cur = score
    sel = jnp.zeros((Q_BLOCK, n_slc), F32)
    for _ in range(n_sel):
        m = jnp.max(cur, axis=-1, keepdims=True)
        first = jnp.min(jnp.where(cur == m, ids, n_slc), axis=-1, keepdims=True)
        hit = ids == first
        sel = jnp.where(hit, 1.0, sel)
        cur = jnp.where(hit, -jnp.inf, cur)
    sel = jnp.where(score > NEG / 2, sel, 0.0).astype(BF16)

    per_chunk = kchunk // SLC_BLOCK
    blk_of_key = lax.broadcasted_iota(jnp.int32, (n_slc, kchunk), 1) // SLC_BLOCK
    blk_row = lax.broadcasted_iota(jnp.int32, (n_slc, kchunk), 0)
    key_lane = lax.broadcasted_iota(jnp.int32, (Q_BLOCK, kchunk), 1)
    qpos = pos_of((Q_BLOCK, kchunk))

    def sel_body(c, carry):
        m_run, l_run, acc = carry
        ksl = pl.ds(pl.multiple_of(c * kchunk, kchunk), kchunk)
        s = lax.dot_general(qb, ks_ref[0, 0, ksl, :], (((1,), (1,)), ((), ())), preferred_element_type=F32)
        expand = (blk_row == blk_of_key + c * per_chunk).astype(BF16)
        chosen = jnp.dot(sel, expand, preferred_element_type=F32)
        ok = (chosen > 0.5) & (key_lane + c * kchunk <= qpos)
        s = s + tile_heads(jnp.where(ok, 0.0, NEG))
        m_new = jnp.maximum(m_run, jnp.max(s, axis=-1, keepdims=True))
        alpha = jnp.exp(m_run - m_new)
        p = jnp.exp(s - m_new)
        l_new = l_run * alpha + jnp.sum(p, axis=-1, keepdims=True)
        acc_new = acc * alpha + jnp.dot(p.astype(BF16), vs_ref[0, 0, ksl, :], preferred_element_type=F32)
        return m_new, l_new, acc_new

    n_chunks = (t0 + Q_BLOCK + kchunk - 1) // kchunk
    init = (jnp.full((rows, 1), NEG, F32), jnp.zeros((rows, 1), F32), jnp.zeros((rows, NSA_HD), F32))
    _, l_fin, acc_fin = lax.fori_loop(0, n_chunks, sel_body, init)
    o_s = acc_fin / l_fin

    span = Q_BLOCK + WINDOW
    start = pl.multiple_of(jnp.maximum(t0 - WINDOW, 0), Q_BLOCK)
    wsl = pl.ds(start, span)
    s_w = lax.dot_general(qb, kw_ref[0, 0, wsl, :], (((1,), (1,)), ((), ())), preferred_element_type=F32)
    kpos = start + lax.broadcasted_iota(jnp.int32, (Q_BLOCK, span), 1)
    wpos = pos_of((Q_BLOCK, span))
    valid_w = (kpos <= wpos) & (kpos > wpos - WINDOW)
    p_w = _masked_softmax(s_w, tile_heads(valid_w))
    o_w = jnp.dot(p_w.astype(BF16), vw_ref[0, 0, wsl, :], preferred_element_type=F32)

    gates = _sigmoid(gate_ref[...])
    outs = []
    for j in range(jn):
        rs = slice(j * Q_BLOCK, (j + 1) * Q_BLOCK)
        outs.append(gates[:, 3 * j:3 * j + 1] * o_c[rs] + gates[:, 3 * j + 1:3 * j + 2] * o_s[rs]
                    + gates[:, 3 * j + 2:3 * j + 3] * o_w[rs])
    o_ref[...] = jnp.concatenate(outs, axis=1)


def _selection_weights(seq):
    nc = seq // CMP_STRIDE
    n_slc = seq // SLC_BLOCK
    ratio = SLC_BLOCK // CMP_STRIDE
    n = jnp.arange(nc)[:, None]
    j = jnp.arange(n_slc)[None, :]
    o = n - ratio * j + (CMP_BLOCK // CMP_STRIDE - 1)
    w = jnp.asarray(SEL_AGG_W, F32)
    return jnp.where((o >= 0) & (o < len(SEL_AGG_W)), w[jnp.clip(o, 0, len(SEL_AGG_W) - 1)], 0.0)


def _nsa(z, batch, seq, w_ck, w_cv, cmp_pe, q_gain, k_gain):
    t = z.shape[0]
    kcr, vcr, ksn, vsb, kwn, vwb = _nsa_prep(z, batch, seq, k_gain)
    kc, vc = _nsa_compress(kcr, vcr, w_ck, w_cv, cmp_pe, k_gain)
    nq = seq // Q_BLOCK
    nc = seq // CMP_STRIDE
    n_slc = seq // SLC_BLOCK
    kchunk = min(512, seq)
    qw = NSA_QPG * NSA_HD
    cmp_spec = pl.BlockSpec((1, 1, nc, NSA_HD), lambda b, g, i: (b, g, 0, 0))
    seq_spec = pl.BlockSpec((1, 1, seq, NSA_HD), lambda b, g, i: (b, g, 0, 0))
    return pl.pallas_call(
        functools.partial(_nsa_kernel, seq, kchunk),
        grid=(batch, NSA_KV_GROUPS, nq),
        in_specs=[pl.BlockSpec((Q_BLOCK, qw), lambda b, g, i: (b * nq + i, g)),
                  pl.BlockSpec((Q_BLOCK, LANES), lambda b, g, i: (b * nq + i, Z_GATE // LANES + g)),
                  cmp_spec, cmp_spec, seq_spec, seq_spec, seq_spec, seq_spec,
                  pl.BlockSpec((1, NSA_HD), lambda b, g, i: (0, 0)),
                  pl.BlockSpec((nc, n_slc), lambda b, g, i: (0, 0))],
        out_specs=pl.BlockSpec((Q_BLOCK, qw), lambda b, g, i: (b * nq + i, g)),
        out_shape=jax.ShapeDtypeStruct((t, NSA_W), F32),
        compiler_params=_cparams(("arbitrary", "arbitrary", "arbitrary")),
        name="nsa_attention",
    )(z, z, kc, vc, ksn, vsb, kwn, vwb, q_gain.reshape(1, -1), _selection_weights(seq))


def _pad_in_proj(w_in):
    per_group = NSA_QPG * 3
    gate_lo = Z_GATE
    out = jnp.zeros((D_MODEL, Z_N), F32)
    out = out.at[:, :gate_lo].set(w_in[:, :gate_lo])
    for g in range(NSA_KV_GROUPS):
        out = out.at[:, gate_lo + g * LANES:gate_lo + g * LANES + per_group].set(
            w_in[:, gate_lo + g * per_group:gate_lo + (g + 1) * per_group])
    return out.at[:, Z_HQ:].set(w_in[:, gate_lo + NSA_HEADS * 3:])


def _even_mixer(x2, batch, seq, g, sc, sh, gm, w_in, w_out, w_ck, w_cv, cmp_pe, q_gain, k_gain, lb, o_gain):
    z = _in_proj(x2, seq, g, sc, sh, _pad_in_proj(w_in).astype(BF16))
    r = _hgrn(z, batch, seq, lb, o_gain)
    a = _nsa(z, batch, seq, w_ck, w_cv, cmp_pe, q_gain, k_gain)
    return _out_proj(x2, seq, a, r, gm, w_out.astype(BF16))


def _odd_mixer(x2, seq, g, sc, sh, gm, w_pw1, dw, dw_b, ln_g, ln_b, w_pw2):
    u = _glu_proj(x2, seq, g, sc, sh, w_pw1.astype(BF16))
    dw_pad = jnp.zeros((CONV_HALO, D_MODEL), F32).at[:CONV_W].set(dw)
    return _conv_module(x2, seq, u, dw_pad, dw_b.reshape(1, -1), ln_g.reshape(1, -1), ln_b.reshape(1, -1), gm,
                        w_pw2.astype(BF16))


def _router_weights(w_group, w_expert):
    wr = jnp.zeros((D_MODEL, LANES), F32)
    return wr.at[:, :MOE_GROUPS].set(w_group).at[:, MOE_GROUPS:MOE_GROUPS + MOE_EXPERTS].set(w_expert)


def _lower_bounds_kernel(l_ref, o_ref):
    logits = l_ref[...]
    e = jnp.exp(logits - jnp.max(logits, axis=0, keepdims=True))
    p = e / jnp.sum(e, axis=0, keepdims=True)
    n = logits.shape[0]
    acc = jnp.zeros_like(p[0:1])
    for i in range(n):
        acc = acc + p[i:i + 1]
        o_ref[i:i + 1, :] = acc - p[0:1]


def _lower_bounds(lb_logits):
    return pl.pallas_call(
        _lower_bounds_kernel,
        out_shape=jax.ShapeDtypeStruct(lb_logits.shape, F32),
        name="hgrn_lower_bounds",
    )(lb_logits)


def kernel(x, c, ada_w, ada_b, norm_mix, norm_ffn, mix_w_in, mix_w_out, nsa_cmp_wk, nsa_cmp_wv, nsa_cmp_pe, nsa_q_gain, nsa_k_gain, hgrn_lb_logits, hgrn_o_gain, conv_w_pw1, conv_dw, conv_dw_b, conv_ln_g, conv_ln_b, conv_w_pw2, moe_w_group, moe_w_expert, moe_w1, moe_w3, moe_w2):
    batch, seq, d = x.shape
    x2 = x.reshape(batch * seq, d)
    mod = _modulation(c, ada_w, ada_b)
    lower_bounds = _lower_bounds(hgrn_lb_logits)
    w1_bf = moe_w1.astype(BF16)
    w3_bf = moe_w3.astype(BF16)
    w2_bf = moe_w2.astype(BF16)
    for layer in range(DEPTH):
        sh_m, sc_m, g_m, sh_f, sc_f, g_f = (mod[layer, :, k] for k in range(6))
        i = layer // 2
        gain = norm_mix[layer].reshape(1, d)
        if layer % 2 == 0:
            x2 = _even_mixer(x2, batch, seq, gain, sc_m, sh_m, g_m, mix_w_in[i], mix_w_out[i], nsa_cmp_wk[i],
                             nsa_cmp_wv[i], nsa_cmp_pe[i], nsa_q_gain[i], nsa_k_gain[i], lower_bounds[i],
                             hgrn_o_gain[i])
        else:
            x2 = _odd_mixer(x2, seq, gain, sc_m, sh_m, g_m, conv_w_pw1[i], conv_dw[i], conv_dw_b[i], conv_ln_g[i],
                            conv_ln_b[i], conv_w_pw2[i])
        x2 = _moe_layer(x2, seq, norm_ffn[layer].reshape(1, d), sc_f, sh_f, g_f,
                        _router_weights(moe_w_group[layer], moe_w_expert[layer]), w1_bf[layer], w3_bf[layer],
                        w2_bf[layer])
    return x2.reshape(batch, seq, d)
```

```python
import functools

import jax
import jax.numpy as jnp
from jax import lax
from jax.experimental import pallas as pl
from jax.experimental.pallas import tpu as pltpu

F32 = jnp.float32
BF16 = jnp.bfloat16
HIGHEST = lax.Precision.HIGHEST

D_MODEL = 1024
DEPTH = 4

NSA_HEADS = 8
NSA_KV_GROUPS = 2
NSA_QPG = NSA_HEADS // NSA_KV_GROUPS
NSA_HD = 64
NSA_W = NSA_HEADS * NSA_HD
NSA_KV_W = NSA_KV_GROUPS * NSA_HD
CMP_BLOCK = 32
CMP_STRIDE = 16
SLC_BLOCK = 64
N_SELECT = 16
WINDOW = 512
Q_BLOCK = 128
SEL_AGG_W = (1.0, 2.0, 2.0, 2.0, 1.0)

HG_HEADS = 4
HG_DK = 128
HG_DV = 128
HG_W = HG_HEADS * HG_DV
HG_CHUNK = 64
HG_SUB = 8

CONV_W = 31
MOE_GROUPS = 4
MOE_EPG = 8
MOE_EXPERTS = MOE_GROUPS * MOE_EPG
EXPERT_FF = 512

EPS = 1e-6
NEG = -1e30
FORCE = 1e4

LANES = 128
VMEM_LIMIT = 56 * 1024 * 1024

Z_Q = 0
Z_KC = 512
Z_VC = 640
Z_KS = 768
Z_VS = 896
Z_KW = 1024
Z_VW = 1152
Z_GATE = 1280
Z_HQ = Z_GATE + NSA_KV_GROUPS * LANES
Z_HF = Z_HQ + HG_W
Z_HI = Z_HF + HG_W
Z_HG = Z_HI + HG_W
Z_N = Z_HG + HG_W

TM = 256
MOE_BM = 256


def _cparams(sem):
    return pltpu.CompilerParams(dimension_semantics=sem, vmem_limit_bytes=VMEM_LIMIT)


def _sigmoid(x):
    return 1.0 / (1.0 + jnp.exp(-x))


def _silu(x):
    return x * _sigmoid(x)


def _mod_kernel(c_ref, w_ref, b_ref, o_ref):
    c = c_ref[...]
    o_ref[0] = jnp.dot(_silu(c), w_ref[0], precision=HIGHEST, preferred_element_type=F32) + b_ref[0]


def _modulation(c, ada_w, ada_b):
    b = c.shape[0]
    rows = 8
    c_pad = jnp.zeros((rows, D_MODEL), F32).at[:b].set(c)
    n = 6 * D_MODEL
    tn = 1536
    out = pl.pallas_call(
        _mod_kernel,
        grid=(DEPTH, n // tn),
        in_specs=[
            pl.BlockSpec((rows, D_MODEL), lambda l, j: (0, 0)),
            pl.BlockSpec((1, D_MODEL, tn), lambda l, j: (l, 0, j)),
            pl.BlockSpec((1, 1, tn), lambda l, j: (l, 0, j)),
        ],
        out_specs=pl.BlockSpec((1, rows, tn), lambda l, j: (l, 0, j)),
        out_shape=jax.ShapeDtypeStruct((DEPTH, rows, n), F32),
        compiler_params=_cparams(("arbitrary", "arbitrary")),
        name="adaln_mod",
    )(c_pad, ada_w, ada_b.reshape(DEPTH, 1, n))
    return out[:, :b].reshape(DEPTH, b, 6, 1, D_MODEL)


def _norm_mod(x, g, sc, sh):
    y = x * lax.rsqrt(jnp.mean(x * x, axis=-1, keepdims=True) + EPS)
    return (y * g) * (1.0 + sc) + sh


def _row_spec(width):
    return pl.BlockSpec((TM, width), lambda i: (i, 0))


def _full_spec(shape):
    nd = len(shape)
    return pl.BlockSpec(shape, lambda i: (0,) * nd)


def _batch_spec(seq):
    return pl.BlockSpec((1, 1, D_MODEL), lambda i: ((i * TM) // seq, 0, 0))


def _in_proj_kernel(x_ref, g_ref, sc_ref, sh_ref, w_ref, o_ref):
    h = _norm_mod(x_ref[...], g_ref[...], sc_ref[0], sh_ref[0]).astype(BF16)
    tn = Z_N // 4
    for j in range(4):
        o_ref[:, j * tn:(j + 1) * tn] = jnp.dot(h, w_ref[:, j * tn:(j + 1) * tn], preferred_element_type=F32)


def _in_proj(x2, seq, g, sc, sh, w_bf):
    t = x2.shape[0]
    return pl.pallas_call(
        _in_proj_kernel,
        grid=(t // TM,),
        in_specs=[_row_spec(D_MODEL), _full_spec((1, D_MODEL)), _batch_spec(seq), _batch_spec(seq),
                  _full_spec((D_MODEL, Z_N))],
        out_specs=_row_spec(Z_N),
        out_shape=jax.ShapeDtypeStruct((t, Z_N), F32),
        compiler_params=_cparams(("arbitrary",)),
        name="in_proj",
    )(x2, g, sc, sh, w_bf)


def _out_proj_kernel(x_ref, a_ref, r_ref, gm_ref, w_ref, o_ref):
    y = jnp.dot(a_ref[...].astype(BF16), w_ref[:NSA_W, :], preferred_element_type=F32)
    y = y + jnp.dot(r_ref[...].astype(BF16), w_ref[NSA_W:, :], preferred_element_type=F32)
    o_ref[...] = x_ref[...] + gm_ref[0] * y


def _out_proj(x2, seq, a, r, gm, w_bf):
    t = x2.shape[0]
    return pl.pallas_call(
        _out_proj_kernel,
        grid=(t // TM,),
        in_specs=[_row_spec(D_MODEL), _row_spec(NSA_W), _row_spec(HG_W), _batch_spec(seq),
                  _full_spec((NSA_W + HG_W, D_MODEL))],
        out_specs=_row_spec(D_MODEL),
        out_shape=jax.ShapeDtypeStruct((t, D_MODEL), F32),
        compiler_params=_cparams(("arbitrary",)),
        name="out_proj",
    )(x2, a, r, gm, w_bf)


def _glu_kernel(x_ref, g_ref, sc_ref, sh_ref, w_ref, o_ref):
    h = _norm_mod(x_ref[...], g_ref[...], sc_ref[0], sh_ref[0]).astype(BF16)
    tn = 512
    for j in range(D_MODEL // tn):
        a = jnp.dot(h, w_ref[:, j * tn:(j + 1) * tn], preferred_element_type=F32)
        gate = jnp.dot(h, w_ref[:, D_MODEL + j * tn:D_MODEL + (j + 1) * tn], preferred_element_type=F32)
        o_ref[:, j * tn:(j + 1) * tn] = a * _sigmoid(gate)


def _glu_proj(x2, seq, g, sc, sh, w_bf):
    t = x2.shape[0]
    return pl.pallas_call(
        _glu_kernel,
        grid=(t // TM,),
        in_specs=[_row_spec(D_MODEL), _full_spec((1, D_MODEL)), _batch_spec(seq), _batch_spec(seq),
                  _full_spec((D_MODEL, 2 * D_MODEL))],
        out_specs=_row_spec(D_MODEL),
        out_shape=jax.ShapeDtypeStruct((t, D_MODEL), F32),
        compiler_params=_cparams(("arbitrary",)),
        name="conv_glu",
    )(x2, g, sc, sh, w_bf)


CONV_HALO = 32
CONV_RB = 64
SUBLANES = 8
CONV_SHIFT_ROWS = TM + CONV_HALO - SUBLANES


def _conv_kernel(seq, x_ref, u_ref, halo_ref, dw_ref, dwb_ref, lng_ref, lnb_ref, gm_ref, w_ref, o_ref,
                 ext_ref, shift_ref, act_ref):
    i = pl.program_id(0)
    first = (i * TM) % seq == 0
    ext_ref[:CONV_HALO, :] = jnp.where(first, 0.0, halo_ref[...])
    ext_ref[CONV_HALO:, :] = u_ref[...]
    for s in range(1, SUBLANES):
        shift_ref[s - 1] = ext_ref[s:s + CONV_SHIFT_ROWS, :]
    off = CONV_HALO - (CONV_W - 1)
    for r in range(TM // CONV_RB):
        for c in range(D_MODEL // LANES):
            cs = slice(c * LANES, (c + 1) * LANES)
            acc = jnp.zeros((CONV_RB, LANES), F32)
            for k in range(CONV_W):
                s = (off + k) % SUBLANES
                lo = r * CONV_RB + off + k - s
                win = ext_ref[lo:lo + CONV_RB, cs] if s == 0 else shift_ref[s - 1, lo:lo + CONV_RB, cs]
                acc = acc + win * dw_ref[k:k + 1, cs]
            act_ref[r * CONV_RB:(r + 1) * CONV_RB, cs] = acc + dwb_ref[:, cs]
    v = act_ref[...]
    mu = jnp.mean(v, axis=-1, keepdims=True)
    var = jnp.mean(jnp.square(v - mu), axis=-1, keepdims=True)
    y = (v - mu) * lax.rsqrt(var + EPS) * lng_ref[...] + lnb_ref[...]
    y = _silu(y).astype(BF16)
    o_ref[...] = x_ref[...] + gm_ref[0] * jnp.dot(y, w_ref[...], preferred_element_type=F32)


def _conv_module(x2, seq, u, dw, dwb, lng, lnb, gm, w_bf):
    t = x2.shape[0]
    per = TM // CONV_HALO
    return pl.pallas_call(
        functools.partial(_conv_kernel, seq),
        grid=(t // TM,),
        in_specs=[_row_spec(D_MODEL), _row_spec(D_MODEL),
                  pl.BlockSpec((CONV_HALO, D_MODEL), lambda i: (jnp.maximum(i * per - 1, 0), 0)),
                  _full_spec((CONV_HALO, D_MODEL)), _full_spec((1, D_MODEL)), _full_spec((1, D_MODEL)),
                  _full_spec((1, D_MODEL)), _batch_spec(seq), _full_spec((D_MODEL, D_MODEL))],
        out_specs=_row_spec(D_MODEL),
        out_shape=jax.ShapeDtypeStruct((t, D_MODEL), F32),
        scratch_shapes=[pltpu.VMEM((TM + CONV_HALO, D_MODEL), F32),
                        pltpu.VMEM((SUBLANES - 1, CONV_SHIFT_ROWS, D_MODEL), F32),
                        pltpu.VMEM((TM, D_MODEL), F32)],
        compiler_params=_cparams(("arbitrary",)),
        name="conv_module",
    )(x2, u, u, dw, dwb, lng, lnb, gm, w_bf)


def _first_index_of_max(v, lane):
    m = jnp.max(v, axis=-1, keepdims=True)
    idx = jnp.min(jnp.where(v == m, lane, LANES), axis=-1, keepdims=True)
    return m, idx


def _router_kernel(x_ref, g_ref, sc_ref, sh_ref, wr_ref, h_ref, ids_ref, gates_ref, cnt_ref, carry_ref):
    i = pl.program_id(0)

    @pl.when(i == 0)
    def _():
        carry_ref[...] = jnp.zeros_like(carry_ref)

    h = _norm_mod(x_ref[...], g_ref[...], sc_ref[0], sh_ref[0])
    h_ref[...] = h
    logits = jnp.dot(h, wr_ref[...], precision=HIGHEST, preferred_element_type=F32)
    lane = lax.broadcasted_iota(jnp.int32, logits.shape, 1)
    gl = jnp.where(lane < MOE_GROUPS, logits, -jnp.inf)
    gmax, gsel = _first_index_of_max(gl, lane)
    g_gate = 1.0 / jnp.sum(jnp.exp(gl - gmax), axis=-1, keepdims=True)
    lo = MOE_GROUPS + MOE_EPG * gsel
    in_grp = (lane >= lo) & (lane < lo + MOE_EPG)
    el = jnp.where(in_grp, logits, -jnp.inf)
    emax = jnp.max(el, axis=-1, keepdims=True)
    p = jnp.exp(el - emax)
    prob = p / jnp.sum(p, axis=-1, keepdims=True)
    cand = jnp.where(in_grp, prob, -1.0)
    v1, i1 = _first_index_of_max(cand, lane)
    cand2 = jnp.where(lane == i1, -1.0, cand)
    v2, i2 = _first_index_of_max(cand2, lane)
    den = v1 + v2
    w1 = g_gate * v1 / den
    w2 = g_gate * v2 / den
    e1 = i1 - MOE_GROUPS
    e2 = i2 - MOE_GROUPS
    oh1 = (lane == e1).astype(F32)
    oh2 = (lane == e2).astype(F32)
    rows = lax.broadcasted_iota(jnp.int32, (TM, TM), 0)
    cols = lax.broadcasted_iota(jnp.int32, (TM, TM), 1)
    before = (cols < rows).astype(BF16)
    prior = jnp.dot(before, (oh1 + oh2).astype(BF16), preferred_element_type=F32) + carry_ref[...]
    r1 = jnp.sum(oh1 * prior, axis=-1, keepdims=True)
    r2 = jnp.sum(oh2 * prior, axis=-1, keepdims=True)
    carry_ref[...] = carry_ref[...] + jnp.sum(oh1 + oh2, axis=0, keepdims=True)
    cnt_ref[...] = jnp.broadcast_to(carry_ref[...], cnt_ref.shape)
    r1 = r1.astype(jnp.int32)
    r2 = r2.astype(jnp.int32)
    ids_ref[...] = jnp.where(lane == 0, e1, jnp.where(lane == 1, e2, jnp.where(lane == 2, r1, jnp.where(lane == 3, r2, 0))))
    gates_ref[...] = jnp.where(lane == 0, w1, jnp.where(lane == 1, w2, 0.0))


def _router(x2, seq, g, sc, sh, wr):
    t = x2.shape[0]
    return pl.pallas_call(
        _router_kernel,
        grid=(t // TM,),
        in_specs=[_row_spec(D_MODEL), _full_spec((1, D_MODEL)), _batch_spec(seq), _batch_spec(seq),
                  _full_spec((D_MODEL, LANES))],
        out_specs=[_row_spec(D_MODEL), _row_spec(LANES), _row_spec(LANES), _full_spec((8, LANES))],
        out_shape=[jax.ShapeDtypeStruct((t, D_MODEL), F32), jax.ShapeDtypeStruct((t, LANES), jnp.int32),
                   jax.ShapeDtypeStruct((t, LANES), F32), jax.ShapeDtypeStruct((8, LANES), F32)],
        scratch_shapes=[pltpu.VMEM((1, LANES), F32)],
        compiler_params=_cparams(("arbitrary",)),
        name="moe_router",
    )(x2, g, sc, sh, wr)


def _row_copy(src_ref, src_row, dst_ref, dst_row, sem):
    return pltpu.make_async_copy(src_ref.at[pl.ds(src_row, 1)], dst_ref.at[pl.ds(dst_row, 1)], sem)


def _dispatch_kernel(dest_ref, h_ref, xs_in_ref, xs_ref, buf_ref, sem_ref):
    del xs_in_ref
    i = pl.program_id(0)
    n = pl.num_programs(0)
    slot = i % 2

    def wait_slot(s):
        def body(r, c):
            _row_copy(buf_ref.at[s], r, xs_ref, 0, sem_ref.at[s]).wait()
            _row_copy(buf_ref.at[s], r, xs_ref, 0, sem_ref.at[s]).wait()
            return c
        lax.fori_loop(0, TM, body, 0)

    @pl.when(i >= 2)
    def _():
        wait_slot(slot)

    buf_ref[slot] = h_ref[...]

    def issue(r, c):
        a = 2 * (i * TM + r)
        _row_copy(buf_ref.at[slot], r, xs_ref, dest_ref[a], sem_ref.at[slot]).start(priority=0)
        _row_copy(buf_ref.at[slot], r, xs_ref, dest_ref[a + 1], sem_ref.at[slot]).start(priority=1)
        return c
    lax.fori_loop(0, TM, issue, 0)

    @pl.when(i == n - 1)
    def _():
        wait_slot(slot)

        @pl.when(n >= 2)
        def _():
            wait_slot(1 - slot)


def _dispatch(dest, h, n_rows):
    t = h.shape[0]
    xs0 = jnp.zeros((n_rows, D_MODEL), F32)
    return pl.pallas_call(
        _dispatch_kernel,
        grid_spec=pltpu.PrefetchScalarGridSpec(
            num_scalar_prefetch=1,
            grid=(t // TM,),
            in_specs=[pl.BlockSpec((TM, D_MODEL), lambda i, d: (i, 0)), pl.BlockSpec(memory_space=pl.ANY)],
            out_specs=pl.BlockSpec(memory_space=pl.ANY),
            scratch_shapes=[pltpu.VMEM((2, TM, D_MODEL), F32), pltpu.SemaphoreType.DMA((2,))],
        ),
        out_shape=jax.ShapeDtypeStruct((n_rows, D_MODEL), F32),
        input_output_aliases={2: 0},
        compiler_params=_cparams(("arbitrary",)),
        name="moe_dispatch",
    )(dest, h, xs0)


def _expert_kernel(be_ref, nb_ref, xs_ref, w1_ref, w3_ref, w2_ref, y_ref, w1b_ref, w3b_ref, w2b_ref):
    i = pl.program_id(0)
    used = i < nb_ref[0]
    prev = be_ref[jnp.maximum(i - 1, 0)]

    @pl.when(used & ((i == 0) | (be_ref[i] != prev)))
    def _():
        w1b_ref[...] = w1_ref[0].astype(BF16)
        w3b_ref[...] = w3_ref[0].astype(BF16)
        w2b_ref[...] = w2_ref[0].astype(BF16)

    @pl.when(used)
    def _():
        xb = xs_ref[...].astype(BF16)
        a = jnp.dot(xb, w1b_ref[...], preferred_element_type=F32)
        b = jnp.dot(xb, w3b_ref[...], preferred_element_type=F32)
        hmid = (_silu(a) * b).astype(BF16)
        y_ref[...] = jnp.dot(hmid, w2b_ref[...], preferred_element_type=F32)

    @pl.when(i >= nb_ref[0])
    def _():
        y_ref[...] = jnp.zeros_like(y_ref)


def _experts(blk_exp, n_used, xs, layer, w1, w3, w2):
    n_rows = xs.shape[0]
    nb = n_rows // MOE_BM

    def row_map(i, be, nu):
        return (jnp.minimum(i, nu[0] - 1), 0)

    def w_map(i, be, nu):
        return (layer, be[jnp.minimum(i, nu[0] - 1)], 0, 0)

    return pl.pallas_call(
        _expert_kernel,
        grid_spec=pltpu.PrefetchScalarGridSpec(
            num_scalar_prefetch=2,
            grid=(nb,),
            in_specs=[pl.BlockSpec((MOE_BM, D_MODEL), row_map),
                      pl.BlockSpec((None, 1, D_MODEL, EXPERT_FF), w_map),
                      pl.BlockSpec((None, 1, D_MODEL, EXPERT_FF), w_map),
                      pl.BlockSpec((None, 1, EXPERT_FF, D_MODEL), w_map)],
            out_specs=pl.BlockSpec((MOE_BM, D_MODEL), lambda i, be, nu: (i, 0)),
            scratch_shapes=[pltpu.VMEM((D_MODEL, EXPERT_FF), BF16), pltpu.VMEM((D_MODEL, EXPERT_FF), BF16),
                            pltpu.VMEM((EXPERT_FF, D_MODEL), BF16)],
        ),
        out_shape=jax.ShapeDtypeStruct((n_rows, D_MODEL), F32),
        compiler_params=_cparams(("arbitrary",)),
        name="moe_experts",
    )(blk_exp, n_used, xs, w1, w3, w2)


def _combine_kernel(dest_ref, x_ref, gates_ref, gf_ref, y_ref, o_ref, buf_ref, sem_ref):
    i = pl.program_id(0)
    n = pl.num_programs(0)
    slot = i % 2

    def issue(tile, s):
        def body(r, c):
            a = 2 * (tile * TM + r)
            _row_copy(y_ref, dest_ref[a], buf_ref.at[s, 0], r, sem_ref.at[s]).start(priority=0)
            _row_copy(y_ref, dest_ref[a + 1], buf_ref.at[s, 1], r, sem_ref.at[s]).start(priority=1)
            return c
        lax.fori_loop(0, TM, body, 0)

    @pl.when(i == 0)
    def _():
        issue(0, 0)

    @pl.when(i + 1 < n)
    def _():
        issue(i + 1, 1 - slot)

    def wait_body(r, c):
        _row_copy(y_ref, 0, buf_ref.at[slot, 0], r, sem_ref.at[slot]).wait()
        _row_copy(y_ref, 0, buf_ref.at[slot, 1], r, sem_ref.at[slot]).wait()
        return c
    lax.fori_loop(0, TM, wait_body, 0)

    w0 = gates_ref[:, 0:1]
    w1 = gates_ref[:, 1:2]
    o_ref[...] = x_ref[...] + gf_ref[0] * (buf_ref[slot, 0] * w0 + buf_ref[slot, 1] * w1)


def _combine(dest, x2, seq, gates, gf, y):
    t = x2.shape[0]
    return pl.pallas_call(
        _combine_kernel,
        grid_spec=pltpu.PrefetchScalarGridSpec(
            num_scalar_prefetch=1,
            grid=(t // TM,),
            in_specs=[pl.BlockSpec((TM, D_MODEL), lambda i, d: (i, 0)),
                      pl.BlockSpec((TM, LANES), lambda i, d: (i, 0)),
                      pl.BlockSpec((1, 1, D_MODEL), lambda i, d: ((i * TM) // seq, 0, 0)),
                      pl.BlockSpec(memory_space=pl.ANY)],
            out_specs=pl.BlockSpec((TM, D_MODEL), lambda i, d: (i, 0)),
            scratch_shapes=[pltpu.VMEM((2, 2, TM, D_MODEL), F32), pltpu.SemaphoreType.DMA((2,))],
        ),
        out_shape=jax.ShapeDtypeStruct((t, D_MODEL), F32),
        compiler_params=_cparams(("arbitrary",)),
        name="moe_combine",
    )(dest, x2, gates, gf, y)


def _moe_layer(x2, seq, g, sc, sh, gf, wr, layer, w1, w3, w2):
    t = x2.shape[0]
    h, ids, gates, cnt = _router(x2, seq, g, sc, sh, wr)
    counts = cnt[0, :MOE_EXPERTS].astype(jnp.int32)
    padded = (counts + MOE_BM - 1) // MOE_BM * MOE_BM
    pends = jnp.cumsum(padded)
    pstarts = pends - padded
    dest = (pstarts[ids[:, 0:2]] + ids[:, 2:4]).reshape(2 * t)
    n_rows = 2 * t + MOE_EXPERTS * MOE_BM
    nb = n_rows // MOE_BM
    blk_start = jnp.arange(nb, dtype=jnp.int32) * MOE_BM
    blk_exp = jnp.minimum(jnp.sum((pends[None, :] <= blk_start[:, None]).astype(jnp.int32), axis=1), MOE_EXPERTS - 1)
    n_used = (pends[-1:] // MOE_BM).astype(jnp.int32)
    xs = _dispatch(dest, h, n_rows)
    y = _experts(blk_exp, n_used, xs, layer, w1, w3, w2)
    return _combine(dest, x2, seq, gates, gf, y)


HG_LT = 256


def _sublane_roll(x, shift):
    return pltpu.roll(x, shift % x.shape[0], 0)


def _hgrn_chunk(hq, hf, hi, lb, state_t):
    c, sub = HG_CHUNK, HG_SUB
    q = _silu(hq)
    f = lb + (1.0 - lb) * _sigmoid(hf)
    lf = jnp.log(f)
    k = 1.0 - f
    v = hi
    row = lax.broadcasted_iota(jnp.int32, (c, c), 0)
    col = lax.broadcasted_iota(jnp.int32, (c, c), 1)
    start = (row // sub) * sub
    tri_local = ((col >= start) & (col <= row)).astype(F32)
    tri_before = (col < start).astype(F32)
    bloc = jnp.dot(tri_local, lf, precision=HIGHEST, preferred_element_type=F32)
    rref = jnp.dot(tri_before, lf, precision=HIGHEST, preferred_element_type=F32)
    b = rref + bloc
    qt = q * jnp.exp(bloc)
    ridx = lax.broadcasted_iota(jnp.int32, (c, HG_DK), 0)

    blocks = [jnp.zeros((sub, c), F32)]
    for i in range(1, c // sub):
        r_i = rref[i * sub:i * sub + 1, :]
        live = ridx < i * sub
        k_i = jnp.where(live, k * jnp.exp(jnp.where(live, r_i - b, 0.0)), 0.0)
        blocks.append(lax.dot_general(qt[i * sub:(i + 1) * sub].astype(BF16), k_i.astype(BF16),
                                      (((1,), (1,)), ((), ())), preferred_element_type=F32))
    attn_off = jnp.concatenate(blocks, axis=0)
    o = jnp.dot(attn_off.astype(BF16), v.astype(BF16), preferred_element_type=F32)

    pos = ridx % sub
    for d in range(sub):
        qd = q if d == 0 else _sublane_roll(q, -d)
        bd = bloc if d == 0 else _sublane_roll(bloc, -d)
        ok = pos + d < sub
        e = qd * k * jnp.exp(jnp.where(ok, bd - bloc, -jnp.inf))
        w = jnp.sum(e, axis=-1, keepdims=True) * v
        o = o + (w if d == 0 else _sublane_roll(w, d))

    qe = qt * jnp.exp(rref)
    o = o + lax.dot_general(qe.astype(BF16), state_t.astype(BF16), (((1,), (1,)), ((), ())),
                            preferred_element_type=F32)
    b_end = b[c - 1:c, :]
    kd = k * jnp.exp(b_end - b)
    new_state = state_t * jnp.exp(b_end) + lax.dot_general(v.astype(BF16), kd.astype(BF16), (((0,), (0,)), ((), ())),
                                                           preferred_element_type=F32)
    return o, new_state


def _hgrn_kernel(hq_ref, hf_ref, hi_ref, hg_ref, lb_ref, og_ref, o_ref, state_ref):
    @pl.when(pl.program_id(0) == 0)
    def _():
        state_ref[...] = jnp.zeros_like(state_ref)

    og = og_ref[...]

    def body(ci, carry):
        rows = pl.ds(pl.multiple_of(ci * HG_CHUNK, HG_CHUNK), HG_CHUNK)
        for b in range(hq_ref.shape[0]):
            for h in range(HG_HEADS):
                cs = slice(h * HG_DK, (h + 1) * HG_DK)
                o, st = _hgrn_chunk(hq_ref[b, rows, cs], hf_ref[b, rows, cs], hi_ref[b, rows, cs], lb_ref[:, cs],
                                    state_ref[b, h])
                state_ref[b, h] = st
                on = o * lax.rsqrt(jnp.mean(o * o, axis=-1, keepdims=True) + EPS) * og
                o_ref[b, rows, cs] = on * _silu(hg_ref[b, rows, cs])
        return carry
    lax.fori_loop(0, HG_LT // HG_CHUNK, body, 0)


def _hgrn(z, batch, seq, lb, o_gain):
    t = z.shape[0]
    z3 = z.reshape(batch, seq, z.shape[1])

    def col(base):
        return pl.BlockSpec((batch, HG_LT, HG_W), lambda l: (0, l, base // HG_W))

    out = pl.pallas_call(
        _hgrn_kernel,
        grid=(seq // HG_LT,),
        in_specs=[col(Z_HQ), col(Z_HF), col(Z_HI), col(Z_HG),
                  pl.BlockSpec((1, HG_W), lambda l: (0, 0)),
                  pl.BlockSpec((1, HG_DV), lambda l: (0, 0))],
        out_specs=pl.BlockSpec((batch, HG_LT, HG_W), lambda l: (0, l, 0)),
        out_shape=jax.ShapeDtypeStruct((batch, seq, HG_W), F32),
        scratch_shapes=[pltpu.VMEM((batch, HG_HEADS, HG_DV, HG_DK), F32)],
        compiler_params=_cparams(("arbitrary",)),
        name="hgrn2",
    )(z3, z3, z3, z3, lb.reshape(1, -1), o_gain.reshape(1, -1))
    return out.reshape(t, HG_W)


def _rms64(v, gain):
    return v * lax.rsqrt(jnp.mean(v * v, axis=-1, keepdims=True) + EPS) * gain


def _nsa_prep_kernel(kc_ref, vc_ref, ks_ref, vs_ref, kw_ref, vw_ref, kg_ref,
                     kcr_ref, vcr_ref, ksn_ref, vsb_ref, kwn_ref, vwb_ref):
    for g in range(NSA_KV_GROUPS):
        cs = slice(g * NSA_HD, (g + 1) * NSA_HD)
        kcr_ref[0, g] = kc_ref[:, cs]
        vcr_ref[0, g] = vc_ref[:, cs]
        ksn_ref[0, g] = _rms64(ks_ref[:, cs], kg_ref[1:2, :]).astype(BF16)
        vsb_ref[0, g] = vs_ref[:, cs].astype(BF16)
        kwn_ref[0, g] = _rms64(kw_ref[:, cs], kg_ref[2:3, :]).astype(BF16)
        vwb_ref[0, g] = vw_ref[:, cs].astype(BF16)


def _nsa_prep(z, batch, seq, k_gain):
    per = seq // TM

    def col(base):
        return pl.BlockSpec((TM, NSA_KV_W), lambda i: (i, base // NSA_KV_W))

    def out(dtype):
        return (pl.BlockSpec((1, NSA_KV_GROUPS, TM, NSA_HD), lambda i: (i // per, 0, i % per, 0)),
                jax.ShapeDtypeStruct((batch, NSA_KV_GROUPS, seq, NSA_HD), dtype))

    outs = [out(F32), out(F32), out(BF16), out(BF16), out(BF16), out(BF16)]
    return pl.pallas_call(
        _nsa_prep_kernel,
        grid=(batch * per,),
        in_specs=[col(Z_KC), col(Z_VC), col(Z_KS), col(Z_VS), col(Z_KW), col(Z_VW), _full_spec((3, NSA_HD))],
        out_specs=[o[0] for o in outs],
        out_shape=[o[1] for o in outs],
        compiler_params=_cparams(("arbitrary",)),
        name="nsa_prep",
    )(z, z, z, z, z, z, k_gain)


def _nsa_compress_kernel(ak_ref, av_ref, wk_ref, wv_ref, pe_ref, kg_ref, kc_ref, vc_ref):
    half = CMP_STRIDE * NSA_HD

    def compress(a, w_ref):
        top = jnp.dot(a, w_ref[:half, :], precision=HIGHEST, preferred_element_type=F32)
        bot = jnp.dot(a, w_ref[half:, :], precision=HIGHEST, preferred_element_type=F32)
        pe = jnp.dot(pe_ref[...], w_ref[...], precision=HIGHEST, preferred_element_type=F32)
        return top + _sublane_roll(bot, -1) + pe[0:1]

    kc_ref[0, 0] = _rms64(compress(ak_ref[0, 0], wk_ref), kg_ref[0:1, :])
    vc_ref[0, 0] = compress(av_ref[0, 0], wv_ref)


def _nsa_compress(kcr, vcr, w_ck, w_cv, cmp_pe, k_gain):
    batch, groups, seq, _ = kcr.shape
    nc = seq // CMP_STRIDE
    wide = CMP_STRIDE * NSA_HD
    spec = pl.BlockSpec((1, 1, nc, wide), lambda b, g: (b, g, 0, 0))
    ospec = pl.BlockSpec((1, 1, nc, NSA_HD), lambda b, g: (b, g, 0, 0))
    oshape = jax.ShapeDtypeStruct((batch, groups, nc, NSA_HD), F32)
    full2 = lambda shape: pl.BlockSpec(shape, lambda b, g: (0, 0))
    return pl.pallas_call(
        _nsa_compress_kernel,
        grid=(batch, groups),
        in_specs=[spec, spec, full2((2 * wide, NSA_HD)), full2((2 * wide, NSA_HD)), full2((8, 2 * wide)),
                  full2((3, NSA_HD))],
        out_specs=[ospec, ospec],
        out_shape=[oshape, oshape],
        compiler_params=_cparams(("arbitrary", "arbitrary")),
        name="nsa_compress",
    )(kcr.reshape(batch, groups, nc, wide), vcr.reshape(batch, groups, nc, wide), w_ck, w_cv,
      jnp.broadcast_to(cmp_pe.reshape(1, 2 * wide), (8, 2 * wide)), k_gain)


def _nsa_kernel(seq, kchunk, q_ref, gate_ref, kc_ref, vc_ref, ks_ref, vs_ref, kw_ref, vw_ref, qg_ref, agg_ref,
                o_ref):
    qi = pl.program_id(2)
    t0 = qi * Q_BLOCK
    jn = NSA_QPG
    nc = seq // CMP_STRIDE
    n_slc = seq // SLC_BLOCK
    n_sel = min(N_SELECT, n_slc)
    nt = (((1,), (1,)), ((), ()))

    qraw = q_ref[...]
    qs = [_rms64(qraw[:, j * NSA_HD:(j + 1) * NSA_HD], qg_ref[...]) * (NSA_HD ** -0.5) for j in range(jn)]
    qb = [q.astype(BF16) for q in qs]

    def pos_of(shape):
        return t0 + lax.broadcasted_iota(jnp.int32, shape, 0)

    cmp_end = lax.broadcasted_iota(jnp.int32, (Q_BLOCK, nc), 1) * CMP_STRIDE + (CMP_BLOCK - 1)
    valid_c = cmp_end <= pos_of((Q_BLOCK, nc))
    sees_any = pos_of((Q_BLOCK, 1)) >= CMP_BLOCK - 1
    kc = kc_ref[0, 0]
    vcb = vc_ref[0, 0].astype(BF16)
    o_c = []
    imp = None
    for j in range(jn):
        s = lax.dot_general(qs[j], kc, nt, precision=HIGHEST, preferred_element_type=F32)
        sm = jnp.where(valid_c, s, NEG)
        e = jnp.exp(sm - jnp.max(sm, axis=-1, keepdims=True))
        p = e * jnp.where(sees_any, 1.0 / jnp.sum(e, axis=-1, keepdims=True), 0.0)
        o_c.append(jnp.dot(p.astype(BF16), vcb, preferred_element_type=F32))
        imp = p if imp is None else imp + p

    imp_s = jnp.dot(imp, agg_ref[...], precision=HIGHEST, preferred_element_type=F32)
    ids = lax.broadcasted_iota(jnp.int32, (Q_BLOCK, n_slc), 1)
    q_blk = pos_of((Q_BLOCK, n_slc)) // SLC_BLOCK
    forced = (ids == 0) | (ids == q_blk) | (ids == q_blk - 1)
    score_t = jnp.where(ids > q_blk, NEG, jnp.where(forced, FORCE, imp_s)).T
    ids_t = lax.broadcasted_iota(jnp.int32, (n_slc, Q_BLOCK), 0).astype(F32)
    cur = score_t
    sel_t = jnp.zeros((n_slc, Q_BLOCK), F32)
    for _ in range(n_sel):
        m = jnp.max(cur, axis=0, keepdims=True)
        first = jnp.min(jnp.where(cur == m, ids_t, float(n_slc)), axis=0, keepdims=True)
        hit = ids_t == first
        sel_t = jnp.where(hit, 1.0, sel_t)
        cur = jnp.where(hit, -jnp.inf, cur)
    sel = jnp.where(score_t > NEG / 2, sel_t, 0.0).T.astype(BF16)

    span = Q_BLOCK + WINDOW
    start = pl.multiple_of(jnp.maximum(t0 - WINDOW, 0), Q_BLOCK)
    wsl = pl.ds(start, span)
    kpos = start + lax.broadcasted_iota(jnp.int32, (Q_BLOCK, span), 1)
    wpos = pos_of((Q_BLOCK, span))
    bias_w = jnp.where((kpos <= wpos) & (kpos > wpos - WINDOW), 0.0, NEG)
    kw = kw_ref[0, 0, wsl, :]
    vw = vw_ref[0, 0, wsl, :]
    o_w = []
    for j in range(jn):
        s = lax.dot_general(qb[j], kw, nt, preferred_element_type=F32) + bias_w
        e = jnp.exp(s - jnp.max(s, axis=-1, keepdims=True))
        o_w.append(jnp.dot(e.astype(BF16), vw, preferred_element_type=F32) / jnp.sum(e, axis=-1, keepdims=True))

    per_chunk = kchunk // SLC_BLOCK
    blk_of_key = lax.broadcasted_iota(jnp.int32, (n_slc, kchunk), 1) // SLC_BLOCK
    blk_row = lax.broadcasted_iota(jnp.int32, (n_slc, kchunk), 0)
    key_lane = lax.broadcasted_iota(jnp.int32, (Q_BLOCK, kchunk), 1)
    qpos = pos_of((Q_BLOCK, kchunk))

    def sel_body(c, carry):
        ksl = pl.ds(pl.multiple_of(c * kchunk, kchunk), kchunk)
        k_c = ks_ref[0, 0, ksl, :]
        v_c = vs_ref[0, 0, ksl, :]
        expand = (blk_row == blk_of_key + c * per_chunk).astype(BF16)
        chosen = jnp.dot(sel, expand, preferred_element_type=F32)
        bias = jnp.where((chosen > 0.5) & (key_lane + c * kchunk <= qpos), 0.0, NEG)
        m_run, l_run, acc = carry
        s = lax.dot_general(qb_all, k_c, nt, preferred_element_type=F32) + jnp.concatenate([bias] * jn, axis=0)
        m_new = jnp.maximum(m_run, jnp.max(s, axis=-1, keepdims=True))
        alpha = jnp.exp(m_run - m_new)
        p = jnp.exp(s - m_new)
        l_new = l_run * alpha + jnp.sum(p, axis=-1, keepdims=True)
        acc_new = acc * alpha + jnp.dot(p.astype(BF16), v_c, preferred_element_type=F32)
        return m_new, l_new, acc_new

    rows = jn * Q_BLOCK
    qb_all = jnp.concatenate(qb, axis=0)
    n_chunks = (t0 + Q_BLOCK + kchunk - 1) // kchunk
    init = (jnp.full((rows, 1), NEG, F32), jnp.zeros((rows, 1), F32), jnp.zeros((rows, NSA_HD), F32))
    _, l_fin, acc_fin = lax.fori_loop(0, n_chunks, sel_body, init)
    o_sel = acc_fin / l_fin

    gates = _sigmoid(gate_ref[...])
    outs = []
    for j in range(jn):
        o_s = o_sel[j * Q_BLOCK:(j + 1) * Q_BLOCK]
        outs.append(gates[:, 3 * j:3 * j + 1] * o_c[j] + gates[:, 3 * j + 1:3 * j + 2] * o_s
                    + gates[:, 3 * j + 2:3 * j + 3] * o_w[j])
    o_ref[...] = jnp.concatenate(outs, axis=1)


def _selection_weights(seq):
    nc = seq // CMP_STRIDE
    n_slc = seq // SLC_BLOCK
    ratio = SLC_BLOCK // CMP_STRIDE
    n = jnp.arange(nc)[:, None]
    j = jnp.arange(n_slc)[None, :]
    o = n - ratio * j + (CMP_BLOCK // CMP_STRIDE - 1)
    w = jnp.asarray(SEL_AGG_W, F32)
    return jnp.where((o >= 0) & (o < len(SEL_AGG_W)), w[jnp.clip(o, 0, len(SEL_AGG_W) - 1)], 0.0)


def _nsa(z, batch, seq, w_ck, w_cv, cmp_pe, q_gain, k_gain):
    t = z.shape[0]
    kcr, vcr, ksn, vsb, kwn, vwb = _nsa_prep(z, batch, seq, k_gain)
    kc, vc = _nsa_compress(kcr, vcr, w_ck, w_cv, cmp_pe, k_gain)
    nq = seq // Q_BLOCK
    nc = seq // CMP_STRIDE
    n_slc = seq // SLC_BLOCK
    kchunk = min(512, seq)
    qw = NSA_QPG * NSA_HD
    cmp_spec = pl.BlockSpec((1, 1, nc, NSA_HD), lambda b, g, i: (b, g, 0, 0))
    seq_spec = pl.BlockSpec((1, 1, seq, NSA_HD), lambda b, g, i: (b, g, 0, 0))
    return pl.pallas_call(
        functools.partial(_nsa_kernel, seq, kchunk),
        grid=(batch, NSA_KV_GROUPS, nq),
        in_specs=[pl.BlockSpec((Q_BLOCK, qw), lambda b, g, i: (b * nq + i, g)),
                  pl.BlockSpec((Q_BLOCK, LANES), lambda b, g, i: (b * nq + i, Z_GATE // LANES + g)),
                  cmp_spec, cmp_spec, seq_spec, seq_spec, seq_spec, seq_spec,
                  pl.BlockSpec((1, NSA_HD), lambda b, g, i: (0, 0)),
                  pl.BlockSpec((nc, n_slc), lambda b, g, i: (0, 0))],
        out_specs=pl.BlockSpec((Q_BLOCK, qw), lambda b, g, i: (b * nq + i, g)),
        out_shape=jax.ShapeDtypeStruct((t, NSA_W), F32),
        compiler_params=_cparams(("arbitrary", "arbitrary", "arbitrary")),
        name="nsa_attention",
    )(z, z, kc, vc, ksn, vsb, kwn, vwb, q_gain.reshape(1, -1), _selection_weights(seq))


def _pad_in_proj(w_in):
    per_group = NSA_QPG * 3
    gate_lo = Z_GATE
    out = jnp.zeros((D_MODEL, Z_N), F32)
    out = out.at[:, :gate_lo].set(w_in[:, :gate_lo])
    for g in range(NSA_KV_GROUPS):
        out = out.at[:, gate_lo + g * LANES:gate_lo + g * LANES + per_group].set(
            w_in[:, gate_lo + g * per_group:gate_lo + (g + 1) * per_group])
    return out.at[:, Z_HQ:].set(w_in[:, gate_lo + NSA_HEADS * 3:])


def _even_mixer(x2, batch, seq, g, sc, sh, gm, w_in, w_out, w_ck, w_cv, cmp_pe, q_gain, k_gain, lb, o_gain):
    z = _in_proj(x2, seq, g, sc, sh, _pad_in_proj(w_in).astype(BF16))
    r = _hgrn(z, batch, seq, lb, o_gain)
    a = _nsa(z, batch, seq, w_ck, w_cv, cmp_pe, q_gain, k_gain)
    return _out_proj(x2, seq, a, r, gm, w_out.astype(BF16))


def _odd_mixer(x2, seq, g, sc, sh, gm, w_pw1, dw, dw_b, ln_g, ln_b, w_pw2):
    u = _glu_proj(x2, seq, g, sc, sh, w_pw1.astype(BF16))
    dw_pad = jnp.zeros((CONV_HALO, D_MODEL), F32).at[:CONV_W].set(dw)
    return _conv_module(x2, seq, u, dw_pad, dw_b.reshape(1, -1), ln_g.reshape(1, -1), ln_b.reshape(1, -1), gm,
                        w_pw2.astype(BF16))


def _router_weights(w_group, w_expert):
    wr = jnp.zeros((D_MODEL, LANES), F32)
    return wr.at[:, :MOE_GROUPS].set(w_group).at[:, MOE_GROUPS:MOE_GROUPS + MOE_EXPERTS].set(w_expert)


def _lower_bounds_kernel(l_ref, o_ref):
    logits = l_ref[...]
    e = jnp.exp(logits - jnp.max(logits, axis=0, keepdims=True))
    p = e / jnp.sum(e, axis=0, keepdims=True)
    n = logits.shape[0]
    acc = jnp.zeros_like(p[0:1])
    for i in range(n):
        acc = acc + p[i:i + 1]
        o_ref[i:i + 1, :] = acc - p[0:1]


def _lower_bounds(lb_logits):
    return pl.pallas_call(
        _lower_bounds_kernel,
        out_shape=jax.ShapeDtypeStruct(lb_logits.shape, F32),
        name="hgrn_lower_bounds",
    )(lb_logits)


def kernel(x, c, ada_w, ada_b, norm_mix, norm_ffn, mix_w_in, mix_w_out, nsa_cmp_wk, nsa_cmp_wv, nsa_cmp_pe, nsa_q_gain, nsa_k_gain, hgrn_lb_logits, hgrn_o_gain, conv_w_pw1, conv_dw, conv_dw_b, conv_ln_g, conv_ln_b, conv_w_pw2, moe_w_group, moe_w_expert, moe_w1, moe_w3, moe_w2):
    batch, seq, d = x.shape
    x2 = x.reshape(batch * seq, d)
    mod = _modulation(c, ada_w, ada_b)
    lower_bounds = _lower_bounds(hgrn_lb_logits)
    for layer in range(DEPTH):
        sh_m, sc_m, g_m, sh_f, sc_f, g_f = (mod[layer, :, k] for k in range(6))
        i = layer // 2
        gain = norm_mix[layer].reshape(1, d)
        if layer % 2 == 0:
            x2 = _even_mixer(x2, batch, seq, gain, sc_m, sh_m, g_m, mix_w_in[i], mix_w_out[i], nsa_cmp_wk[i],
                             nsa_cmp_wv[i], nsa_cmp_pe[i], nsa_q_gain[i], nsa_k_gain[i], lower_bounds[i],
                             hgrn_o_gain[i])
        else:
            x2 = _odd_mixer(x2, seq, gain, sc_m, sh_m, g_m, conv_w_pw1[i], conv_dw[i], conv_dw_b[i], conv_ln_g[i],
                            conv_ln_b[i], conv_w_pw2[i])
        x2 = _moe_layer(x2, seq, norm_ffn[layer].reshape(1, d), sc_f, sh_f, g_f,
                        _router_weights(moe_w_group[layer], moe_w_expert[layer]), layer, moe_w1, moe_w3, moe_w2)
    return x2.reshape(batch, seq, d)
```

```python
import functools

import jax
import jax.numpy as jnp
from jax import lax
from jax.experimental import pallas as pl
from jax.experimental.pallas import tpu as pltpu

F32 = jnp.float32
BF16 = jnp.bfloat16
HIGHEST = lax.Precision.HIGHEST

D_MODEL = 1024
DEPTH = 4

NSA_HEADS = 8
NSA_KV_GROUPS = 2
NSA_QPG = NSA_HEADS // NSA_KV_GROUPS
NSA_HD = 64
NSA_W = NSA_HEADS * NSA_HD
NSA_KV_W = NSA_KV_GROUPS * NSA_HD
NSA_KCHUNK = 512
NSA_VW = 128
CMP_BLOCK = 32
CMP_STRIDE = 16
SLC_BLOCK = 64
N_SELECT = 16
WINDOW = 512
Q_BLOCK = 256
SEL_AGG_W = (1.0, 2.0, 2.0, 2.0, 1.0)

HG_HEADS = 4
HG_DK = 128
HG_DV = 128
HG_W = HG_HEADS * HG_DV
HG_CHUNK = 64
HG_SUB = 8

CONV_W = 31
MOE_GROUPS = 4
MOE_EPG = 8
MOE_EXPERTS = MOE_GROUPS * MOE_EPG
EXPERT_FF = 512

EPS = 1e-6
NEG = -1e30
FORCE = 1e4

LANES = 128
VMEM_LIMIT = 56 * 1024 * 1024

Z_Q = 0
Z_KC = 512
Z_VC = 640
Z_KS = 768
Z_VS = 896
Z_KW = 1024
Z_VW = 1152
Z_GATE = 1280
Z_HQ = Z_GATE + NSA_KV_GROUPS * LANES
Z_HF = Z_HQ + HG_W
Z_HI = Z_HF + HG_W
Z_HG = Z_HI + HG_W
Z_N = Z_HG + HG_W

TM = 256
MOE_BM = 256


def _cparams(sem):
    return pltpu.CompilerParams(dimension_semantics=sem, vmem_limit_bytes=VMEM_LIMIT)


def _sigmoid(x):
    return 1.0 / (1.0 + jnp.exp(-x))


def _silu(x):
    return x * _sigmoid(x)


def _mod_kernel(c_ref, w_ref, b_ref, o_ref):
    c = c_ref[...]
    o_ref[0] = jnp.dot(_silu(c), w_ref[0], precision=HIGHEST, preferred_element_type=F32) + b_ref[0]


def _modulation(c, ada_w, ada_b):
    b = c.shape[0]
    rows = 8
    c_pad = jnp.zeros((rows, D_MODEL), F32).at[:b].set(c)
    n = 6 * D_MODEL
    tn = 1536
    out = pl.pallas_call(
        _mod_kernel,
        grid=(DEPTH, n // tn),
        in_specs=[
            pl.BlockSpec((rows, D_MODEL), lambda l, j: (0, 0)),
            pl.BlockSpec((1, D_MODEL, tn), lambda l, j: (l, 0, j)),
            pl.BlockSpec((1, 1, tn), lambda l, j: (l, 0, j)),
        ],
        out_specs=pl.BlockSpec((1, rows, tn), lambda l, j: (l, 0, j)),
        out_shape=jax.ShapeDtypeStruct((DEPTH, rows, n), F32),
        compiler_params=_cparams(("arbitrary", "arbitrary")),
        name="adaln_mod",
    )(c_pad, ada_w, ada_b.reshape(DEPTH, 1, n))
    return out[:, :b].reshape(DEPTH, b, 6, 1, D_MODEL)


def _norm_mod(x, g, sc, sh):
    y = x * lax.rsqrt(jnp.mean(x * x, axis=-1, keepdims=True) + EPS)
    return (y * g) * (1.0 + sc) + sh


def _row_spec(width):
    return pl.BlockSpec((TM, width), lambda i: (i, 0))


def _full_spec(shape):
    nd = len(shape)
    return pl.BlockSpec(shape, lambda i: (0,) * nd)


def _batch_spec(seq):
    return pl.BlockSpec((1, 1, D_MODEL), lambda i: ((i * TM) // seq, 0, 0))


def _in_proj_kernel(x_ref, g_ref, sc_ref, sh_ref, w_ref, o_ref):
    h = _norm_mod(x_ref[...], g_ref[...], sc_ref[0], sh_ref[0]).astype(BF16)
    tn = Z_N // 4
    for j in range(4):
        o_ref[:, j * tn:(j + 1) * tn] = jnp.dot(h, w_ref[:, j * tn:(j + 1) * tn], preferred_element_type=F32)


def _in_proj(x2, seq, g, sc, sh, w_bf):
    t = x2.shape[0]
    return pl.pallas_call(
        _in_proj_kernel,
        grid=(t // TM,),
        in_specs=[_row_spec(D_MODEL), _full_spec((1, D_MODEL)), _batch_spec(seq), _batch_spec(seq),
                  _full_spec((D_MODEL, Z_N))],
        out_specs=_row_spec(Z_N),
        out_shape=jax.ShapeDtypeStruct((t, Z_N), F32),
        compiler_params=_cparams(("arbitrary",)),
        name="in_proj",
    )(x2, g, sc, sh, w_bf)


def _out_proj_kernel(x_ref, a_ref, r_ref, gm_ref, w_ref, o_ref):
    y = jnp.dot(a_ref[...].astype(BF16), w_ref[:NSA_W, :], preferred_element_type=F32)
    y = y + jnp.dot(r_ref[...].astype(BF16), w_ref[NSA_W:, :], preferred_element_type=F32)
    o_ref[...] = x_ref[...] + gm_ref[0] * y


def _out_proj(x2, seq, a, r, gm, w_bf):
    t = x2.shape[0]
    return pl.pallas_call(
        _out_proj_kernel,
        grid=(t // TM,),
        in_specs=[_row_spec(D_MODEL), _row_spec(NSA_W), _row_spec(HG_W), _batch_spec(seq),
                  _full_spec((NSA_W + HG_W, D_MODEL))],
        out_specs=_row_spec(D_MODEL),
        out_shape=jax.ShapeDtypeStruct((t, D_MODEL), F32),
        compiler_params=_cparams(("arbitrary",)),
        name="out_proj",
    )(x2, a, r, gm, w_bf)


def _glu_kernel(x_ref, g_ref, sc_ref, sh_ref, w_ref, o_ref):
    h = _norm_mod(x_ref[...], g_ref[...], sc_ref[0], sh_ref[0]).astype(BF16)
    tn = 512
    for j in range(D_MODEL // tn):
        a = jnp.dot(h, w_ref[:, j * tn:(j + 1) * tn], preferred_element_type=F32)
        gate = jnp.dot(h, w_ref[:, D_MODEL + j * tn:D_MODEL + (j + 1) * tn], preferred_element_type=F32)
        o_ref[:, j * tn:(j + 1) * tn] = a * _sigmoid(gate)


def _glu_proj(x2, seq, g, sc, sh, w_bf):
    t = x2.shape[0]
    return pl.pallas_call(
        _glu_kernel,
        grid=(t // TM,),
        in_specs=[_row_spec(D_MODEL), _full_spec((1, D_MODEL)), _batch_spec(seq), _batch_spec(seq),
                  _full_spec((D_MODEL, 2 * D_MODEL))],
        out_specs=_row_spec(D_MODEL),
        out_shape=jax.ShapeDtypeStruct((t, D_MODEL), F32),
        compiler_params=_cparams(("arbitrary",)),
        name="conv_glu",
    )(x2, g, sc, sh, w_bf)


CONV_HALO = 32
CONV_RB = 64
SUBLANES = 8
CONV_SHIFT_ROWS = TM + CONV_HALO - SUBLANES


def _conv_kernel(seq, x_ref, u_ref, halo_ref, dw_ref, dwb_ref, lng_ref, lnb_ref, gm_ref, w_ref, o_ref,
                 ext_ref, shift_ref, act_ref):
    i = pl.program_id(0)
    first = (i * TM) % seq == 0
    ext_ref[:CONV_HALO, :] = jnp.where(first, 0.0, halo_ref[...])
    ext_ref[CONV_HALO:, :] = u_ref[...]
    for s in range(1, SUBLANES):
        shift_ref[s - 1] = ext_ref[s:s + CONV_SHIFT_ROWS, :]
    off = CONV_HALO - (CONV_W - 1)
    for r in range(TM // CONV_RB):
        for c in range(D_MODEL // LANES):
            cs = slice(c * LANES, (c + 1) * LANES)
            acc = jnp.zeros((CONV_RB, LANES), F32)
            for k in range(CONV_W):
                s = (off + k) % SUBLANES
                lo = r * CONV_RB + off + k - s
                win = ext_ref[lo:lo + CONV_RB, cs] if s == 0 else shift_ref[s - 1, lo:lo + CONV_RB, cs]
                acc = acc + win * dw_ref[k:k + 1, cs]
            act_ref[r * CONV_RB:(r + 1) * CONV_RB, cs] = acc + dwb_ref[:, cs]
    v = act_ref[...]
    mu = jnp.mean(v, axis=-1, keepdims=True)
    var = jnp.mean(jnp.square(v - mu), axis=-1, keepdims=True)
    y = (v - mu) * lax.rsqrt(var + EPS) * lng_ref[...] + lnb_ref[...]
    y = _silu(y).astype(BF16)
    o_ref[...] = x_ref[...] + gm_ref[0] * jnp.dot(y, w_ref[...], preferred_element_type=F32)


def _conv_module(x2, seq, u, dw, dwb, lng, lnb, gm, w_bf):
    t = x2.shape[0]
    per = TM // CONV_HALO
    return pl.pallas_call(
        functools.partial(_conv_kernel, seq),
        grid=(t // TM,),
        in_specs=[_row_spec(D_MODEL), _row_spec(D_MODEL),
                  pl.BlockSpec((CONV_HALO, D_MODEL), lambda i: (jnp.maximum(i * per - 1, 0), 0)),
                  _full_spec((CONV_HALO, D_MODEL)), _full_spec((1, D_MODEL)), _full_spec((1, D_MODEL)),
                  _full_spec((1, D_MODEL)), _batch_spec(seq), _full_spec((D_MODEL, D_MODEL))],
        out_specs=_row_spec(D_MODEL),
        out_shape=jax.ShapeDtypeStruct((t, D_MODEL), F32),
        scratch_shapes=[pltpu.VMEM((TM + CONV_HALO, D_MODEL), F32),
                        pltpu.VMEM((SUBLANES - 1, CONV_SHIFT_ROWS, D_MODEL), F32),
                        pltpu.VMEM((TM, D_MODEL), F32)],
        compiler_params=_cparams(("arbitrary",)),
        name="conv_module",
    )(x2, u, u, dw, dwb, lng, lnb, gm, w_bf)


ROW_TILE = (SUBLANES, LANES)


def _store_tile_rows(ref, v):
    ref[...] = v.reshape((v.shape[0],) + ROW_TILE)


def _load_tile_rows(ref):
    return ref[...].reshape(ref.shape[0], D_MODEL)


def _first_index_of_max(v, lane):
    m = jnp.max(v, axis=-1, keepdims=True)
    idx = jnp.min(jnp.where(v == m, lane, LANES), axis=-1, keepdims=True)
    return m, idx


def _router_kernel(x_ref, g_ref, sc_ref, sh_ref, wr_ref, h_ref, ids_ref, gates_ref, cnt_ref, carry_ref):
    i = pl.program_id(0)

    @pl.when(i == 0)
    def _():
        carry_ref[...] = jnp.zeros_like(carry_ref)

    h = _norm_mod(x_ref[...], g_ref[...], sc_ref[0], sh_ref[0])
    _store_tile_rows(h_ref, h)
    logits = jnp.dot(h, wr_ref[...], precision=HIGHEST, preferred_element_type=F32)
    lane = lax.broadcasted_iota(jnp.int32, logits.shape, 1)
    gl = jnp.where(lane < MOE_GROUPS, logits, -jnp.inf)
    gmax, gsel = _first_index_of_max(gl, lane)
    g_gate = 1.0 / jnp.sum(jnp.exp(gl - gmax), axis=-1, keepdims=True)
    lo = MOE_GROUPS + MOE_EPG * gsel
    in_grp = (lane >= lo) & (lane < lo + MOE_EPG)
    el = jnp.where(in_grp, logits, -jnp.inf)
    emax = jnp.max(el, axis=-1, keepdims=True)
    p = jnp.exp(el - emax)
    prob = p / jnp.sum(p, axis=-1, keepdims=True)
    cand = jnp.where(in_grp, prob, -1.0)
    v1, i1 = _first_index_of_max(cand, lane)
    cand2 = jnp.where(lane == i1, -1.0, cand)
    v2, i2 = _first_index_of_max(cand2, lane)
    den = v1 + v2
    w1 = g_gate * v1 / den
    w2 = g_gate * v2 / den
    e1 = i1 - MOE_GROUPS
    e2 = i2 - MOE_GROUPS
    oh1 = (lane == e1).astype(F32)
    oh2 = (lane == e2).astype(F32)
    rows = lax.broadcasted_iota(jnp.int32, (TM, TM), 0)
    cols = lax.broadcasted_iota(jnp.int32, (TM, TM), 1)
    before = (cols < rows).astype(BF16)
    prior = jnp.dot(before, (oh1 + oh2).astype(BF16), preferred_element_type=F32) + carry_ref[...]
    r1 = jnp.sum(oh1 * prior, axis=-1, keepdims=True)
    r2 = jnp.sum(oh2 * prior, axis=-1, keepdims=True)
    carry_ref[...] = carry_ref[...] + jnp.sum(oh1 + oh2, axis=0, keepdims=True)
    cnt_ref[...] = jnp.broadcast_to(carry_ref[...], cnt_ref.shape)
    r1 = r1.astype(jnp.int32)
    r2 = r2.astype(jnp.int32)
    ids_ref[...] = jnp.where(lane == 0, e1, jnp.where(lane == 1, e2, jnp.where(lane == 2, r1, jnp.where(lane == 3, r2, 0))))
    gates_ref[...] = jnp.where(lane == 0, w1, jnp.where(lane == 1, w2, 0.0))


def _router(x2, seq, g, sc, sh, wr):
    t = x2.shape[0]
    return pl.pallas_call(
        _router_kernel,
        grid=(t // TM,),
        in_specs=[_row_spec(D_MODEL), _full_spec((1, D_MODEL)), _batch_spec(seq), _batch_spec(seq),
                  _full_spec((D_MODEL, LANES))],
        out_specs=[pl.BlockSpec((TM,) + ROW_TILE, lambda i: (i, 0, 0)), _row_spec(LANES), _row_spec(LANES),
                   _full_spec((8, LANES))],
        out_shape=[jax.ShapeDtypeStruct((t,) + ROW_TILE, F32), jax.ShapeDtypeStruct((t, LANES), jnp.int32),
                   jax.ShapeDtypeStruct((t, LANES), F32), jax.ShapeDtypeStruct((8, LANES), F32)],
        scratch_shapes=[pltpu.VMEM((1, LANES), F32)],
        compiler_params=_cparams(("arbitrary",)),
        name="moe_router",
    )(x2, g, sc, sh, wr)


def _row_copy(src_ref, src_row, dst_ref, dst_row, sem):
    return pltpu.make_async_copy(src_ref.at[src_row], dst_ref.at[dst_row], sem)


def _dispatch_kernel(dest_ref, h_ref, xs_in_ref, xs_ref, buf_ref, sem_ref):
    del xs_in_ref
    i = pl.program_id(0)
    n = pl.num_programs(0)
    slot = i % 2

    def wait_slot(s):
        for _ in range(2):
            pltpu.make_async_copy(buf_ref.at[s], xs_ref.at[pl.ds(0, TM)], sem_ref.at[s]).wait()

    @pl.when(i >= 2)
    def _():
        wait_slot(slot)

    buf_ref[slot] = h_ref[...]

    def issue(r, c):
        a = 2 * (i * TM + r)
        _row_copy(buf_ref.at[slot], r, xs_ref, dest_ref[a], sem_ref.at[slot]).start(priority=0)
        _row_copy(buf_ref.at[slot], r, xs_ref, dest_ref[a + 1], sem_ref.at[slot]).start(priority=1)
        return c
    lax.fori_loop(0, TM, issue, 0)

    @pl.when(i == n - 1)
    def _():
        wait_slot(slot)

        @pl.when(n >= 2)
        def _():
            wait_slot(1 - slot)


def _dispatch(dest, h, n_rows):
    t = h.shape[0]
    xs0 = jnp.zeros((n_rows,) + ROW_TILE, F32)
    return pl.pallas_call(
        _dispatch_kernel,
        grid_spec=pltpu.PrefetchScalarGridSpec(
            num_scalar_prefetch=1,
            grid=(t // TM,),
            in_specs=[pl.BlockSpec((TM,) + ROW_TILE, lambda i, d: (i, 0, 0)), pl.BlockSpec(memory_space=pl.ANY)],
            out_specs=pl.BlockSpec(memory_space=pl.ANY),
            scratch_shapes=[pltpu.VMEM((2, TM) + ROW_TILE, F32), pltpu.SemaphoreType.DMA((2,))],
        ),
        out_shape=jax.ShapeDtypeStruct((n_rows,) + ROW_TILE, F32),
        input_output_aliases={2: 0},
        compiler_params=_cparams(("arbitrary",)),
        name="moe_dispatch",
    )(dest, h, xs0)


def _expert_kernel(be_ref, nb_ref, xs_ref, w1_ref, w3_ref, w2_ref, y_ref, w1b_ref, w3b_ref, w2b_ref):
    i = pl.program_id(0)
    used = i < nb_ref[0]
    prev = be_ref[jnp.maximum(i - 1, 0)]

    @pl.when(used & ((i == 0) | (be_ref[i] != prev)))
    def _():
        w1b_ref[...] = w1_ref[0].astype(BF16)
        w3b_ref[...] = w3_ref[0].astype(BF16)
        w2b_ref[...] = w2_ref[0].astype(BF16)

    @pl.when(used)
    def _():
        xb = _load_tile_rows(xs_ref).astype(BF16)
        a = jnp.dot(xb, w1b_ref[...], preferred_element_type=F32)
        b = jnp.dot(xb, w3b_ref[...], preferred_element_type=F32)
        hmid = (_silu(a) * b).astype(BF16)
        _store_tile_rows(y_ref, jnp.dot(hmid, w2b_ref[...], preferred_element_type=F32))

    @pl.when(i >= nb_ref[0])
    def _():
        y_ref[...] = jnp.zeros_like(y_ref)


def _experts(blk_exp, n_used, xs, layer, w1, w3, w2):
    n_rows = xs.shape[0]
    nb = n_rows // MOE_BM

    def row_map(i, be, nu):
        return (jnp.minimum(i, nu[0] - 1), 0, 0)

    def w_map(i, be, nu):
        return (layer, be[jnp.minimum(i, nu[0] - 1)], 0, 0)

    return pl.pallas_call(
        _expert_kernel,
        grid_spec=pltpu.PrefetchScalarGridSpec(
            num_scalar_prefetch=2,
            grid=(nb,),
            in_specs=[pl.BlockSpec((MOE_BM,) + ROW_TILE, row_map),
                      pl.BlockSpec((None, 1, D_MODEL, EXPERT_FF), w_map),
                      pl.BlockSpec((None, 1, D_MODEL, EXPERT_FF), w_map),
                      pl.BlockSpec((None, 1, EXPERT_FF, D_MODEL), w_map)],
            out_specs=pl.BlockSpec((MOE_BM,) + ROW_TILE, lambda i, be, nu: (i, 0, 0)),
            scratch_shapes=[pltpu.VMEM((D_MODEL, EXPERT_FF), BF16), pltpu.VMEM((D_MODEL, EXPERT_FF), BF16),
                            pltpu.VMEM((EXPERT_FF, D_MODEL), BF16)],
        ),
        out_shape=jax.ShapeDtypeStruct((n_rows,) + ROW_TILE, F32),
        compiler_params=_cparams(("arbitrary",)),
        name="moe_experts",
    )(blk_exp, n_used, xs, w1, w3, w2)


def _combine_kernel(dest_ref, x_ref, gates_ref, gf_ref, y_ref, o_ref, buf_ref, sem_ref):
    i = pl.program_id(0)
    n = pl.num_programs(0)
    slot = i % 2

    def issue(tile, s):
        def body(r, c):
            a = 2 * (tile * TM + r)
            _row_copy(y_ref, dest_ref[a], buf_ref.at[s, 0], r, sem_ref.at[s]).start(priority=0)
            _row_copy(y_ref, dest_ref[a + 1], buf_ref.at[s, 1], r, sem_ref.at[s]).start(priority=1)
            return c
        lax.fori_loop(0, TM, body, 0)

    @pl.when(i == 0)
    def _():
        issue(0, 0)

    @pl.when(i + 1 < n)
    def _():
        issue(i + 1, 1 - slot)

    for k in range(2):
        pltpu.make_async_copy(y_ref.at[pl.ds(0, TM)], buf_ref.at[slot, k], sem_ref.at[slot]).wait()

    w0 = gates_ref[:, 0:1]
    w1 = gates_ref[:, 1:2]
    mix = _load_tile_rows(buf_ref.at[slot, 0]) * w0 + _load_tile_rows(buf_ref.at[slot, 1]) * w1
    o_ref[...] = x_ref[...] + gf_ref[0] * mix


def _combine(dest, x2, seq, gates, gf, y):
    t = x2.shape[0]
    return pl.pallas_call(
        _combine_kernel,
        grid_spec=pltpu.PrefetchScalarGridSpec(
            num_scalar_prefetch=1,
            grid=(t // TM,),
            in_specs=[pl.BlockSpec((TM, D_MODEL), lambda i, d: (i, 0)),
                      pl.BlockSpec((TM, LANES), lambda i, d: (i, 0)),
                      pl.BlockSpec((1, 1, D_MODEL), lambda i, d: ((i * TM) // seq, 0, 0)),
                      pl.BlockSpec(memory_space=pl.ANY)],
            out_specs=pl.BlockSpec((TM, D_MODEL), lambda i, d: (i, 0)),
            scratch_shapes=[pltpu.VMEM((2, 2, TM) + ROW_TILE, F32), pltpu.SemaphoreType.DMA((2,))],
        ),
        out_shape=jax.ShapeDtypeStruct((t, D_MODEL), F32),
        compiler_params=_cparams(("arbitrary",)),
        name="moe_combine",
    )(dest, x2, gates, gf, y)


def _moe_layer(x2, seq, g, sc, sh, gf, wr, layer, w1, w3, w2):
    t = x2.shape[0]
    h, ids, gates, cnt = _router(x2, seq, g, sc, sh, wr)
    counts = cnt[0, :MOE_EXPERTS].astype(jnp.int32)
    padded = (counts + MOE_BM - 1) // MOE_BM * MOE_BM
    pends = jnp.cumsum(padded)
    pstarts = pends - padded
    dest = (pstarts[ids[:, 0:2]] + ids[:, 2:4]).reshape(2 * t)
    n_rows = 2 * t + MOE_EXPERTS * MOE_BM
    nb = n_rows // MOE_BM
    blk_start = jnp.arange(nb, dtype=jnp.int32) * MOE_BM
    blk_exp = jnp.minimum(jnp.sum((pends[None, :] <= blk_start[:, None]).astype(jnp.int32), axis=1), MOE_EXPERTS - 1)
    n_used = (pends[-1:] // MOE_BM).astype(jnp.int32)
    xs = _dispatch(dest, h, n_rows)
    y = _experts(blk_exp, n_used, xs, layer, w1, w3, w2)
    return _combine(dest, x2, seq, gates, gf, y)


HG_LT = 256


def _sublane_roll(x, shift):
    return pltpu.roll(x, shift % x.shape[0], 0)


def _hgrn_chunk(hq, hf, hi, lb, state_t):
    c, sub = HG_CHUNK, HG_SUB
    q = _silu(hq)
    f = lb + (1.0 - lb) * _sigmoid(hf)
    lf = jnp.log(f)
    k = 1.0 - f
    v = hi
    row = lax.broadcasted_iota(jnp.int32, (c, c), 0)
    col = lax.broadcasted_iota(jnp.int32, (c, c), 1)
    start = (row // sub) * sub
    tri_local = ((col >= start) & (col <= row)).astype(F32)
    tri_before = (col < start).astype(F32)
    bloc = jnp.dot(tri_local, lf, precision=HIGHEST, preferred_element_type=F32)
    rref = jnp.dot(tri_before, lf, precision=HIGHEST, preferred_element_type=F32)
    b = rref + bloc
    qt = q * jnp.exp(bloc)
    ridx = lax.broadcasted_iota(jnp.int32, (c, HG_DK), 0)

    blocks = [jnp.zeros((sub, c), F32)]
    for i in range(1, c // sub):
        r_i = rref[i * sub:i * sub + 1, :]
        live = ridx < i * sub
        k_i = jnp.where(live, k * jnp.exp(jnp.where(live, r_i - b, 0.0)), 0.0)
        blocks.append(lax.dot_general(qt[i * sub:(i + 1) * sub].astype(BF16), k_i.astype(BF16),
                                      (((1,), (1,)), ((), ())), preferred_element_type=F32))
    attn_off = jnp.concatenate(blocks, axis=0)
    o = jnp.dot(attn_off.astype(BF16), v.astype(BF16), preferred_element_type=F32)

    pos = ridx % sub
    for d in range(sub):
        qd = q if d == 0 else _sublane_roll(q, -d)
        bd = bloc if d == 0 else _sublane_roll(bloc, -d)
        ok = pos + d < sub
        e = qd * k * jnp.exp(jnp.where(ok, bd - bloc, -jnp.inf))
        w = jnp.sum(e, axis=-1, keepdims=True) * v
        o = o + (w if d == 0 else _sublane_roll(w, d))

    qe = qt * jnp.exp(rref)
    o = o + lax.dot_general(qe.astype(BF16), state_t.astype(BF16), (((1,), (1,)), ((), ())),
                            preferred_element_type=F32)
    b_end = b[c - 1:c, :]
    kd = k * jnp.exp(b_end - b)
    new_state = state_t * jnp.exp(b_end) + lax.dot_general(v.astype(BF16), kd.astype(BF16), (((0,), (0,)), ((), ())),
                                                           preferred_element_type=F32)
    return o, new_state


def _hgrn_kernel(hq_ref, hf_ref, hi_ref, hg_ref, lb_ref, og_ref, o_ref, state_ref):
    @pl.when(pl.program_id(0) == 0)
    def _():
        state_ref[...] = jnp.zeros_like(state_ref)

    og = og_ref[...]

    def body(ci, carry):
        rows = pl.ds(pl.multiple_of(ci * HG_CHUNK, HG_CHUNK), HG_CHUNK)
        for b in range(hq_ref.shape[0]):
            for h in range(HG_HEADS):
                cs = slice(h * HG_DK, (h + 1) * HG_DK)
                o, st = _hgrn_chunk(hq_ref[b, rows, cs], hf_ref[b, rows, cs], hi_ref[b, rows, cs], lb_ref[:, cs],
                                    state_ref[b, h])
                state_ref[b, h] = st
                on = o * lax.rsqrt(jnp.mean(o * o, axis=-1, keepdims=True) + EPS) * og
                o_ref[b, rows, cs] = on * _silu(hg_ref[b, rows, cs])
        return carry
    lax.fori_loop(0, HG_LT // HG_CHUNK, body, 0)


def _hgrn(z, batch, seq, lb, o_gain):
    t = z.shape[0]
    z3 = z.reshape(batch, seq, z.shape[1])

    def col(base):
        return pl.BlockSpec((batch, HG_LT, HG_W), lambda l: (0, l, base // HG_W))

    out = pl.pallas_call(
        _hgrn_kernel,
        grid=(seq // HG_LT,),
        in_specs=[col(Z_HQ), col(Z_HF), col(Z_HI), col(Z_HG),
                  pl.BlockSpec((1, HG_W), lambda l: (0, 0)),
                  pl.BlockSpec((1, HG_DV), lambda l: (0, 0))],
        out_specs=pl.BlockSpec((batch, HG_LT, HG_W), lambda l: (0, l, 0)),
        out_shape=jax.ShapeDtypeStruct((batch, seq, HG_W), F32),
        scratch_shapes=[pltpu.VMEM((batch, HG_HEADS, HG_DV, HG_DK), F32)],
        compiler_params=_cparams(("arbitrary",)),
        name="hgrn2",
    )(z3, z3, z3, z3, lb.reshape(1, -1), o_gain.reshape(1, -1))
    return out.reshape(t, HG_W)


def _rms64(v, gain):
    return v * lax.rsqrt(jnp.mean(v * v, axis=-1, keepdims=True) + EPS) * gain


def _nsa_prep_kernel(kc_ref, vc_ref, ks_ref, vs_ref, kw_ref, vw_ref, kg_ref,
                     kcr_ref, vcr_ref, ksn_ref, vsb_ref, kwn_ref, vwb_ref):
    ones_col = (lax.broadcasted_iota(jnp.int32, (TM, NSA_VW - NSA_HD), 1) == 0).astype(F32)

    def with_ones(v):
        return jnp.concatenate([v, ones_col], axis=1).astype(BF16)

    for g in range(NSA_KV_GROUPS):
        cs = slice(g * NSA_HD, (g + 1) * NSA_HD)
        kcr_ref[0, g] = kc_ref[:, cs]
        vcr_ref[0, g] = vc_ref[:, cs]
        ksn_ref[0, g] = _rms64(ks_ref[:, cs], kg_ref[1:2, :]).astype(BF16)
        vsb_ref[0, g] = with_ones(vs_ref[:, cs])
        kwn_ref[0, g] = _rms64(kw_ref[:, cs], kg_ref[2:3, :]).astype(BF16)
        vwb_ref[0, g] = with_ones(vw_ref[:, cs])


def _nsa_prep(z, batch, seq, k_gain):
    per = seq // TM

    def col(base):
        return pl.BlockSpec((TM, NSA_KV_W), lambda i: (i, base // NSA_KV_W))

    def out(dtype, width=NSA_HD):
        return (pl.BlockSpec((1, NSA_KV_GROUPS, TM, width), lambda i: (i // per, 0, i % per, 0)),
                jax.ShapeDtypeStruct((batch, NSA_KV_GROUPS, seq, width), dtype))

    outs = [out(F32), out(F32), out(BF16), out(BF16, NSA_VW), out(BF16), out(BF16, NSA_VW)]
    return pl.pallas_call(
        _nsa_prep_kernel,
        grid=(batch * per,),
        in_specs=[col(Z_KC), col(Z_VC), col(Z_KS), col(Z_VS), col(Z_KW), col(Z_VW), _full_spec((3, NSA_HD))],
        out_specs=[o[0] for o in outs],
        out_shape=[o[1] for o in outs],
        compiler_params=_cparams(("arbitrary",)),
        name="nsa_prep",
    )(z, z, z, z, z, z, k_gain)


def _nsa_compress_kernel(ak_ref, av_ref, wk_ref, wv_ref, pe_ref, kg_ref, kc_ref, vc_ref):
    half = CMP_STRIDE * NSA_HD

    def compress(a, w_ref):
        top = jnp.dot(a, w_ref[:half, :], precision=HIGHEST, preferred_element_type=F32)
        bot = jnp.dot(a, w_ref[half:, :], precision=HIGHEST, preferred_element_type=F32)
        pe = jnp.dot(pe_ref[...], w_ref[...], precision=HIGHEST, preferred_element_type=F32)
        return top + _sublane_roll(bot, -1) + pe[0:1]

    kc_ref[0, 0] = _rms64(compress(ak_ref[0, 0], wk_ref), kg_ref[0:1, :])
    vc_ref[0, 0] = compress(av_ref[0, 0], wv_ref)


def _nsa_compress(kcr, vcr, w_ck, w_cv, cmp_pe, k_gain):
    batch, groups, seq, _ = kcr.shape
    nc = seq // CMP_STRIDE
    wide = CMP_STRIDE * NSA_HD
    spec = pl.BlockSpec((1, 1, nc, wide), lambda b, g: (b, g, 0, 0))
    ospec = pl.BlockSpec((1, 1, nc, NSA_HD), lambda b, g: (b, g, 0, 0))
    oshape = jax.ShapeDtypeStruct((batch, groups, nc, NSA_HD), F32)
    full2 = lambda shape: pl.BlockSpec(shape, lambda b, g: (0, 0))
    return pl.pallas_call(
        _nsa_compress_kernel,
        grid=(batch, groups),
        in_specs=[spec, spec, full2((2 * wide, NSA_HD)), full2((2 * wide, NSA_HD)), full2((8, 2 * wide)),
                  full2((3, NSA_HD))],
        out_specs=[ospec, ospec],
        out_shape=[oshape, oshape],
        compiler_params=_cparams(("arbitrary", "arbitrary")),
        name="nsa_compress",
    )(kcr.reshape(batch, groups, nc, wide), vcr.reshape(batch, groups, nc, wide), w_ck, w_cv,
      jnp.broadcast_to(cmp_pe.reshape(1, 2 * wide), (8, 2 * wide)), k_gain)


def _nsa_kernel(seq, kchunk, q_ref, gate_ref, kc_ref, vc_ref, ks_ref, vs_ref, kw_ref, vw_ref, qg_ref, agg_ref,
                o_ref):
    qi = pl.program_id(2)
    t0 = qi * Q_BLOCK
    jn = NSA_QPG
    nc = seq // CMP_STRIDE
    n_slc = seq // SLC_BLOCK
    n_sel = min(N_SELECT, n_slc)
    nt = (((1,), (1,)), ((), ()))

    qraw = q_ref[...]
    qs = [_rms64(qraw[:, j * NSA_HD:(j + 1) * NSA_HD], qg_ref[...]) * (NSA_HD ** -0.5) for j in range(jn)]
    qb = [q.astype(BF16) for q in qs]

    def pos_of(shape):
        return t0 + lax.broadcasted_iota(jnp.int32, shape, 0)

    cmp_end = lax.broadcasted_iota(jnp.int32, (Q_BLOCK, nc), 1) * CMP_STRIDE + (CMP_BLOCK - 1)
    valid_c = cmp_end <= pos_of((Q_BLOCK, nc))
    sees_any = pos_of((Q_BLOCK, 1)) >= CMP_BLOCK - 1
    kc = kc_ref[0, 0]
    vcb = vc_ref[0, 0].astype(BF16)
    o_c = []
    imp = None
    for j in range(jn):
        s = lax.dot_general(qs[j], kc, nt, precision=HIGHEST, preferred_element_type=F32)
        sm = jnp.where(valid_c, s, NEG)
        e = jnp.exp(sm - jnp.max(sm, axis=-1, keepdims=True))
        p = e * jnp.where(sees_any, 1.0 / jnp.sum(e, axis=-1, keepdims=True), 0.0)
        o_c.append(jnp.dot(p.astype(BF16), vcb, preferred_element_type=F32))
        imp = p if imp is None else imp + p

    imp_s = jnp.dot(imp, agg_ref[...], precision=HIGHEST, preferred_element_type=F32)
    ids = lax.broadcasted_iota(jnp.int32, (Q_BLOCK, n_slc), 1)
    q_blk = pos_of((Q_BLOCK, n_slc)) // SLC_BLOCK
    forced = (ids == 0) | (ids == q_blk) | (ids == q_blk - 1)
    score_t = jnp.where(ids > q_blk, NEG, jnp.where(forced, FORCE, imp_s)).T
    ids_t = lax.broadcasted_iota(jnp.int32, (n_slc, Q_BLOCK), 0).astype(F32)
    cur = score_t
    sel_t = jnp.zeros((n_slc, Q_BLOCK), F32)
    for _ in range(n_sel):
        m = jnp.max(cur, axis=0, keepdims=True)
        first = jnp.min(jnp.where(cur == m, ids_t, float(n_slc)), axis=0, keepdims=True)
        hit = ids_t == first
        sel_t = jnp.where(hit, 1.0, sel_t)
        cur = jnp.where(hit, -jnp.inf, cur)
    sel = jnp.where(score_t > NEG / 2, sel_t, 0.0).T.astype(BF16)

    span = Q_BLOCK + WINDOW
    start = pl.multiple_of(jnp.maximum(t0 - WINDOW, 0), Q_BLOCK)
    wsl = pl.ds(start, span)
    kpos = start + lax.broadcasted_iota(jnp.int32, (Q_BLOCK, span), 1)
    wpos = pos_of((Q_BLOCK, span))
    bias_w = jnp.where((kpos <= wpos) & (kpos > wpos - WINDOW), 0.0, NEG)
    kw = kw_ref[0, 0, wsl, :]
    vw = vw_ref[0, 0, wsl, :]
    o_w = []
    for j in range(jn):
        s = lax.dot_general(qb[j], kw, nt, preferred_element_type=F32) + bias_w
        e = jnp.exp((s - jnp.max(s, axis=-1, keepdims=True)).astype(BF16))
        ow = jnp.dot(e, vw, preferred_element_type=F32)
        o_w.append(ow[:, :NSA_HD] / ow[:, NSA_HD:NSA_HD + 1])

    per_chunk = kchunk // SLC_BLOCK
    blk_of_key = lax.broadcasted_iota(jnp.int32, (n_slc, kchunk), 1) // SLC_BLOCK
    blk_row = lax.broadcasted_iota(jnp.int32, (n_slc, kchunk), 0)
    key_lane = lax.broadcasted_iota(jnp.int32, (Q_BLOCK, kchunk), 1)
    qpos = pos_of((Q_BLOCK, kchunk))

    def sel_body(c, carry):
        ksl = pl.ds(pl.multiple_of(c * kchunk, kchunk), kchunk)
        k_c = ks_ref[0, 0, ksl, :]
        v_c = vs_ref[0, 0, ksl, :]
        expand = (blk_row == blk_of_key + c * per_chunk).astype(BF16)
        chosen = jnp.dot(sel, expand, preferred_element_type=F32)
        bias = jnp.where((chosen > 0.5) & (key_lane + c * kchunk <= qpos), 0.0, NEG)
        m_run, acc = carry
        s = lax.dot_general(qb_all, k_c, nt, preferred_element_type=F32) + jnp.concatenate([bias] * jn, axis=0)
        m_new = jnp.maximum(m_run, jnp.max(s, axis=-1, keepdims=True))
        alpha = jnp.exp(m_run - m_new)
        p = jnp.exp((s - m_new).astype(BF16))
        acc_new = acc * alpha + jnp.dot(p, v_c, preferred_element_type=F32)
        return m_new, acc_new

    rows = jn * Q_BLOCK
    qb_all = jnp.concatenate(qb, axis=0)
    n_chunks = (t0 + Q_BLOCK + kchunk - 1) // kchunk
    init = (jnp.full((rows, 1), NEG, F32), jnp.zeros((rows, NSA_VW), F32))
    _, acc_fin = lax.fori_loop(0, n_chunks, sel_body, init)
    o_sel = acc_fin[:, :NSA_HD] / acc_fin[:, NSA_HD:NSA_HD + 1]

    gates = _sigmoid(gate_ref[...])
    outs = []
    for j in range(jn):
        o_s = o_sel[j * Q_BLOCK:(j + 1) * Q_BLOCK]
        outs.append(gates[:, 3 * j:3 * j + 1] * o_c[j] + gates[:, 3 * j + 1:3 * j + 2] * o_s
                    + gates[:, 3 * j + 2:3 * j + 3] * o_w[j])
    o_ref[...] = jnp.concatenate(outs, axis=1)


def _selection_weights(seq):
    nc = seq // CMP_STRIDE
    n_slc = seq // SLC_BLOCK
    ratio = SLC_BLOCK // CMP_STRIDE
    n = jnp.arange(nc)[:, None]
    j = jnp.arange(n_slc)[None, :]
    o = n - ratio * j + (CMP_BLOCK // CMP_STRIDE - 1)
    w = jnp.asarray(SEL_AGG_W, F32)
    return jnp.where((o >= 0) & (o < len(SEL_AGG_W)), w[jnp.clip(o, 0, len(SEL_AGG_W) - 1)], 0.0)


def _nsa(z, batch, seq, w_ck, w_cv, cmp_pe, q_gain, k_gain):
    t = z.shape[0]
    kcr, vcr, ksn, vsb, kwn, vwb = _nsa_prep(z, batch, seq, k_gain)
    kc, vc = _nsa_compress(kcr, vcr, w_ck, w_cv, cmp_pe, k_gain)
    nq = seq // Q_BLOCK
    nc = seq // CMP_STRIDE
    n_slc = seq // SLC_BLOCK
    kchunk = min(NSA_KCHUNK, seq)
    qw = NSA_QPG * NSA_HD
    cmp_spec = pl.BlockSpec((1, 1, nc, NSA_HD), lambda b, g, i: (b, g, 0, 0))
    key_spec = pl.BlockSpec((1, 1, seq, NSA_HD), lambda b, g, i: (b, g, 0, 0))
    val_spec = pl.BlockSpec((1, 1, seq, NSA_VW), lambda b, g, i: (b, g, 0, 0))
    return pl.pallas_call(
        functools.partial(_nsa_kernel, seq, kchunk),
        grid=(batch, NSA_KV_GROUPS, nq),
        in_specs=[pl.BlockSpec((Q_BLOCK, qw), lambda b, g, i: (b * nq + i, g)),
                  pl.BlockSpec((Q_BLOCK, LANES), lambda b, g, i: (b * nq + i, Z_GATE // LANES + g)),
                  cmp_spec, cmp_spec, key_spec, val_spec, key_spec, val_spec,
                  pl.BlockSpec((1, NSA_HD), lambda b, g, i: (0, 0)),
                  pl.BlockSpec((nc, n_slc), lambda b, g, i: (0, 0))],
        out_specs=pl.BlockSpec((Q_BLOCK, qw), lambda b, g, i: (b * nq + i, g)),
        out_shape=jax.ShapeDtypeStruct((t, NSA_W), F32),
        compiler_params=_cparams(("arbitrary", "arbitrary", "arbitrary")),
        name="nsa_attention",
    )(z, z, kc, vc, ksn, vsb, kwn, vwb, q_gain.reshape(1, -1), _selection_weights(seq))


def _pad_in_proj(w_in):
    per_group = NSA_QPG * 3
    gate_lo = Z_GATE
    out = jnp.zeros((D_MODEL, Z_N), F32)
    out = out.at[:, :gate_lo].set(w_in[:, :gate_lo])
    for g in range(NSA_KV_GROUPS):
        out = out.at[:, gate_lo + g * LANES:gate_lo + g * LANES + per_group].set(
            w_in[:, gate_lo + g * per_group:gate_lo + (g + 1) * per_group])
    return out.at[:, Z_HQ:].set(w_in[:, gate_lo + NSA_HEADS * 3:])


def _even_mixer(x2, batch, seq, g, sc, sh, gm, w_in, w_out, w_ck, w_cv, cmp_pe, q_gain, k_gain, lb, o_gain):
    z = _in_proj(x2, seq, g, sc, sh, _pad_in_proj(w_in).astype(BF16))
    r = _hgrn(z, batch, seq, lb, o_gain)
    a = _nsa(z, batch, seq, w_ck, w_cv, cmp_pe, q_gain, k_gain)
    return _out_proj(x2, seq, a, r, gm, w_out.astype(BF16))


def _odd_mixer(x2, seq, g, sc, sh, gm, w_pw1, dw, dw_b, ln_g, ln_b, w_pw2):
    u = _glu_proj(x2, seq, g, sc, sh, w_pw1.astype(BF16))
    dw_pad = jnp.zeros((CONV_HALO, D_MODEL), F32).at[:CONV_W].set(dw)
    return _conv_module(x2, seq, u, dw_pad, dw_b.reshape(1, -1), ln_g.reshape(1, -1), ln_b.reshape(1, -1), gm,
                        w_pw2.astype(BF16))


def _router_weights(w_group, w_expert):
    wr = jnp.zeros((D_MODEL, LANES), F32)
    return wr.at[:, :MOE_GROUPS].set(w_group).at[:, MOE_GROUPS:MOE_GROUPS + MOE_EXPERTS].set(w_expert)


def _lower_bounds_kernel(l_ref, o_ref):
    logits = l_ref[...]
    e = jnp.exp(logits - jnp.max(logits, axis=0, keepdims=True))
    p = e / jnp.sum(e, axis=0, keepdims=True)
    n = logits.shape[0]
    acc = jnp.zeros_like(p[0:1])
    for i in range(n):
        acc = acc + p[i:i + 1]
        o_ref[i:i + 1, :] = acc - p[0:1]


def _lower_bounds(lb_logits):
    return pl.pallas_call(
        _lower_bounds_kernel,
        out_shape=jax.ShapeDtypeStruct(lb_logits.shape, F32),
        name="hgrn_lower_bounds",
    )(lb_logits)


def kernel(x, c, ada_w, ada_b, norm_mix, norm_ffn, mix_w_in, mix_w_out, nsa_cmp_wk, nsa_cmp_wv, nsa_cmp_pe, nsa_q_gain, nsa_k_gain, hgrn_lb_logits, hgrn_o_gain, conv_w_pw1, conv_dw, conv_dw_b, conv_ln_g, conv_ln_b, conv_w_pw2, moe_w_group, moe_w_expert, moe_w1, moe_w3, moe_w2):
    batch, seq, d = x.shape
    x2 = x.reshape(batch * seq, d)
    mod = _modulation(c, ada_w, ada_b)
    lower_bounds = _lower_bounds(hgrn_lb_logits)
    for layer in range(DEPTH):
        sh_m, sc_m, g_m, sh_f, sc_f, g_f = (mod[layer, :, k] for k in range(6))
        i = layer // 2
        gain = norm_mix[layer].reshape(1, d)
        if layer % 2 == 0:
            x2 = _even_mixer(x2, batch, seq, gain, sc_m, sh_m, g_m, mix_w_in[i], mix_w_out[i], nsa_cmp_wk[i],
                             nsa_cmp_wv[i], nsa_cmp_pe[i], nsa_q_gain[i], nsa_k_gain[i], lower_bounds[i],
                             hgrn_o_gain[i])
        else:
            x2 = _odd_mixer(x2, seq, gain, sc_m, sh_m, g_m, conv_w_pw1[i], conv_dw[i], conv_dw_b[i], conv_ln_g[i],
                            conv_ln_b[i], conv_w_pw2[i])
        x2 = _moe_layer(x2, seq, norm_ffn[layer].reshape(1, d), sc_f, sh_f, g_f,
                        _router_weights(moe_w_group[layer], moe_w_expert[layer]), layer, moe_w1, moe_w3, moe_w2)
    return x2.reshape(batch, seq, d)
```

```python
import functools

import jax
import jax.numpy as jnp
from jax import lax
from jax.experimental import pallas as pl
from jax.experimental.pallas import tpu as pltpu

F32 = jnp.float32
BF16 = jnp.bfloat16
HIGHEST = lax.Precision.HIGHEST

D_MODEL = 1024
DEPTH = 4

NSA_HEADS = 8
NSA_KV_GROUPS = 2
NSA_QPG = NSA_HEADS // NSA_KV_GROUPS
NSA_HD = 64
NSA_W = NSA_HEADS * NSA_HD
NSA_KV_W = NSA_KV_GROUPS * NSA_HD
NSA_KCHUNK = 512
NSA_VW = 128
CMP_BLOCK = 32
CMP_STRIDE = 16
SLC_BLOCK = 64
N_SELECT = 16
WINDOW = 512
Q_BLOCK = 256
SEL_AGG_W = (1.0, 2.0, 2.0, 2.0, 1.0)

HG_HEADS = 4
HG_DK = 128
HG_DV = 128
HG_W = HG_HEADS * HG_DV
HG_CHUNK = 64
HG_SUB = 8

CONV_W = 31
MOE_GROUPS = 4
MOE_EPG = 8
MOE_EXPERTS = MOE_GROUPS * MOE_EPG
EXPERT_FF = 512

EPS = 1e-6
NEG = -1e30
FORCE = 1e4

LANES = 128
VMEM_LIMIT = 56 * 1024 * 1024

Z_Q = 0
Z_KC = 512
Z_VC = 640
Z_KS = 768
Z_VS = 896
Z_KW = 1024
Z_VW = 1152
Z_GATE = 1280
Z_HQ = Z_GATE + NSA_KV_GROUPS * LANES
Z_HF = Z_HQ + HG_W
Z_HI = Z_HF + HG_W
Z_HG = Z_HI + HG_W
Z_N = Z_HG + HG_W

TM = 256
MOE_BM = 256


def _cparams(sem):
    return pltpu.CompilerParams(dimension_semantics=sem, vmem_limit_bytes=VMEM_LIMIT)


def _split_bf16(x, terms):
    parts = []
    for _ in range(terms):
        p = x.astype(BF16)
        parts.append(p)
        x = x - p.astype(F32)
    return parts


def _dot_bf16x3(a, b, dims):
    ah, al = _split_bf16(a, 2)
    bh, bl = _split_bf16(b, 2)

    def d(x, y):
        return lax.dot_general(x, y, dims, preferred_element_type=F32)
    return d(ah, bh) + (d(ah, bl) + d(al, bh))


def _sigmoid(x):
    return 1.0 / (1.0 + jnp.exp(-x))


def _silu(x):
    return x * _sigmoid(x)


def _mod_kernel(c_ref, w_ref, b_ref, o_ref):
    c = c_ref[...]
    o_ref[0] = jnp.dot(_silu(c), w_ref[0], precision=HIGHEST, preferred_element_type=F32) + b_ref[0]


def _modulation(c, ada_w, ada_b):
    b = c.shape[0]
    rows = 8
    c_pad = jnp.zeros((rows, D_MODEL), F32).at[:b].set(c)
    n = 6 * D_MODEL
    tn = 1536
    out = pl.pallas_call(
        _mod_kernel,
        grid=(DEPTH, n // tn),
        in_specs=[
            pl.BlockSpec((rows, D_MODEL), lambda l, j: (0, 0)),
            pl.BlockSpec((1, D_MODEL, tn), lambda l, j: (l, 0, j)),
            pl.BlockSpec((1, 1, tn), lambda l, j: (l, 0, j)),
        ],
        out_specs=pl.BlockSpec((1, rows, tn), lambda l, j: (l, 0, j)),
        out_shape=jax.ShapeDtypeStruct((DEPTH, rows, n), F32),
        compiler_params=_cparams(("arbitrary", "arbitrary")),
        name="adaln_mod",
    )(c_pad, ada_w, ada_b.reshape(DEPTH, 1, n))
    return out[:, :b].reshape(DEPTH, b, 6, 1, D_MODEL)


def _norm_mod(x, g, sc, sh):
    y = x * lax.rsqrt(jnp.mean(x * x, axis=-1, keepdims=True) + EPS)
    return (y * g) * (1.0 + sc) + sh


def _row_spec(width):
    return pl.BlockSpec((TM, width), lambda i: (i, 0))


def _full_spec(shape):
    nd = len(shape)
    return pl.BlockSpec(shape, lambda i: (0,) * nd)


def _batch_spec(seq):
    return pl.BlockSpec((1, 1, D_MODEL), lambda i: ((i * TM) // seq, 0, 0))


def _in_proj_kernel(x_ref, g_ref, sc_ref, sh_ref, w_ref, o_ref):
    h = _norm_mod(x_ref[...], g_ref[...], sc_ref[0], sh_ref[0]).astype(BF16)
    tn = Z_N // 4
    for j in range(4):
        o_ref[:, j * tn:(j + 1) * tn] = jnp.dot(h, w_ref[:, j * tn:(j + 1) * tn], preferred_element_type=F32)


def _in_proj(x2, seq, g, sc, sh, w_bf):
    t = x2.shape[0]
    return pl.pallas_call(
        _in_proj_kernel,
        grid=(t // TM,),
        in_specs=[_row_spec(D_MODEL), _full_spec((1, D_MODEL)), _batch_spec(seq), _batch_spec(seq),
                  _full_spec((D_MODEL, Z_N))],
        out_specs=_row_spec(Z_N),
        out_shape=jax.ShapeDtypeStruct((t, Z_N), F32),
        compiler_params=_cparams(("arbitrary",)),
        name="in_proj",
    )(x2, g, sc, sh, w_bf)


def _out_proj_kernel(x_ref, a_ref, r_ref, gm_ref, w_ref, o_ref):
    y = jnp.dot(a_ref[...].astype(BF16), w_ref[:NSA_W, :], preferred_element_type=F32)
    y = y + jnp.dot(r_ref[...].astype(BF16), w_ref[NSA_W:, :], preferred_element_type=F32)
    o_ref[...] = x_ref[...] + gm_ref[0] * y


def _out_proj(x2, seq, a, r, gm, w_bf):
    t = x2.shape[0]
    return pl.pallas_call(
        _out_proj_kernel,
        grid=(t // TM,),
        in_specs=[_row_spec(D_MODEL), _row_spec(NSA_W), _row_spec(HG_W), _batch_spec(seq),
                  _full_spec((NSA_W + HG_W, D_MODEL))],
        out_specs=_row_spec(D_MODEL),
        out_shape=jax.ShapeDtypeStruct((t, D_MODEL), F32),
        compiler_params=_cparams(("arbitrary",)),
        name="out_proj",
    )(x2, a, r, gm, w_bf)


def _glu_kernel(x_ref, g_ref, sc_ref, sh_ref, w_ref, o_ref):
    h = _norm_mod(x_ref[...], g_ref[...], sc_ref[0], sh_ref[0]).astype(BF16)
    tn = 512
    for j in range(D_MODEL // tn):
        a = jnp.dot(h, w_ref[:, j * tn:(j + 1) * tn], preferred_element_type=F32)
        gate = jnp.dot(h, w_ref[:, D_MODEL + j * tn:D_MODEL + (j + 1) * tn], preferred_element_type=F32)
        o_ref[:, j * tn:(j + 1) * tn] = a * _sigmoid(gate)


def _glu_proj(x2, seq, g, sc, sh, w_bf):
    t = x2.shape[0]
    return pl.pallas_call(
        _glu_kernel,
        grid=(t // TM,),
        in_specs=[_row_spec(D_MODEL), _full_spec((1, D_MODEL)), _batch_spec(seq), _batch_spec(seq),
                  _full_spec((D_MODEL, 2 * D_MODEL))],
        out_specs=_row_spec(D_MODEL),
        out_shape=jax.ShapeDtypeStruct((t, D_MODEL), F32),
        compiler_params=_cparams(("arbitrary",)),
        name="conv_glu",
    )(x2, g, sc, sh, w_bf)


CONV_HALO = 32
CONV_RB = 64
SUBLANES = 8
CONV_SHIFT_ROWS = TM + CONV_HALO - SUBLANES


def _conv_kernel(seq, x_ref, u_ref, halo_ref, dw_ref, dwb_ref, lng_ref, lnb_ref, gm_ref, w_ref, o_ref,
                 ext_ref, shift_ref, act_ref):
    i = pl.program_id(0)
    first = (i * TM) % seq == 0
    ext_ref[:CONV_HALO, :] = jnp.where(first, 0.0, halo_ref[...])
    ext_ref[CONV_HALO:, :] = u_ref[...]
    for s in range(1, SUBLANES):
        shift_ref[s - 1] = ext_ref[s:s + CONV_SHIFT_ROWS, :]
    off = CONV_HALO - (CONV_W - 1)
    for r in range(TM // CONV_RB):
        for c in range(D_MODEL // LANES):
            cs = slice(c * LANES, (c + 1) * LANES)
            acc = jnp.zeros((CONV_RB, LANES), F32)
            for k in range(CONV_W):
                s = (off + k) % SUBLANES
                lo = r * CONV_RB + off + k - s
                win = ext_ref[lo:lo + CONV_RB, cs] if s == 0 else shift_ref[s - 1, lo:lo + CONV_RB, cs]
                acc = acc + win * dw_ref[k:k + 1, cs]
            act_ref[r * CONV_RB:(r + 1) * CONV_RB, cs] = acc + dwb_ref[:, cs]
    v = act_ref[...]
    mu = jnp.mean(v, axis=-1, keepdims=True)
    var = jnp.mean(jnp.square(v - mu), axis=-1, keepdims=True)
    y = (v - mu) * lax.rsqrt(var + EPS) * lng_ref[...] + lnb_ref[...]
    y = _silu(y).astype(BF16)
    o_ref[...] = x_ref[...] + gm_ref[0] * jnp.dot(y, w_ref[...], preferred_element_type=F32)


def _conv_module(x2, seq, u, dw, dwb, lng, lnb, gm, w_bf):
    t = x2.shape[0]
    per = TM // CONV_HALO
    return pl.pallas_call(
        functools.partial(_conv_kernel, seq),
        grid=(t // TM,),
        in_specs=[_row_spec(D_MODEL), _row_spec(D_MODEL),
                  pl.BlockSpec((CONV_HALO, D_MODEL), lambda i: (jnp.maximum(i * per - 1, 0), 0)),
                  _full_spec((CONV_HALO, D_MODEL)), _full_spec((1, D_MODEL)), _full_spec((1, D_MODEL)),
                  _full_spec((1, D_MODEL)), _batch_spec(seq), _full_spec((D_MODEL, D_MODEL))],
        out_specs=_row_spec(D_MODEL),
        out_shape=jax.ShapeDtypeStruct((t, D_MODEL), F32),
        scratch_shapes=[pltpu.VMEM((TM + CONV_HALO, D_MODEL), F32),
                        pltpu.VMEM((SUBLANES - 1, CONV_SHIFT_ROWS, D_MODEL), F32),
                        pltpu.VMEM((TM, D_MODEL), F32)],
        compiler_params=_cparams(("arbitrary",)),
        name="conv_module",
    )(x2, u, u, dw, dwb, lng, lnb, gm, w_bf)


ROW_TILE = (SUBLANES, LANES)


def _store_tile_rows(ref, v):
    ref[...] = v.reshape((v.shape[0],) + ROW_TILE)


def _load_tile_rows(ref):
    return ref[...].reshape(ref.shape[0], D_MODEL)


def _first_index_of_max(v, lane):
    m = jnp.max(v, axis=-1, keepdims=True)
    idx = jnp.min(jnp.where(v == m, lane, LANES), axis=-1, keepdims=True)
    return m, idx


def _router_kernel(x_ref, g_ref, sc_ref, sh_ref, wr_ref, h_ref, ids_ref, gates_ref, cnt_ref, carry_ref):
    i = pl.program_id(0)

    @pl.when(i == 0)
    def _():
        carry_ref[...] = jnp.zeros_like(carry_ref)

    h = _norm_mod(x_ref[...], g_ref[...], sc_ref[0], sh_ref[0])
    _store_tile_rows(h_ref, h)
    logits = _dot_bf16x3(h, wr_ref[...], (((1,), (0,)), ((), ())))
    lane = lax.broadcasted_iota(jnp.int32, logits.shape, 1)
    gl = jnp.where(lane < MOE_GROUPS, logits, -jnp.inf)
    gmax, gsel = _first_index_of_max(gl, lane)
    g_gate = 1.0 / jnp.sum(jnp.exp(gl - gmax), axis=-1, keepdims=True)
    lo = MOE_GROUPS + MOE_EPG * gsel
    in_grp = (lane >= lo) & (lane < lo + MOE_EPG)
    el = jnp.where(in_grp, logits, -jnp.inf)
    emax = jnp.max(el, axis=-1, keepdims=True)
    p = jnp.exp(el - emax)
    prob = p / jnp.sum(p, axis=-1, keepdims=True)
    cand = jnp.where(in_grp, prob, -1.0)
    v1, i1 = _first_index_of_max(cand, lane)
    cand2 = jnp.where(lane == i1, -1.0, cand)
    v2, i2 = _first_index_of_max(cand2, lane)
    den = v1 + v2
    w1 = g_gate * v1 / den
    w2 = g_gate * v2 / den
    e1 = i1 - MOE_GROUPS
    e2 = i2 - MOE_GROUPS
    oh1 = (lane == e1).astype(F32)
    oh2 = (lane == e2).astype(F32)
    rows = lax.broadcasted_iota(jnp.int32, (TM, TM), 0)
    cols = lax.broadcasted_iota(jnp.int32, (TM, TM), 1)
    before = (cols < rows).astype(BF16)
    prior = jnp.dot(before, (oh1 + oh2).astype(BF16), preferred_element_type=F32) + carry_ref[...]
    r1 = jnp.sum(oh1 * prior, axis=-1, keepdims=True)
    r2 = jnp.sum(oh2 * prior, axis=-1, keepdims=True)
    carry_ref[...] = carry_ref[...] + jnp.sum(oh1 + oh2, axis=0, keepdims=True)
    cnt_ref[...] = jnp.broadcast_to(carry_ref[...], cnt_ref.shape)
    r1 = r1.astype(jnp.int32)
    r2 = r2.astype(jnp.int32)
    ids_ref[...] = jnp.where(lane == 0, e1, jnp.where(lane == 1, e2, jnp.where(lane == 2, r1, jnp.where(lane == 3, r2, 0))))
    gates_ref[...] = jnp.where(lane == 0, w1, jnp.where(lane == 1, w2, 0.0))


def _router(x2, seq, g, sc, sh, wr):
    t = x2.shape[0]
    return pl.pallas_call(
        _router_kernel,
        grid=(t // TM,),
        in_specs=[_row_spec(D_MODEL), _full_spec((1, D_MODEL)), _batch_spec(seq), _batch_spec(seq),
                  _full_spec((D_MODEL, LANES))],
        out_specs=[pl.BlockSpec((TM,) + ROW_TILE, lambda i: (i, 0, 0)), _row_spec(LANES), _row_spec(LANES),
                   _full_spec((8, LANES))],
        out_shape=[jax.ShapeDtypeStruct((t,) + ROW_TILE, F32), jax.ShapeDtypeStruct((t, LANES), jnp.int32),
                   jax.ShapeDtypeStruct((t, LANES), F32), jax.ShapeDtypeStruct((8, LANES), F32)],
        scratch_shapes=[pltpu.VMEM((1, LANES), F32)],
        compiler_params=_cparams(("arbitrary",)),
        name="moe_router",
    )(x2, g, sc, sh, wr)


def _row_copy(src_ref, src_row, dst_ref, dst_row, sem):
    return pltpu.make_async_copy(src_ref.at[src_row], dst_ref.at[dst_row], sem)


def _dispatch_kernel(dest_ref, zfill_ref, h_ref, xs_ref, buf_ref, zero_ref, sem_ref, zsem_ref):
    i = pl.program_id(0)
    n = pl.num_programs(0)
    slot = i % 2

    def zero_copy(k):
        return pltpu.make_async_copy(zero_ref, xs_ref.at[pl.ds(zfill_ref[k], MOE_BM)], zsem_ref.at[0])

    @pl.when(i == 0)
    def _():
        zero_ref[...] = jnp.zeros_like(zero_ref)

        def start(k, c):
            @pl.when(zfill_ref[k] >= 0)
            def _():
                zero_copy(k).start()
            return c
        lax.fori_loop(0, 2 * MOE_EXPERTS, start, 0)

        def drain(k, c):
            @pl.when(zfill_ref[k] >= 0)
            def _():
                zero_copy(k).wait()
            return c
        lax.fori_loop(0, 2 * MOE_EXPERTS, drain, 0)

    def wait_slot(s):
        for _ in range(2):
            pltpu.make_async_copy(buf_ref.at[s], xs_ref.at[pl.ds(0, TM)], sem_ref.at[s]).wait()

    @pl.when(i >= 2)
    def _():
        wait_slot(slot)

    buf_ref[slot] = h_ref[...]

    def issue(r, c):
        a = 2 * (i * TM + r)
        _row_copy(buf_ref.at[slot], r, xs_ref, dest_ref[a], sem_ref.at[slot]).start(priority=0)
        _row_copy(buf_ref.at[slot], r, xs_ref, dest_ref[a + 1], sem_ref.at[slot]).start(priority=1)
        return c
    lax.fori_loop(0, TM, issue, 0)

    @pl.when(i == n - 1)
    def _():
        wait_slot(slot)

        @pl.when(n >= 2)
        def _():
            wait_slot(1 - slot)


def _dispatch(dest, zfill, h, n_rows):
    t = h.shape[0]
    return pl.pallas_call(
        _dispatch_kernel,
        grid_spec=pltpu.PrefetchScalarGridSpec(
            num_scalar_prefetch=2,
            grid=(t // TM,),
            in_specs=[pl.BlockSpec((TM,) + ROW_TILE, lambda i, d, z: (i, 0, 0))],
            out_specs=pl.BlockSpec(memory_space=pl.ANY),
            scratch_shapes=[pltpu.VMEM((2, TM) + ROW_TILE, F32), pltpu.VMEM((MOE_BM,) + ROW_TILE, F32),
                            pltpu.SemaphoreType.DMA((2,)), pltpu.SemaphoreType.DMA((1,))],
        ),
        out_shape=jax.ShapeDtypeStruct((n_rows,) + ROW_TILE, F32),
        compiler_params=_cparams(("arbitrary",)),
        name="moe_dispatch",
    )(dest, zfill, h)


def _dest_kernel(ids_ref, pstart_ref, o_ref):
    ids = ids_ref[...]
    lane = lax.broadcasted_iota(jnp.int32, ids.shape, 1)
    pstart = pstart_ref[...]

    def row_of(k):
        base = jnp.sum(jnp.where(lane == ids[:, k:k + 1], pstart, 0.0), axis=-1, keepdims=True)
        return base.astype(jnp.int32) + ids[:, 2 + k:3 + k]

    o_ref[...] = jnp.where(lane == 0, row_of(0), jnp.where(lane == 1, row_of(1), 0))


def _dest_rows(ids, pstarts):
    t = ids.shape[0]
    pstart_row = jnp.zeros((1, LANES), F32).at[0, :MOE_EXPERTS].set(pstarts.astype(F32))
    out = pl.pallas_call(
        _dest_kernel,
        grid=(t // TM,),
        in_specs=[_row_spec(LANES), _full_spec((1, LANES))],
        out_specs=_row_spec(LANES),
        out_shape=jax.ShapeDtypeStruct((t, LANES), jnp.int32),
        compiler_params=_cparams(("arbitrary",)),
        name="moe_dest",
    )(ids, pstart_row)
    return out[:, :2].reshape(2 * t)


def _expert_kernel(layer, be_ref, nb_ref, nxt_ref, xs_ref, w1_hbm, w3_hbm, w2_hbm, y_ref,
                   w1f_ref, w3f_ref, w2f_ref, w1b_ref, w3b_ref, w2b_ref, ord_ref, sem_ref):
    i = pl.program_id(0)
    used = i < nb_ref[0]
    prev = be_ref[jnp.maximum(i - 1, 0)]

    def fetch(e, s):
        return (pltpu.make_async_copy(w1_hbm.at[layer, e], w1f_ref.at[s], sem_ref.at[s]),
                pltpu.make_async_copy(w3_hbm.at[layer, e], w3f_ref.at[s], sem_ref.at[s]),
                pltpu.make_async_copy(w2_hbm.at[layer, e], w2f_ref.at[s], sem_ref.at[s]))

    @pl.when(i == 0)
    def _():
        ord_ref[0] = 0
        for c in fetch(be_ref[0], 0):
            c.start()

    @pl.when(used & ((i == 0) | (be_ref[i] != prev)))
    def _():
        s = ord_ref[0] % 2
        for c in fetch(be_ref[i], s):
            c.wait()
        w1b_ref[...] = w1f_ref[s].astype(BF16)
        w3b_ref[...] = w3f_ref[s].astype(BF16)
        w2b_ref[...] = w2f_ref[s].astype(BF16)

        @pl.when(nxt_ref[i] >= 0)
        def _():
            for c in fetch(nxt_ref[i], 1 - s):
                c.start()
        ord_ref[0] = ord_ref[0] + 1

    @pl.when(used)
    def _():
        xb = _load_tile_rows(xs_ref).astype(BF16)
        a = jnp.dot(xb, w1b_ref[...], preferred_element_type=F32)
        b = jnp.dot(xb, w3b_ref[...], preferred_element_type=F32)
        hmid = (_silu(a) * b).astype(BF16)
        _store_tile_rows(y_ref, jnp.dot(hmid, w2b_ref[...], preferred_element_type=F32))

    @pl.when(i >= nb_ref[0])
    def _():
        y_ref[...] = jnp.zeros_like(y_ref)


def _experts(blk_exp, n_used, blk_next, xs, layer, w1, w3, w2):
    n_rows = xs.shape[0]
    nb = n_rows // MOE_BM

    def row_map(i, be, nu, nx):
        return (jnp.minimum(i, nu[0] - 1), 0, 0)

    return pl.pallas_call(
        functools.partial(_expert_kernel, layer),
        grid_spec=pltpu.PrefetchScalarGridSpec(
            num_scalar_prefetch=3,
            grid=(nb,),
            in_specs=[pl.BlockSpec((MOE_BM,) + ROW_TILE, row_map),
                      pl.BlockSpec(memory_space=pl.ANY), pl.BlockSpec(memory_space=pl.ANY),
                      pl.BlockSpec(memory_space=pl.ANY)],
            out_specs=pl.BlockSpec((MOE_BM,) + ROW_TILE, lambda i, be, nu, nx: (i, 0, 0)),
            scratch_shapes=[pltpu.VMEM((2, D_MODEL, EXPERT_FF), F32), pltpu.VMEM((2, D_MODEL, EXPERT_FF), F32),
                            pltpu.VMEM((2, EXPERT_FF, D_MODEL), F32),
                            pltpu.VMEM((D_MODEL, EXPERT_FF), BF16), pltpu.VMEM((D_MODEL, EXPERT_FF), BF16),
                            pltpu.VMEM((EXPERT_FF, D_MODEL), BF16),
                            pltpu.SMEM((1,), jnp.int32), pltpu.SemaphoreType.DMA((2,))],
        ),
        out_shape=jax.ShapeDtypeStruct((n_rows,) + ROW_TILE, F32),
        compiler_params=_cparams(("arbitrary",)),
        name="moe_experts",
    )(blk_exp, n_used, blk_next, xs, w1, w3, w2)


def _combine_kernel(dest_ref, x_ref, gates_ref, gf_ref, y_ref, o_ref, buf_ref, sem_ref):
    i = pl.program_id(0)
    n = pl.num_programs(0)
    slot = i % 2

    def issue(tile, s):
        def body(r, c):
            a = 2 * (tile * TM + r)
            _row_copy(y_ref, dest_ref[a], buf_ref.at[s, 0], r, sem_ref.at[s]).start(priority=0)
            _row_copy(y_ref, dest_ref[a + 1], buf_ref.at[s, 1], r, sem_ref.at[s]).start(priority=1)
            return c
        lax.fori_loop(0, TM, body, 0)

    @pl.when(i == 0)
    def _():
        issue(0, 0)

    @pl.when(i + 1 < n)
    def _():
        issue(i + 1, 1 - slot)

    for k in range(2):
        pltpu.make_async_copy(y_ref.at[pl.ds(0, TM)], buf_ref.at[slot, k], sem_ref.at[slot]).wait()

    w0 = gates_ref[:, 0:1]
    w1 = gates_ref[:, 1:2]
    mix = _load_tile_rows(buf_ref.at[slot, 0]) * w0 + _load_tile_rows(buf_ref.at[slot, 1]) * w1
    o_ref[...] = x_ref[...] + gf_ref[0] * mix


def _combine(dest, x2, seq, gates, gf, y):
    t = x2.shape[0]
    return pl.pallas_call(
        _combine_kernel,
        grid_spec=pltpu.PrefetchScalarGridSpec(
            num_scalar_prefetch=1,
            grid=(t // TM,),
            in_specs=[pl.BlockSpec((TM, D_MODEL), lambda i, d: (i, 0)),
                      pl.BlockSpec((TM, LANES), lambda i, d: (i, 0)),
                      pl.BlockSpec((1, 1, D_MODEL), lambda i, d: ((i * TM) // seq, 0, 0)),
                      pl.BlockSpec(memory_space=pl.ANY)],
            out_specs=pl.BlockSpec((TM, D_MODEL), lambda i, d: (i, 0)),
            scratch_shapes=[pltpu.VMEM((2, 2, TM) + ROW_TILE, F32), pltpu.SemaphoreType.DMA((2,))],
        ),
        out_shape=jax.ShapeDtypeStruct((t, D_MODEL), F32),
        compiler_params=_cparams(("arbitrary",)),
        name="moe_combine",
    )(dest, x2, gates, gf, y)


def _moe_layer(x2, seq, g, sc, sh, gf, wr, layer, w1, w3, w2):
    t = x2.shape[0]
    h, ids, gates, cnt = _router(x2, seq, g, sc, sh, wr)
    counts = cnt[0, :MOE_EXPERTS].astype(jnp.int32)
    padded = (counts + MOE_BM - 1) // MOE_BM * MOE_BM
    pends = jnp.cumsum(padded)
    pstarts = pends - padded
    dest = _dest_rows(ids, pstarts)
    n_rows = 2 * t + MOE_EXPERTS * MOE_BM
    nb = n_rows // MOE_BM
    blk_start = jnp.arange(nb, dtype=jnp.int32) * MOE_BM
    blk_exp = jnp.minimum(jnp.sum((pends[None, :] <= blk_start[:, None]).astype(jnp.int32), axis=1), MOE_EXPERTS - 1)
    n_used = (pends[-1:] // MOE_BM).astype(jnp.int32)
    tail = pends[-1] + jnp.arange(MOE_EXPERTS, dtype=jnp.int32) * MOE_BM
    zfill = jnp.concatenate([pstarts + counts, jnp.where(tail < n_rows, tail, -1)]).astype(jnp.int32)
    xs = _dispatch(dest, zfill, h, n_rows)
    eid = jnp.arange(MOE_EXPERTS, dtype=jnp.int32)
    later = (padded[None, :] > 0) & (eid[None, :] > eid[:, None])
    next_exp = jnp.min(jnp.where(later, eid[None, :], MOE_EXPERTS), axis=1)
    next_exp = jnp.where(next_exp < MOE_EXPERTS, next_exp, -1).astype(jnp.int32)
    y = _experts(blk_exp, n_used, next_exp[blk_exp], xs, layer, w1, w3, w2)
    return _combine(dest, x2, seq, gates, gf, y)


HG_LT = 256


def _sublane_roll(x, shift):
    return pltpu.roll(x, shift % x.shape[0], 0)


def _hgrn_chunk(hq, hf, hi, lb, state_t):
    c, sub = HG_CHUNK, HG_SUB
    q = _silu(hq)
    f = lb + (1.0 - lb) * _sigmoid(hf)
    lf = jnp.log(f)
    k = 1.0 - f
    v = hi
    row = lax.broadcasted_iota(jnp.int32, (c, c), 0)
    col = lax.broadcasted_iota(jnp.int32, (c, c), 1)
    start = (row // sub) * sub
    tri_local = ((col >= start) & (col <= row)).astype(F32)
    tri_before = (col < start).astype(F32)
    bloc = jnp.dot(tri_local, lf, precision=HIGHEST, preferred_element_type=F32)
    rref = jnp.dot(tri_before, lf, precision=HIGHEST, preferred_element_type=F32)
    b = rref + bloc
    qt = q * jnp.exp(bloc)
    ridx = lax.broadcasted_iota(jnp.int32, (c, HG_DK), 0)

    blocks = [jnp.zeros((sub, c), F32)]
    for i in range(1, c // sub):
        r_i = rref[i * sub:i * sub + 1, :]
        live = ridx < i * sub
        k_i = jnp.where(live, k * jnp.exp(jnp.where(live, r_i - b, 0.0)), 0.0)
        blocks.append(lax.dot_general(qt[i * sub:(i + 1) * sub].astype(BF16), k_i.astype(BF16),
                                      (((1,), (1,)), ((), ())), preferred_element_type=F32))
    attn_off = jnp.concatenate(blocks, axis=0)
    o = jnp.dot(attn_off.astype(BF16), v.astype(BF16), preferred_element_type=F32)

    pos = ridx % sub
    for d in range(sub):
        qd = q if d == 0 else _sublane_roll(q, -d)
        bd = bloc if d == 0 else _sublane_roll(bloc, -d)
        ok = pos + d < sub
        e = qd * k * jnp.exp(jnp.where(ok, bd - bloc, -jnp.inf))
        w = jnp.sum(e, axis=-1, keepdims=True) * v
        o = o + (w if d == 0 else _sublane_roll(w, d))

    qe = qt * jnp.exp(rref)
    o = o + lax.dot_general(qe.astype(BF16), state_t.astype(BF16), (((1,), (1,)), ((), ())),
                            preferred_element_type=F32)
    b_end = b[c - 1:c, :]
    kd = k * jnp.exp(b_end - b)
    new_state = state_t * jnp.exp(b_end) + lax.dot_general(v.astype(BF16), kd.astype(BF16), (((0,), (0,)), ((), ())),
                                                           preferred_element_type=F32)
    return o, new_state


def _hgrn_kernel(hq_ref, hf_ref, hi_ref, hg_ref, lb_ref, og_ref, o_ref, state_ref):
    @pl.when(pl.program_id(0) == 0)
    def _():
        state_ref[...] = jnp.zeros_like(state_ref)

    og = og_ref[...]

    def body(ci, carry):
        rows = pl.ds(pl.multiple_of(ci * HG_CHUNK, HG_CHUNK), HG_CHUNK)
        for b in range(hq_ref.shape[0]):
            for h in range(HG_HEADS):
                cs = slice(h * HG_DK, (h + 1) * HG_DK)
                o, st = _hgrn_chunk(hq_ref[b, rows, cs], hf_ref[b, rows, cs], hi_ref[b, rows, cs], lb_ref[:, cs],
                                    state_ref[b, h])
                state_ref[b, h] = st
                on = o * lax.rsqrt(jnp.mean(o * o, axis=-1, keepdims=True) + EPS) * og
                o_ref[b, rows, cs] = on * _silu(hg_ref[b, rows, cs])
        return carry
    lax.fori_loop(0, HG_LT // HG_CHUNK, body, 0)


def _hgrn(z, batch, seq, lb, o_gain):
    t = z.shape[0]
    z3 = z.reshape(batch, seq, z.shape[1])

    def col(base):
        return pl.BlockSpec((batch, HG_LT, HG_W), lambda l: (0, l, base // HG_W))

    out = pl.pallas_call(
        _hgrn_kernel,
        grid=(seq // HG_LT,),
        in_specs=[col(Z_HQ), col(Z_HF), col(Z_HI), col(Z_HG),
                  pl.BlockSpec((1, HG_W), lambda l: (0, 0)),
                  pl.BlockSpec((1, HG_DV), lambda l: (0, 0))],
        out_specs=pl.BlockSpec((batch, HG_LT, HG_W), lambda l: (0, l, 0)),
        out_shape=jax.ShapeDtypeStruct((batch, seq, HG_W), F32),
        scratch_shapes=[pltpu.VMEM((batch, HG_HEADS, HG_DV, HG_DK), F32)],
        compiler_params=_cparams(("arbitrary",)),
        name="hgrn2",
    )(z3, z3, z3, z3, lb.reshape(1, -1), o_gain.reshape(1, -1))
    return out.reshape(t, HG_W)


def _rms64(v, gain):
    return v * lax.rsqrt(jnp.mean(v * v, axis=-1, keepdims=True) + EPS) * gain


def _nsa_prep_kernel(kc_ref, vc_ref, ks_ref, vs_ref, kw_ref, vw_ref, kg_ref,
                     kcr_ref, vcr_ref, ksn_ref, vsb_ref, kwn_ref, vwb_ref):
    ones_col = (lax.broadcasted_iota(jnp.int32, (TM, NSA_VW - NSA_HD), 1) == 0).astype(F32)

    def with_ones(v):
        return jnp.concatenate([v, ones_col], axis=1).astype(BF16)

    for g in range(NSA_KV_GROUPS):
        cs = slice(g * NSA_HD, (g + 1) * NSA_HD)
        kcr_ref[0, g] = kc_ref[:, cs]
        vcr_ref[0, g] = vc_ref[:, cs]
        ksn_ref[0, g] = _rms64(ks_ref[:, cs], kg_ref[1:2, :]).astype(BF16)
        vsb_ref[0, g] = with_ones(vs_ref[:, cs])
        kwn_ref[0, g] = _rms64(kw_ref[:, cs], kg_ref[2:3, :]).astype(BF16)
        vwb_ref[0, g] = with_ones(vw_ref[:, cs])


def _nsa_prep(z, batch, seq, k_gain):
    per = seq // TM

    def col(base):
        return pl.BlockSpec((TM, NSA_KV_W), lambda i: (i, base // NSA_KV_W))

    def out(dtype, width=NSA_HD):
        return (pl.BlockSpec((1, NSA_KV_GROUPS, TM, width), lambda i: (i // per, 0, i % per, 0)),
                jax.ShapeDtypeStruct((batch, NSA_KV_GROUPS, seq, width), dtype))

    outs = [out(F32), out(F32), out(BF16), out(BF16, NSA_VW), out(BF16), out(BF16, NSA_VW)]
    return pl.pallas_call(
        _nsa_prep_kernel,
        grid=(batch * per,),
        in_specs=[col(Z_KC), col(Z_VC), col(Z_KS), col(Z_VS), col(Z_KW), col(Z_VW), _full_spec((3, NSA_HD))],
        out_specs=[o[0] for o in outs],
        out_shape=[o[1] for o in outs],
        compiler_params=_cparams(("arbitrary",)),
        name="nsa_prep",
    )(z, z, z, z, z, z, k_gain)


def _nsa_compress_kernel(ak_ref, av_ref, wk_ref, wv_ref, pe_ref, kg_ref, kc_ref, vc_ref):
    half = CMP_STRIDE * NSA_HD

    def compress(a, w_ref):
        top = jnp.dot(a, w_ref[:half, :], precision=HIGHEST, preferred_element_type=F32)
        bot = jnp.dot(a, w_ref[half:, :], precision=HIGHEST, preferred_element_type=F32)
        pe = jnp.dot(pe_ref[...], w_ref[...], precision=HIGHEST, preferred_element_type=F32)
        return top + _sublane_roll(bot, -1) + pe[0:1]

    kc_ref[0, 0] = _rms64(compress(ak_ref[0, 0], wk_ref), kg_ref[0:1, :])
    vc_ref[0, 0] = compress(av_ref[0, 0], wv_ref)


def _nsa_compress(kcr, vcr, w_ck, w_cv, cmp_pe, k_gain):
    batch, groups, seq, _ = kcr.shape
    nc = seq // CMP_STRIDE
    wide = CMP_STRIDE * NSA_HD
    spec = pl.BlockSpec((1, 1, nc, wide), lambda b, g: (b, g, 0, 0))
    ospec = pl.BlockSpec((1, 1, nc, NSA_HD), lambda b, g: (b, g, 0, 0))
    oshape = jax.ShapeDtypeStruct((batch, groups, nc, NSA_HD), F32)
    full2 = lambda shape: pl.BlockSpec(shape, lambda b, g: (0, 0))
    return pl.pallas_call(
        _nsa_compress_kernel,
        grid=(batch, groups),
        in_specs=[spec, spec, full2((2 * wide, NSA_HD)), full2((2 * wide, NSA_HD)), full2((8, 2 * wide)),
                  full2((3, NSA_HD))],
        out_specs=[ospec, ospec],
        out_shape=[oshape, oshape],
        compiler_params=_cparams(("arbitrary", "arbitrary")),
        name="nsa_compress",
    )(kcr.reshape(batch, groups, nc, wide), vcr.reshape(batch, groups, nc, wide), w_ck, w_cv,
      jnp.broadcast_to(cmp_pe.reshape(1, 2 * wide), (8, 2 * wide)), k_gain)


def _nsa_kernel(seq, kchunk, q_ref, gate_ref, kc_ref, vc_ref, ks_ref, vs_ref, kw_ref, vw_ref, qg_ref, agg_ref,
                o_ref):
    qi = pl.program_id(2)
    t0 = qi * Q_BLOCK
    jn = NSA_QPG
    nc = seq // CMP_STRIDE
    n_slc = seq // SLC_BLOCK
    n_sel = min(N_SELECT, n_slc)
    nt = (((1,), (1,)), ((), ()))

    qraw = q_ref[...]
    qs = [_rms64(qraw[:, j * NSA_HD:(j + 1) * NSA_HD], qg_ref[...]) * (NSA_HD ** -0.5) for j in range(jn)]
    qb = [q.astype(BF16) for q in qs]

    def pos_of(shape):
        return t0 + lax.broadcasted_iota(jnp.int32, shape, 0)

    cmp_end = lax.broadcasted_iota(jnp.int32, (Q_BLOCK, nc), 1) * CMP_STRIDE + (CMP_BLOCK - 1)
    valid_c = cmp_end <= pos_of((Q_BLOCK, nc))
    sees_any = pos_of((Q_BLOCK, 1)) >= CMP_BLOCK - 1
    kc_hi, kc_lo = _split_bf16(kc_ref[0, 0], 2)
    vcb = vc_ref[0, 0].astype(BF16)
    o_c = []
    imp = None
    for j in range(jn):
        qh, ql = _split_bf16(qs[j], 2)
        s = (lax.dot_general(qh, kc_hi, nt, preferred_element_type=F32)
             + (lax.dot_general(qh, kc_lo, nt, preferred_element_type=F32)
                + lax.dot_general(ql, kc_hi, nt, preferred_element_type=F32)))
        sm = jnp.where(valid_c, s, NEG)
        e = jnp.exp(sm - jnp.max(sm, axis=-1, keepdims=True))
        p = e * jnp.where(sees_any, 1.0 / jnp.sum(e, axis=-1, keepdims=True), 0.0)
        o_c.append(jnp.dot(p.astype(BF16), vcb, preferred_element_type=F32))
        imp = p if imp is None else imp + p

    agg = agg_ref[...].astype(BF16)
    imp_s = sum(jnp.dot(part, agg, preferred_element_type=F32) for part in _split_bf16(imp, 3))
    ids = lax.broadcasted_iota(jnp.int32, (Q_BLOCK, n_slc), 1)
    q_blk = pos_of((Q_BLOCK, n_slc)) // SLC_BLOCK
    forced = (ids == 0) | (ids == q_blk) | (ids == q_blk - 1)
    score_t = jnp.where(ids > q_blk, NEG, jnp.where(forced, FORCE, imp_s)).T
    ids_t = lax.broadcasted_iota(jnp.int32, (n_slc, Q_BLOCK), 0).astype(F32)
    cur = score_t
    sel_t = jnp.zeros((n_slc, Q_BLOCK), F32)
    for _ in range(n_sel):
        m = jnp.max(cur, axis=0, keepdims=True)
        first = jnp.min(jnp.where(cur == m, ids_t, float(n_slc)), axis=0, keepdims=True)
        hit = ids_t == first
        sel_t = jnp.where(hit, 1.0, sel_t)
        cur = jnp.where(hit, -jnp.inf, cur)
    sel = jnp.where(score_t > NEG / 2, sel_t, 0.0).T.astype(BF16)

    span = Q_BLOCK + WINDOW
    start = pl.multiple_of(jnp.maximum(t0 - WINDOW, 0), Q_BLOCK)
    wsl = pl.ds(start, span)
    kpos = start + lax.broadcasted_iota(jnp.int32, (Q_BLOCK, span), 1)
    wpos = pos_of((Q_BLOCK, span))
    bias_w = jnp.where((kpos <= wpos) & (kpos > wpos - WINDOW), 0.0, NEG)
    kw = kw_ref[0, 0, wsl, :]
    vw = vw_ref[0, 0, wsl, :]
    o_w = []
    for j in range(jn):
        s = lax.dot_general(qb[j], kw, nt, preferred_element_type=F32) + bias_w
        e = jnp.exp((s - jnp.max(s, axis=-1, keepdims=True)).astype(BF16))
        ow = jnp.dot(e, vw, preferred_element_type=F32)
        o_w.append(ow[:, :NSA_HD] / ow[:, NSA_HD:NSA_HD + 1])

    per_chunk = kchunk // SLC_BLOCK
    blk_of_key = lax.broadcasted_iota(jnp.int32, (n_slc, kchunk), 1) // SLC_BLOCK
    blk_row = lax.broadcasted_iota(jnp.int32, (n_slc, kchunk), 0)
    key_lane = lax.broadcasted_iota(jnp.int32, (Q_BLOCK, kchunk), 1)
    qpos = pos_of((Q_BLOCK, kchunk))

    def sel_body(c, carry):
        ksl = pl.ds(pl.multiple_of(c * kchunk, kchunk), kchunk)
        k_c = ks_ref[0, 0, ksl, :]
        v_c = vs_ref[0, 0, ksl, :]
        expand = (blk_row == blk_of_key + c * per_chunk).astype(BF16)
        chosen = jnp.dot(sel, expand, preferred_element_type=F32)
        bias = jnp.where((chosen > 0.5) & (key_lane + c * kchunk <= qpos), 0.0, NEG)
        m_run, acc = carry
        s = lax.dot_general(qb_all, k_c, nt, preferred_element_type=F32) + jnp.concatenate([bias] * jn, axis=0)
        m_new = jnp.maximum(m_run, jnp.max(s, axis=-1, keepdims=True))
        alpha = jnp.exp(m_run - m_new)
        p = jnp.exp((s - m_new).astype(BF16))
        acc_new = acc * alpha + jnp.dot(p, v_c, preferred_element_type=F32)
        return m_new, acc_new

    rows = jn * Q_BLOCK
    qb_all = jnp.concatenate(qb, axis=0)
    n_chunks = (t0 + Q_BLOCK + kchunk - 1) // kchunk
    init = (jnp.full((rows, 1), NEG, F32), jnp.zeros((rows, NSA_VW), F32))
    _, acc_fin = lax.fori_loop(0, n_chunks, sel_body, init)
    o_sel = acc_fin[:, :NSA_HD] / acc_fin[:, NSA_HD:NSA_HD + 1]

    gates = _sigmoid(gate_ref[...])
    outs = []
    for j in range(jn):
        o_s = o_sel[j * Q_BLOCK:(j + 1) * Q_BLOCK]
        outs.append(gates[:, 3 * j:3 * j + 1] * o_c[j] + gates[:, 3 * j + 1:3 * j + 2] * o_s
                    + gates[:, 3 * j + 2:3 * j + 3] * o_w[j])
    o_ref[...] = jnp.concatenate(outs, axis=1)


def _selection_weights(seq):
    nc = seq // CMP_STRIDE
    n_slc = seq // SLC_BLOCK
    ratio = SLC_BLOCK // CMP_STRIDE
    n = jnp.arange(nc)[:, None]
    j = jnp.arange(n_slc)[None, :]
    o = n - ratio * j + (CMP_BLOCK // CMP_STRIDE - 1)
    w = jnp.asarray(SEL_AGG_W, F32)
    return jnp.where((o >= 0) & (o < len(SEL_AGG_W)), w[jnp.clip(o, 0, len(SEL_AGG_W) - 1)], 0.0)


def _nsa(z, batch, seq, w_ck, w_cv, cmp_pe, q_gain, k_gain):
    t = z.shape[0]
    kcr, vcr, ksn, vsb, kwn, vwb = _nsa_prep(z, batch, seq, k_gain)
    kc, vc = _nsa_compress(kcr, vcr, w_ck, w_cv, cmp_pe, k_gain)
    nq = seq // Q_BLOCK
    nc = seq // CMP_STRIDE
    n_slc = seq // SLC_BLOCK
    kchunk = min(NSA_KCHUNK, seq)
    qw = NSA_QPG * NSA_HD
    cmp_spec = pl.BlockSpec((1, 1, nc, NSA_HD), lambda b, g, i: (b, g, 0, 0))
    key_spec = pl.BlockSpec((1, 1, seq, NSA_HD), lambda b, g, i: (b, g, 0, 0))
    val_spec = pl.BlockSpec((1, 1, seq, NSA_VW), lambda b, g, i: (b, g, 0, 0))
    return pl.pallas_call(
        functools.partial(_nsa_kernel, seq, kchunk),
        grid=(batch, NSA_KV_GROUPS, nq),
        in_specs=[pl.BlockSpec((Q_BLOCK, qw), lambda b, g, i: (b * nq + i, g)),
                  pl.BlockSpec((Q_BLOCK, LANES), lambda b, g, i: (b * nq + i, Z_GATE // LANES + g)),
                  cmp_spec, cmp_spec, key_spec, val_spec, key_spec, val_spec,
                  pl.BlockSpec((1, NSA_HD), lambda b, g, i: (0, 0)),
                  pl.BlockSpec((nc, n_slc), lambda b, g, i: (0, 0))],
        out_specs=pl.BlockSpec((Q_BLOCK, qw), lambda b, g, i: (b * nq + i, g)),
        out_shape=jax.ShapeDtypeStruct((t, NSA_W), F32),
        compiler_params=_cparams(("arbitrary", "arbitrary", "arbitrary")),
        name="nsa_attention",
    )(z, z, kc, vc, ksn, vsb, kwn, vwb, q_gain.reshape(1, -1), _selection_weights(seq))


def _pad_in_proj(w_in):
    per_group = NSA_QPG * 3
    gate_lo = Z_GATE
    out = jnp.zeros((D_MODEL, Z_N), F32)
    out = out.at[:, :gate_lo].set(w_in[:, :gate_lo])
    for g in range(NSA_KV_GROUPS):
        out = out.at[:, gate_lo + g * LANES:gate_lo + g * LANES + per_group].set(
            w_in[:, gate_lo + g * per_group:gate_lo + (g + 1) * per_group])
    return out.at[:, Z_HQ:].set(w_in[:, gate_lo + NSA_HEADS * 3:])


def _even_mixer(x2, batch, seq, g, sc, sh, gm, w_in, w_out, w_ck, w_cv, cmp_pe, q_gain, k_gain, lb, o_gain):
    z = _in_proj(x2, seq, g, sc, sh, _pad_in_proj(w_in).astype(BF16))
    r = _hgrn(z, batch, seq, lb, o_gain)
    a = _nsa(z, batch, seq, w_ck, w_cv, cmp_pe, q_gain, k_gain)
    return _out_proj(x2, seq, a, r, gm, w_out.astype(BF16))


def _odd_mixer(x2, seq, g, sc, sh, gm, w_pw1, dw, dw_b, ln_g, ln_b, w_pw2):
    u = _glu_proj(x2, seq, g, sc, sh, w_pw1.astype(BF16))
    dw_pad = jnp.zeros((CONV_HALO, D_MODEL), F32).at[:CONV_W].set(dw)
    return _conv_module(x2, seq, u, dw_pad, dw_b.reshape(1, -1), ln_g.reshape(1, -1), ln_b.reshape(1, -1), gm,
                        w_pw2.astype(BF16))


def _router_weights(w_group, w_expert):
    wr = jnp.zeros((D_MODEL, LANES), F32)
    return wr.at[:, :MOE_GROUPS].set(w_group).at[:, MOE_GROUPS:MOE_GROUPS + MOE_EXPERTS].set(w_expert)


def _lower_bounds_kernel(l_ref, o_ref):
    logits = l_ref[...]
    e = jnp.exp(logits - jnp.max(logits, axis=0, keepdims=True))
    p = e / jnp.sum(e, axis=0, keepdims=True)
    n = logits.shape[0]
    acc = jnp.zeros_like(p[0:1])
    for i in range(n):
        acc = acc + p[i:i + 1]
        o_ref[i:i + 1, :] = acc - p[0:1]


def _lower_bounds(lb_logits):
    return pl.pallas_call(
        _lower_bounds_kernel,
        out_shape=jax.ShapeDtypeStruct(lb_logits.shape, F32),
        name="hgrn_lower_bounds",
    )(lb_logits)


def kernel(x, c, ada_w, ada_b, norm_mix, norm_ffn, mix_w_in, mix_w_out, nsa_cmp_wk, nsa_cmp_wv, nsa_cmp_pe, nsa_q_gain, nsa_k_gain, hgrn_lb_logits, hgrn_o_gain, conv_w_pw1, conv_dw, conv_dw_b, conv_ln_g, conv_ln_b, conv_w_pw2, moe_w_group, moe_w_expert, moe_w1, moe_w3, moe_w2):
    batch, seq, d = x.shape
    x2 = x.reshape(batch * seq, d)
    mod = _modulation(c, ada_w, ada_b)
    lower_bounds = _lower_bounds(hgrn_lb_logits)
    for layer in range(DEPTH):
        sh_m, sc_m, g_m, sh_f, sc_f, g_f = (mod[layer, :, k] for k in range(6))
        i = layer // 2
        gain = norm_mix[layer].reshape(1, d)
        if layer % 2 == 0:
            x2 = _even_mixer(x2, batch, seq, gain, sc_m, sh_m, g_m, mix_w_in[i], mix_w_out[i], nsa_cmp_wk[i],
                             nsa_cmp_wv[i], nsa_cmp_pe[i], nsa_q_gain[i], nsa_k_gain[i], lower_bounds[i],
                             hgrn_o_gain[i])
        else:
            x2 = _odd_mixer(x2, seq, gain, sc_m, sh_m, g_m, conv_w_pw1[i], conv_dw[i], conv_dw_b[i], conv_ln_g[i],
                            conv_ln_b[i], conv_w_pw2[i])
        x2 = _moe_layer(x2, seq, norm_ffn[layer].reshape(1, d), sc_f, sh_f, g_f,
                        _router_weights(moe_w_group[layer], moe_w_expert[layer]), layer, moe_w1, moe_w3, moe_w2)
    return x2.reshape(batch, seq, d)
```

```python
import functools

import jax
import jax.numpy as jnp
from jax import lax
from jax.experimental import pallas as pl
from jax.experimental.pallas import tpu as pltpu

F32 = jnp.float32
BF16 = jnp.bfloat16
HIGHEST = lax.Precision.HIGHEST

D_MODEL = 1024
DEPTH = 4

NSA_HEADS = 8
NSA_KV_GROUPS = 2
NSA_QPG = NSA_HEADS // NSA_KV_GROUPS
NSA_HD = 64
NSA_W = NSA_HEADS * NSA_HD
NSA_KV_W = NSA_KV_GROUPS * NSA_HD
NSA_KCHUNK = 512
NSA_VW = 128
CMP_BLOCK = 32
CMP_STRIDE = 16
SLC_BLOCK = 64
N_SELECT = 16
WINDOW = 512
Q_BLOCK = 256
SEL_AGG_W = (1.0, 2.0, 2.0, 2.0, 1.0)

HG_HEADS = 4
HG_DK = 128
HG_DV = 128
HG_W = HG_HEADS * HG_DV
HG_CHUNK = 64
HG_SUB = 8

CONV_W = 31
MOE_GROUPS = 4
MOE_EPG = 8
MOE_EXPERTS = MOE_GROUPS * MOE_EPG
EXPERT_FF = 512

EPS = 1e-6
NEG = -1e30
FORCE = 1e4

LANES = 128
VMEM_LIMIT = 56 * 1024 * 1024

Z_Q = 0
Z_KC = 512
Z_VC = 640
Z_KS = 768
Z_VS = 896
Z_KW = 1024
Z_VW = 1152
Z_GATE = 1280
Z_HQ = Z_GATE + NSA_KV_GROUPS * LANES
Z_HF = Z_HQ + HG_W
Z_HI = Z_HF + HG_W
Z_HG = Z_HI + HG_W
Z_N = Z_HG + HG_W

TM = 256
MOE_BM = 256


def _cparams(sem):
    return pltpu.CompilerParams(dimension_semantics=sem, vmem_limit_bytes=VMEM_LIMIT)


def _split_bf16(x, terms):
    parts = []
    for _ in range(terms):
        p = x.astype(BF16)
        parts.append(p)
        x = x - p.astype(F32)
    return parts


def _dot_bf16x3(a, b, dims):
    ah, al = _split_bf16(a, 2)
    bh, bl = _split_bf16(b, 2)

    def d(x, y):
        return lax.dot_general(x, y, dims, preferred_element_type=F32)
    return d(ah, bh) + (d(ah, bl) + d(al, bh))


def _sigmoid(x):
    return 1.0 / (1.0 + jnp.exp(-x))


def _silu(x):
    return x * _sigmoid(x)


def _mod_kernel(c_ref, w_ref, b_ref, o_ref):
    c = c_ref[...]
    o_ref[0] = jnp.dot(_silu(c), w_ref[0], precision=HIGHEST, preferred_element_type=F32) + b_ref[0]


def _modulation(c, ada_w, ada_b):
    b = c.shape[0]
    rows = 8
    c_pad = jnp.zeros((rows, D_MODEL), F32).at[:b].set(c)
    n = 6 * D_MODEL
    tn = 1536
    out = pl.pallas_call(
        _mod_kernel,
        grid=(DEPTH, n // tn),
        in_specs=[
            pl.BlockSpec((rows, D_MODEL), lambda l, j: (0, 0)),
            pl.BlockSpec((1, D_MODEL, tn), lambda l, j: (l, 0, j)),
            pl.BlockSpec((1, 1, tn), lambda l, j: (l, 0, j)),
        ],
        out_specs=pl.BlockSpec((1, rows, tn), lambda l, j: (l, 0, j)),
        out_shape=jax.ShapeDtypeStruct((DEPTH, rows, n), F32),
        compiler_params=_cparams(("arbitrary", "arbitrary")),
        name="adaln_mod",
    )(c_pad, ada_w, ada_b.reshape(DEPTH, 1, n))
    return out[:, :b].reshape(DEPTH, b, 6, 1, D_MODEL)


def _norm_mod(x, g, sc, sh):
    y = x * lax.rsqrt(jnp.mean(x * x, axis=-1, keepdims=True) + EPS)
    return (y * g) * (1.0 + sc) + sh


def _row_spec(width):
    return pl.BlockSpec((TM, width), lambda i: (i, 0))


def _full_spec(shape):
    nd = len(shape)
    return pl.BlockSpec(shape, lambda i: (0,) * nd)


def _batch_spec(seq):
    return pl.BlockSpec((1, 1, D_MODEL), lambda i: ((i * TM) // seq, 0, 0))


def _in_proj_kernel(x_ref, g_ref, sc_ref, sh_ref, w_ref, o_ref):
    h = _norm_mod(x_ref[...], g_ref[...], sc_ref[0], sh_ref[0]).astype(BF16)
    tn = Z_N // 4
    for j in range(4):
        o_ref[:, j * tn:(j + 1) * tn] = jnp.dot(h, w_ref[:, j * tn:(j + 1) * tn], preferred_element_type=F32)


def _in_proj(x2, seq, g, sc, sh, w_bf):
    t = x2.shape[0]
    return pl.pallas_call(
        _in_proj_kernel,
        grid=(t // TM,),
        in_specs=[_row_spec(D_MODEL), _full_spec((1, D_MODEL)), _batch_spec(seq), _batch_spec(seq),
                  _full_spec((D_MODEL, Z_N))],
        out_specs=_row_spec(Z_N),
        out_shape=jax.ShapeDtypeStruct((t, Z_N), F32),
        compiler_params=_cparams(("arbitrary",)),
        name="in_proj",
    )(x2, g, sc, sh, w_bf)


def _out_proj_kernel(x_ref, a_ref, r_ref, gm_ref, w_ref, o_ref):
    y = jnp.dot(a_ref[...].astype(BF16), w_ref[:NSA_W, :], preferred_element_type=F32)
    y = y + jnp.dot(r_ref[...].astype(BF16), w_ref[NSA_W:, :], preferred_element_type=F32)
    o_ref[...] = x_ref[...] + gm_ref[0] * y


def _out_proj(x2, seq, a, r, gm, w_bf):
    t = x2.shape[0]
    return pl.pallas_call(
        _out_proj_kernel,
        grid=(t // TM,),
        in_specs=[_row_spec(D_MODEL), _row_spec(NSA_W), _row_spec(HG_W), _batch_spec(seq),
                  _full_spec((NSA_W + HG_W, D_MODEL))],
        out_specs=_row_spec(D_MODEL),
        out_shape=jax.ShapeDtypeStruct((t, D_MODEL), F32),
        compiler_params=_cparams(("arbitrary",)),
        name="out_proj",
    )(x2, a, r, gm, w_bf)


def _glu_kernel(x_ref, g_ref, sc_ref, sh_ref, w_ref, o_ref):
    h = _norm_mod(x_ref[...], g_ref[...], sc_ref[0], sh_ref[0]).astype(BF16)
    tn = 512
    for j in range(D_MODEL // tn):
        a = jnp.dot(h, w_ref[:, j * tn:(j + 1) * tn], preferred_element_type=F32)
        gate = jnp.dot(h, w_ref[:, D_MODEL + j * tn:D_MODEL + (j + 1) * tn], preferred_element_type=F32)
        o_ref[:, j * tn:(j + 1) * tn] = a * _sigmoid(gate)


def _glu_proj(x2, seq, g, sc, sh, w_bf):
    t = x2.shape[0]
    return pl.pallas_call(
        _glu_kernel,
        grid=(t // TM,),
        in_specs=[_row_spec(D_MODEL), _full_spec((1, D_MODEL)), _batch_spec(seq), _batch_spec(seq),
                  _full_spec((D_MODEL, 2 * D_MODEL))],
        out_specs=_row_spec(D_MODEL),
        out_shape=jax.ShapeDtypeStruct((t, D_MODEL), F32),
        compiler_params=_cparams(("arbitrary",)),
        name="conv_glu",
    )(x2, g, sc, sh, w_bf)


CONV_HALO = 32
CONV_RB = 64
SUBLANES = 8
CONV_SHIFT_ROWS = TM + CONV_HALO - SUBLANES


def _conv_kernel(seq, x_ref, u_ref, halo_ref, dw_ref, dwb_ref, lng_ref, lnb_ref, gm_ref, w_ref, o_ref,
                 ext_ref, shift_ref, act_ref):
    i = pl.program_id(0)
    first = (i * TM) % seq == 0
    ext_ref[:CONV_HALO, :] = jnp.where(first, 0.0, halo_ref[...])
    ext_ref[CONV_HALO:, :] = u_ref[...]
    for s in range(1, SUBLANES):
        shift_ref[s - 1] = ext_ref[s:s + CONV_SHIFT_ROWS, :]
    off = CONV_HALO - (CONV_W - 1)
    for r in range(TM // CONV_RB):
        for c in range(D_MODEL // LANES):
            cs = slice(c * LANES, (c + 1) * LANES)
            acc = jnp.zeros((CONV_RB, LANES), F32)
            for k in range(CONV_W):
                s = (off + k) % SUBLANES
                lo = r * CONV_RB + off + k - s
                win = ext_ref[lo:lo + CONV_RB, cs] if s == 0 else shift_ref[s - 1, lo:lo + CONV_RB, cs]
                acc = acc + win * dw_ref[k:k + 1, cs]
            act_ref[r * CONV_RB:(r + 1) * CONV_RB, cs] = acc + dwb_ref[:, cs]
    v = act_ref[...]
    mu = jnp.mean(v, axis=-1, keepdims=True)
    var = jnp.mean(jnp.square(v - mu), axis=-1, keepdims=True)
    y = (v - mu) * lax.rsqrt(var + EPS) * lng_ref[...] + lnb_ref[...]
    y = _silu(y).astype(BF16)
    o_ref[...] = x_ref[...] + gm_ref[0] * jnp.dot(y, w_ref[...], preferred_element_type=F32)


def _conv_module(x2, seq, u, dw, dwb, lng, lnb, gm, w_bf):
    t = x2.shape[0]
    per = TM // CONV_HALO
    return pl.pallas_call(
        functools.partial(_conv_kernel, seq),
        grid=(t // TM,),
        in_specs=[_row_spec(D_MODEL), _row_spec(D_MODEL),
                  pl.BlockSpec((CONV_HALO, D_MODEL), lambda i: (jnp.maximum(i * per - 1, 0), 0)),
                  _full_spec((CONV_HALO, D_MODEL)), _full_spec((1, D_MODEL)), _full_spec((1, D_MODEL)),
                  _full_spec((1, D_MODEL)), _batch_spec(seq), _full_spec((D_MODEL, D_MODEL))],
        out_specs=_row_spec(D_MODEL),
        out_shape=jax.ShapeDtypeStruct((t, D_MODEL), F32),
        scratch_shapes=[pltpu.VMEM((TM + CONV_HALO, D_MODEL), F32),
                        pltpu.VMEM((SUBLANES - 1, CONV_SHIFT_ROWS, D_MODEL), F32),
                        pltpu.VMEM((TM, D_MODEL), F32)],
        compiler_params=_cparams(("arbitrary",)),
        name="conv_module",
    )(x2, u, u, dw, dwb, lng, lnb, gm, w_bf)


ROW_TILE = (SUBLANES, LANES)


def _store_tile_rows(ref, v):
    ref[...] = v.reshape((v.shape[0],) + ROW_TILE)


def _load_tile_rows(ref):
    return ref[...].reshape(ref.shape[0], D_MODEL)


def _first_index_of_max(v, lane):
    m = jnp.max(v, axis=-1, keepdims=True)
    idx = jnp.min(jnp.where(v == m, lane, float(LANES)), axis=-1, keepdims=True)
    return m, idx


def _router_kernel(x_ref, g_ref, sc_ref, sh_ref, wr_ref, h_ref, ids_ref, gates_ref, cnt_ref, carry_ref):
    i = pl.program_id(0)

    @pl.when(i == 0)
    def _():
        carry_ref[...] = jnp.zeros_like(carry_ref)

    h = _norm_mod(x_ref[...], g_ref[...], sc_ref[0], sh_ref[0])
    _store_tile_rows(h_ref, h)
    logits = _dot_bf16x3(h, wr_ref[...], (((1,), (0,)), ((), ())))
    lane_i = lax.broadcasted_iota(jnp.int32, logits.shape, 1)
    lane = lane_i.astype(F32)
    gl = jnp.where(lane < MOE_GROUPS, logits, -jnp.inf)
    gmax, gsel = _first_index_of_max(gl, lane)
    g_gate = 1.0 / jnp.sum(jnp.exp(gl - gmax), axis=-1, keepdims=True)
    lo = MOE_GROUPS + MOE_EPG * gsel
    in_grp = (lane >= lo) & (lane < lo + MOE_EPG)
    el = jnp.where(in_grp, logits, -jnp.inf)
    emax = jnp.max(el, axis=-1, keepdims=True)
    p = jnp.exp(el - emax)
    prob = p / jnp.sum(p, axis=-1, keepdims=True)
    cand = jnp.where(in_grp, prob, -1.0)
    v1, i1 = _first_index_of_max(cand, lane)
    cand2 = jnp.where(lane == i1, -1.0, cand)
    v2, i2 = _first_index_of_max(cand2, lane)
    den = v1 + v2
    w1 = g_gate * v1 / den
    w2 = g_gate * v2 / den
    e1 = i1 - MOE_GROUPS
    e2 = i2 - MOE_GROUPS
    oh1 = (lane == e1).astype(F32)
    oh2 = (lane == e2).astype(F32)
    rows = lax.broadcasted_iota(jnp.int32, (TM, TM), 0)
    cols = lax.broadcasted_iota(jnp.int32, (TM, TM), 1)
    before = (cols < rows).astype(BF16)
    prior = jnp.dot(before, (oh1 + oh2).astype(BF16), preferred_element_type=F32) + carry_ref[...]
    r1 = jnp.sum(oh1 * prior, axis=-1, keepdims=True)
    r2 = jnp.sum(oh2 * prior, axis=-1, keepdims=True)
    carry_ref[...] = carry_ref[...] + jnp.sum(oh1 + oh2, axis=0, keepdims=True)
    cnt_ref[...] = jnp.broadcast_to(carry_ref[...], cnt_ref.shape)
    packed = jnp.where(lane_i == 0, e1, jnp.where(lane_i == 1, e2, jnp.where(lane_i == 2, r1, jnp.where(lane_i == 3, r2, 0.0))))
    ids_ref[...] = packed.astype(jnp.int32)
    gates_ref[...] = jnp.where(lane_i == 0, w1, jnp.where(lane_i == 1, w2, 0.0))


def _router(x2, seq, g, sc, sh, wr):
    t = x2.shape[0]
    return pl.pallas_call(
        _router_kernel,
        grid=(t // TM,),
        in_specs=[_row_spec(D_MODEL), _full_spec((1, D_MODEL)), _batch_spec(seq), _batch_spec(seq),
                  _full_spec((D_MODEL, LANES))],
        out_specs=[pl.BlockSpec((TM,) + ROW_TILE, lambda i: (i, 0, 0)), _row_spec(LANES), _row_spec(LANES),
                   _full_spec((8, LANES))],
        out_shape=[jax.ShapeDtypeStruct((t,) + ROW_TILE, F32), jax.ShapeDtypeStruct((t, LANES), jnp.int32),
                   jax.ShapeDtypeStruct((t, LANES), F32), jax.ShapeDtypeStruct((8, LANES), F32)],
        scratch_shapes=[pltpu.VMEM((1, LANES), F32)],
        compiler_params=_cparams(("arbitrary",)),
        name="moe_router",
    )(x2, g, sc, sh, wr)


def _row_copy(src_ref, src_row, dst_ref, dst_row, sem):
    return pltpu.make_async_copy(src_ref.at[src_row], dst_ref.at[dst_row], sem)


def _dispatch_kernel(dest_ref, zfill_ref, h_ref, xs_ref, buf_ref, zero_ref, sem_ref, zsem_ref):
    i = pl.program_id(0)
    n = pl.num_programs(0)
    slot = i % 2

    def zero_copy(k):
        return pltpu.make_async_copy(zero_ref, xs_ref.at[pl.ds(zfill_ref[k], MOE_BM)], zsem_ref.at[0])

    @pl.when(i == 0)
    def _():
        zero_ref[...] = jnp.zeros_like(zero_ref)

        def start(k, c):
            @pl.when(zfill_ref[k] >= 0)
            def _():
                zero_copy(k).start()
            return c
        lax.fori_loop(0, 2 * MOE_EXPERTS, start, 0)

        def drain(k, c):
            @pl.when(zfill_ref[k] >= 0)
            def _():
                zero_copy(k).wait()
            return c
        lax.fori_loop(0, 2 * MOE_EXPERTS, drain, 0)

    def wait_slot(s):
        for _ in range(2):
            pltpu.make_async_copy(buf_ref.at[s], xs_ref.at[pl.ds(0, TM)], sem_ref.at[s]).wait()

    @pl.when(i >= 2)
    def _():
        wait_slot(slot)

    buf_ref[slot] = h_ref[...]

    n_tok = n * TM

    def issue(r, c):
        a = i * TM + r
        _row_copy(buf_ref.at[slot], r, xs_ref, dest_ref[a], sem_ref.at[slot]).start(priority=0)
        _row_copy(buf_ref.at[slot], r, xs_ref, dest_ref[n_tok + a], sem_ref.at[slot]).start(priority=1)
        return c
    lax.fori_loop(0, TM, issue, 0, unroll=8)

    @pl.when(i == n - 1)
    def _():
        wait_slot(slot)

        @pl.when(n >= 2)
        def _():
            wait_slot(1 - slot)


def _dispatch(dest, zfill, h, n_rows):
    t = h.shape[0]
    return pl.pallas_call(
        _dispatch_kernel,
        grid_spec=pltpu.PrefetchScalarGridSpec(
            num_scalar_prefetch=2,
            grid=(t // TM,),
            in_specs=[pl.BlockSpec((TM,) + ROW_TILE, lambda i, d, z: (i, 0, 0))],
            out_specs=pl.BlockSpec(memory_space=pl.ANY),
            scratch_shapes=[pltpu.VMEM((2, TM) + ROW_TILE, F32), pltpu.VMEM((MOE_BM,) + ROW_TILE, F32),
                            pltpu.SemaphoreType.DMA((2,)), pltpu.SemaphoreType.DMA((1,))],
        ),
        out_shape=jax.ShapeDtypeStruct((n_rows,) + ROW_TILE, F32),
        compiler_params=_cparams(("arbitrary",)),
        name="moe_dispatch",
    )(dest, zfill, h)


def _dest_kernel(ids_ref, pstart_ref, o_ref):
    ids = ids_ref[...]
    lane = lax.broadcasted_iota(jnp.int32, ids.shape, 1)
    pstart = pstart_ref[...]

    def row_of(k):
        base = jnp.sum(jnp.where(lane == ids[:, k:k + 1], pstart, 0.0), axis=-1, keepdims=True)
        return base + ids[:, 2 + k:3 + k].astype(F32)

    rows = jnp.where(lane == 0, row_of(0), jnp.where(lane == 1, row_of(1), 0.0))
    o_ref[...] = rows.T[:SUBLANES, :].astype(jnp.int32)


DEST_TM = 2048


def _dest_rows(ids, pstarts):
    t = ids.shape[0]
    pstart_row = jnp.zeros((1, LANES), F32).at[0, :MOE_EXPERTS].set(pstarts.astype(F32))
    tm = min(DEST_TM, t)
    out = pl.pallas_call(
        _dest_kernel,
        grid=(t // tm,),
        in_specs=[pl.BlockSpec((tm, LANES), lambda i: (i, 0)), _full_spec((1, LANES))],
        out_specs=pl.BlockSpec((SUBLANES, tm), lambda i: (0, i)),
        out_shape=jax.ShapeDtypeStruct((SUBLANES, t), jnp.int32),
        compiler_params=_cparams(("arbitrary",)),
        name="moe_dest",
    )(ids, pstart_row)
    return out[:2].reshape(2 * t)


def _expert_kernel(layer, be_ref, nb_ref, nxt_ref, xs_ref, w1_hbm, w3_hbm, w2_hbm, y_ref,
                   w1f_ref, w3f_ref, w2f_ref, w1b_ref, w3b_ref, w2b_ref, ord_ref, sem_ref):
    i = pl.program_id(0)
    used = i < nb_ref[0]
    prev = be_ref[jnp.maximum(i - 1, 0)]

    def fetch(e, s):
        return (pltpu.make_async_copy(w1_hbm.at[layer, e], w1f_ref.at[s], sem_ref.at[s]),
                pltpu.make_async_copy(w3_hbm.at[layer, e], w3f_ref.at[s], sem_ref.at[s]),
                pltpu.make_async_copy(w2_hbm.at[layer, e], w2f_ref.at[s], sem_ref.at[s]))

    @pl.when(i == 0)
    def _():
        ord_ref[0] = 0
        for c in fetch(be_ref[0], 0):
            c.start()

    @pl.when(used & ((i == 0) | (be_ref[i] != prev)))
    def _():
        s = ord_ref[0] % 2
        for c in fetch(be_ref[i], s):
            c.wait()
        w1b_ref[...] = w1f_ref[s].astype(BF16)
        w3b_ref[...] = w3f_ref[s].astype(BF16)
        w2b_ref[...] = w2f_ref[s].astype(BF16)

        @pl.when(nxt_ref[i] >= 0)
        def _():
            for c in fetch(nxt_ref[i], 1 - s):
                c.start()
        ord_ref[0] = ord_ref[0] + 1

    @pl.when(used)
    def _():
        xb = _load_tile_rows(xs_ref).astype(BF16)
        a = jnp.dot(xb, w1b_ref[...], preferred_element_type=F32)
        b = jnp.dot(xb, w3b_ref[...], preferred_element_type=F32)
        hmid = (_silu(a) * b).astype(BF16)
        _store_tile_rows(y_ref, jnp.dot(hmid, w2b_ref[...], preferred_element_type=F32))

    @pl.when(i >= nb_ref[0])
    def _():
        y_ref[...] = jnp.zeros_like(y_ref)


def _experts(blk_exp, n_used, blk_next, xs, layer, w1, w3, w2):
    n_rows = xs.shape[0]
    nb = n_rows // MOE_BM

    def row_map(i, be, nu, nx):
        return (jnp.minimum(i, nu[0] - 1), 0, 0)

    return pl.pallas_call(
        functools.partial(_expert_kernel, layer),
        grid_spec=pltpu.PrefetchScalarGridSpec(
            num_scalar_prefetch=3,
            grid=(nb,),
            in_specs=[pl.BlockSpec((MOE_BM,) + ROW_TILE, row_map),
                      pl.BlockSpec(memory_space=pl.ANY), pl.BlockSpec(memory_space=pl.ANY),
                      pl.BlockSpec(memory_space=pl.ANY)],
            out_specs=pl.BlockSpec((MOE_BM,) + ROW_TILE, lambda i, be, nu, nx: (i, 0, 0)),
            scratch_shapes=[pltpu.VMEM((2, D_MODEL, EXPERT_FF), F32), pltpu.VMEM((2, D_MODEL, EXPERT_FF), F32),
                            pltpu.VMEM((2, EXPERT_FF, D_MODEL), F32),
                            pltpu.VMEM((D_MODEL, EXPERT_FF), BF16), pltpu.VMEM((D_MODEL, EXPERT_FF), BF16),
                            pltpu.VMEM((EXPERT_FF, D_MODEL), BF16),
                            pltpu.SMEM((1,), jnp.int32), pltpu.SemaphoreType.DMA((2,))],
        ),
        out_shape=jax.ShapeDtypeStruct((n_rows,) + ROW_TILE, F32),
        compiler_params=_cparams(("arbitrary",)),
        name="moe_experts",
    )(blk_exp, n_used, blk_next, xs, w1, w3, w2)


def _combine_kernel(dest_ref, x_ref, gates_ref, gf_ref, y_ref, o_ref, buf_ref, sem_ref):
    i = pl.program_id(0)
    n = pl.num_programs(0)
    slot = i % 2

    n_tok = n * TM

    def issue(tile, s):
        def body(r, c):
            a = tile * TM + r
            _row_copy(y_ref, dest_ref[a], buf_ref.at[s, 0], r, sem_ref.at[s]).start(priority=0)
            _row_copy(y_ref, dest_ref[n_tok + a], buf_ref.at[s, 1], r, sem_ref.at[s]).start(priority=1)
            return c
        lax.fori_loop(0, TM, body, 0, unroll=8)

    @pl.when(i == 0)
    def _():
        issue(0, 0)

    @pl.when(i + 1 < n)
    def _():
        issue(i + 1, 1 - slot)

    for k in range(2):
        pltpu.make_async_copy(y_ref.at[pl.ds(0, TM)], buf_ref.at[slot, k], sem_ref.at[slot]).wait()

    w0 = gates_ref[:, 0:1]
    w1 = gates_ref[:, 1:2]
    mix = _load_tile_rows(buf_ref.at[slot, 0]) * w0 + _load_tile_rows(buf_ref.at[slot, 1]) * w1
    o_ref[...] = x_ref[...] + gf_ref[0] * mix


def _combine(dest, x2, seq, gates, gf, y):
    t = x2.shape[0]
    return pl.pallas_call(
        _combine_kernel,
        grid_spec=pltpu.PrefetchScalarGridSpec(
            num_scalar_prefetch=1,
            grid=(t // TM,),
            in_specs=[pl.BlockSpec((TM, D_MODEL), lambda i, d: (i, 0)),
                      pl.BlockSpec((TM, LANES), lambda i, d: (i, 0)),
                      pl.BlockSpec((1, 1, D_MODEL), lambda i, d: ((i * TM) // seq, 0, 0)),
                      pl.BlockSpec(memory_space=pl.ANY)],
            out_specs=pl.BlockSpec((TM, D_MODEL), lambda i, d: (i, 0)),
            scratch_shapes=[pltpu.VMEM((2, 2, TM) + ROW_TILE, F32), pltpu.SemaphoreType.DMA((2,))],
        ),
        out_shape=jax.ShapeDtypeStruct((t, D_MODEL), F32),
        compiler_params=_cparams(("arbitrary",)),
        name="moe_combine",
    )(dest, x2, gates, gf, y)


def _moe_layer(x2, seq, g, sc, sh, gf, wr, layer, w1, w3, w2):
    t = x2.shape[0]
    h, ids, gates, cnt = _router(x2, seq, g, sc, sh, wr)
    counts = cnt[0, :MOE_EXPERTS].astype(jnp.int32)
    padded = (counts + MOE_BM - 1) // MOE_BM * MOE_BM
    pends = jnp.cumsum(padded)
    pstarts = pends - padded
    dest = _dest_rows(ids, pstarts)
    n_rows = 2 * t + MOE_EXPERTS * MOE_BM
    nb = n_rows // MOE_BM
    blk_start = jnp.arange(nb, dtype=jnp.int32) * MOE_BM
    blk_exp = jnp.minimum(jnp.sum((pends[None, :] <= blk_start[:, None]).astype(jnp.int32), axis=1), MOE_EXPERTS - 1)
    n_used = (pends[-1:] // MOE_BM).astype(jnp.int32)
    tail = pends[-1] + jnp.arange(MOE_EXPERTS, dtype=jnp.int32) * MOE_BM
    zfill = jnp.concatenate([jnp.where(padded > 0, pends - MOE_BM, -1),
                             jnp.where(tail < n_rows, tail, -1)]).astype(jnp.int32)
    xs = _dispatch(dest, zfill, h, n_rows)
    eid = jnp.arange(MOE_EXPERTS, dtype=jnp.int32)
    later = (padded[None, :] > 0) & (eid[None, :] > eid[:, None])
    next_exp = jnp.min(jnp.where(later, eid[None, :], MOE_EXPERTS), axis=1)
    next_exp = jnp.where(next_exp < MOE_EXPERTS, next_exp, -1).astype(jnp.int32)
    y = _experts(blk_exp, n_used, next_exp[blk_exp], xs, layer, w1, w3, w2)
    return _combine(dest, x2, seq, gates, gf, y)


HG_LT = 256


def _sublane_roll(x, shift):
    return pltpu.roll(x, shift % x.shape[0], 0)


def _hgrn_chunk(hq, hf, hi, lb, state_t):
    c, sub = HG_CHUNK, HG_SUB
    q = _silu(hq)
    f = lb + (1.0 - lb) * _sigmoid(hf)
    lf = jnp.log(f)
    k = 1.0 - f
    v = hi
    row = lax.broadcasted_iota(jnp.int32, (c, c), 0)
    col = lax.broadcasted_iota(jnp.int32, (c, c), 1)
    start = (row // sub) * sub
    tri_local = ((col >= start) & (col <= row)).astype(F32)
    tri_before = (col < start).astype(F32)
    bloc = jnp.dot(tri_local, lf, precision=HIGHEST, preferred_element_type=F32)
    rref = jnp.dot(tri_before, lf, precision=HIGHEST, preferred_element_type=F32)
    b = rref + bloc
    qt = q * jnp.exp(bloc)
    ridx = lax.broadcasted_iota(jnp.int32, (c, HG_DK), 0)

    blocks = [jnp.zeros((sub, c), F32)]
    for i in range(1, c // sub):
        r_i = rref[i * sub:i * sub + 1, :]
        live = ridx < i * sub
        k_i = jnp.where(live, k * jnp.exp(jnp.where(live, r_i - b, 0.0)), 0.0)
        blocks.append(lax.dot_general(qt[i * sub:(i + 1) * sub].astype(BF16), k_i.astype(BF16),
                                      (((1,), (1,)), ((), ())), preferred_element_type=F32))
    attn_off = jnp.concatenate(blocks, axis=0)
    o = jnp.dot(attn_off.astype(BF16), v.astype(BF16), preferred_element_type=F32)

    pos = ridx % sub
    for d in range(sub):
        qd = q if d == 0 else _sublane_roll(q, -d)
        bd = bloc if d == 0 else _sublane_roll(bloc, -d)
        ok = pos + d < sub
        e = qd * k * jnp.exp(jnp.where(ok, bd - bloc, -jnp.inf))
        w = jnp.sum(e, axis=-1, keepdims=True) * v
        o = o + (w if d == 0 else _sublane_roll(w, d))

    qe = qt * jnp.exp(rref)
    o = o + lax.dot_general(qe.astype(BF16), state_t.astype(BF16), (((1,), (1,)), ((), ())),
                            preferred_element_type=F32)
    b_end = b[c - 1:c, :]
    kd = k * jnp.exp(b_end - b)
    new_state = state_t * jnp.exp(b_end) + lax.dot_general(v.astype(BF16), kd.astype(BF16), (((0,), (0,)), ((), ())),
                                                           preferred_element_type=F32)
    return o, new_state


def _hgrn_kernel(hq_ref, hf_ref, hi_ref, hg_ref, lb_ref, og_ref, o_ref, state_ref):
    @pl.when(pl.program_id(0) == 0)
    def _():
        state_ref[...] = jnp.zeros_like(state_ref)

    og = og_ref[...]

    def body(ci, carry):
        rows = pl.ds(pl.multiple_of(ci * HG_CHUNK, HG_CHUNK), HG_CHUNK)
        for b in range(hq_ref.shape[0]):
            for h in range(HG_HEADS):
                cs = slice(h * HG_DK, (h + 1) * HG_DK)
                o, st = _hgrn_chunk(hq_ref[b, rows, cs], hf_ref[b, rows, cs], hi_ref[b, rows, cs], lb_ref[:, cs],
                                    state_ref[b, h])
                state_ref[b, h] = st
                on = o * lax.rsqrt(jnp.mean(o * o, axis=-1, keepdims=True) + EPS) * og
                o_ref[b, rows, cs] = on * _silu(hg_ref[b, rows, cs])
        return carry
    lax.fori_loop(0, HG_LT // HG_CHUNK, body, 0)


def _hgrn(z, batch, seq, lb, o_gain):
    t = z.shape[0]
    z3 = z.reshape(batch, seq, z.shape[1])

    def col(base):
        return pl.BlockSpec((batch, HG_LT, HG_W), lambda l: (0, l, base // HG_W))

    out = pl.pallas_call(
        _hgrn_kernel,
        grid=(seq // HG_LT,),
        in_specs=[col(Z_HQ), col(Z_HF), col(Z_HI), col(Z_HG),
                  pl.BlockSpec((1, HG_W), lambda l: (0, 0)),
                  pl.BlockSpec((1, HG_DV), lambda l: (0, 0))],
        out_specs=pl.BlockSpec((batch, HG_LT, HG_W), lambda l: (0, l, 0)),
        out_shape=jax.ShapeDtypeStruct((batch, seq, HG_W), F32),
        scratch_shapes=[pltpu.VMEM((batch, HG_HEADS, HG_DV, HG_DK), F32)],
        compiler_params=_cparams(("arbitrary",)),
        name="hgrn2",
    )(z3, z3, z3, z3, lb.reshape(1, -1), o_gain.reshape(1, -1))
    return out.reshape(t, HG_W)


def _rms64(v, gain):
    return v * lax.rsqrt(jnp.mean(v * v, axis=-1, keepdims=True) + EPS) * gain


def _nsa_prep_kernel(kc_ref, vc_ref, ks_ref, vs_ref, kw_ref, vw_ref, kg_ref,
                     kcr_ref, vcr_ref, ksn_ref, vsb_ref, kwn_ref, vwb_ref):
    ones_col = (lax.broadcasted_iota(jnp.int32, (TM, NSA_VW - NSA_HD), 1) == 0).astype(F32)

    def with_ones(v):
        return jnp.concatenate([v, ones_col], axis=1).astype(BF16)

    for g in range(NSA_KV_GROUPS):
        cs = slice(g * NSA_HD, (g + 1) * NSA_HD)
        kcr_ref[0, g] = kc_ref[:, cs]
        vcr_ref[0, g] = vc_ref[:, cs]
        ksn_ref[0, g] = _rms64(ks_ref[:, cs], kg_ref[1:2, :]).astype(BF16)
        vsb_ref[0, g] = with_ones(vs_ref[:, cs])
        kwn_ref[0, g] = _rms64(kw_ref[:, cs], kg_ref[2:3, :]).astype(BF16)
        vwb_ref[0, g] = with_ones(vw_ref[:, cs])


def _nsa_prep(z, batch, seq, k_gain):
    per = seq // TM

    def col(base):
        return pl.BlockSpec((TM, NSA_KV_W), lambda i: (i, base // NSA_KV_W))

    def out(dtype, width=NSA_HD):
        return (pl.BlockSpec((1, NSA_KV_GROUPS, TM, width), lambda i: (i // per, 0, i % per, 0)),
                jax.ShapeDtypeStruct((batch, NSA_KV_GROUPS, seq, width), dtype))

    outs = [out(F32), out(F32), out(BF16), out(BF16, NSA_VW), out(BF16), out(BF16, NSA_VW)]
    return pl.pallas_call(
        _nsa_prep_kernel,
        grid=(batch * per,),
        in_specs=[col(Z_KC), col(Z_VC), col(Z_KS), col(Z_VS), col(Z_KW), col(Z_VW), _full_spec((3, NSA_HD))],
        out_specs=[o[0] for o in outs],
        out_shape=[o[1] for o in outs],
        compiler_params=_cparams(("arbitrary",)),
        name="nsa_prep",
    )(z, z, z, z, z, z, k_gain)


def _nsa_compress_kernel(ak_ref, av_ref, wk_ref, wv_ref, pe_ref, kg_ref, kc_ref, vc_ref):
    half = CMP_STRIDE * NSA_HD

    def compress(a, w_ref):
        top = jnp.dot(a, w_ref[:half, :], precision=HIGHEST, preferred_element_type=F32)
        bot = jnp.dot(a, w_ref[half:, :], precision=HIGHEST, preferred_element_type=F32)
        pe = jnp.dot(pe_ref[...], w_ref[...], precision=HIGHEST, preferred_element_type=F32)
        return top + _sublane_roll(bot, -1) + pe[0:1]

    kc_ref[0, 0] = _rms64(compress(ak_ref[0, 0], wk_ref), kg_ref[0:1, :])
    vc_ref[0, 0] = compress(av_ref[0, 0], wv_ref)


def _nsa_compress(kcr, vcr, w_ck, w_cv, cmp_pe, k_gain):
    batch, groups, seq, _ = kcr.shape
    nc = seq // CMP_STRIDE
    wide = CMP_STRIDE * NSA_HD
    spec = pl.BlockSpec((1, 1, nc, wide), lambda b, g: (b, g, 0, 0))
    ospec = pl.BlockSpec((1, 1, nc, NSA_HD), lambda b, g: (b, g, 0, 0))
    oshape = jax.ShapeDtypeStruct((batch, groups, nc, NSA_HD), F32)
    full2 = lambda shape: pl.BlockSpec(shape, lambda b, g: (0, 0))
    return pl.pallas_call(
        _nsa_compress_kernel,
        grid=(batch, groups),
        in_specs=[spec, spec, full2((2 * wide, NSA_HD)), full2((2 * wide, NSA_HD)), full2((8, 2 * wide)),
                  full2((3, NSA_HD))],
        out_specs=[ospec, ospec],
        out_shape=[oshape, oshape],
        compiler_params=_cparams(("arbitrary", "arbitrary")),
        name="nsa_compress",
    )(kcr.reshape(batch, groups, nc, wide), vcr.reshape(batch, groups, nc, wide), w_ck, w_cv,
      jnp.broadcast_to(cmp_pe.reshape(1, 2 * wide), (8, 2 * wide)), k_gain)


def _nsa_kernel(seq, kchunk, q_ref, gate_ref, kc_ref, vc_ref, ks_ref, vs_ref, kw_ref, vw_ref, qg_ref, agg_ref,
                o_ref):
    qi = pl.program_id(2)
    t0 = qi * Q_BLOCK
    jn = NSA_QPG
    nc = seq // CMP_STRIDE
    n_slc = seq // SLC_BLOCK
    n_sel = min(N_SELECT, n_slc)
    nt = (((1,), (1,)), ((), ()))

    qraw = q_ref[...]
    qs = [_rms64(qraw[:, j * NSA_HD:(j + 1) * NSA_HD], qg_ref[...]) * (NSA_HD ** -0.5) for j in range(jn)]
    qb = [q.astype(BF16) for q in qs]

    def pos_of(shape):
        return t0 + lax.broadcasted_iota(jnp.int32, shape, 0)

    cmp_end = lax.broadcasted_iota(jnp.int32, (Q_BLOCK, nc), 1) * CMP_STRIDE + (CMP_BLOCK - 1)
    valid_c = cmp_end <= pos_of((Q_BLOCK, nc))
    sees_any = pos_of((Q_BLOCK, 1)) >= CMP_BLOCK - 1
    kc_hi, kc_lo = _split_bf16(kc_ref[0, 0], 2)
    vcb = vc_ref[0, 0].astype(BF16)
    o_c = []
    imp = None
    for j in range(jn):
        qh, ql = _split_bf16(qs[j], 2)
        s = (lax.dot_general(qh, kc_hi, nt, preferred_element_type=F32)
             + (lax.dot_general(qh, kc_lo, nt, preferred_element_type=F32)
                + lax.dot_general(ql, kc_hi, nt, preferred_element_type=F32)))
        sm = jnp.where(valid_c, s, NEG)
        e = jnp.exp(sm - jnp.max(sm, axis=-1, keepdims=True))
        p = e * jnp.where(sees_any, 1.0 / jnp.sum(e, axis=-1, keepdims=True), 0.0)
        o_c.append(jnp.dot(p.astype(BF16), vcb, preferred_element_type=F32))
        imp = p if imp is None else imp + p

    agg = agg_ref[...].astype(BF16)
    imp_s = sum(jnp.dot(part, agg, preferred_element_type=F32) for part in _split_bf16(imp, 3))
    ids = lax.broadcasted_iota(jnp.int32, (Q_BLOCK, n_slc), 1)
    q_blk = pos_of((Q_BLOCK, n_slc)) // SLC_BLOCK
    forced = (ids == 0) | (ids == q_blk) | (ids == q_blk - 1)
    score_t = jnp.where(ids > q_blk, NEG, jnp.where(forced, FORCE, imp_s)).T
    ids_t = lax.broadcasted_iota(jnp.int32, (n_slc, Q_BLOCK), 0).astype(F32)
    cur = score_t
    sel_t = jnp.zeros((n_slc, Q_BLOCK), F32)
    for _ in range(n_sel):
        m = jnp.max(cur, axis=0, keepdims=True)
        first = jnp.min(jnp.where(cur == m, ids_t, float(n_slc)), axis=0, keepdims=True)
        hit = ids_t == first
        sel_t = jnp.where(hit, 1.0, sel_t)
        cur = jnp.where(hit, -jnp.inf, cur)
    sel = jnp.where(score_t > NEG / 2, sel_t, 0.0).T.astype(BF16)

    span = Q_BLOCK + WINDOW
    start = pl.multiple_of(jnp.maximum(t0 - WINDOW, 0), Q_BLOCK)
    wsl = pl.ds(start, span)
    kpos = start + lax.broadcasted_iota(jnp.int32, (Q_BLOCK, span), 1)
    wpos = pos_of((Q_BLOCK, span))
    bias_w = jnp.where((kpos <= wpos) & (kpos > wpos - WINDOW), 0.0, NEG)
    kw = kw_ref[0, 0, wsl, :]
    vw = vw_ref[0, 0, wsl, :]
    o_w = []
    for j in range(jn):
        s = lax.dot_general(qb[j], kw, nt, preferred_element_type=F32) + bias_w
        e = jnp.exp((s - jnp.max(s, axis=-1, keepdims=True)).astype(BF16))
        ow = jnp.dot(e, vw, preferred_element_type=F32)
        o_w.append(ow[:, :NSA_HD] / ow[:, NSA_HD:NSA_HD + 1])

    per_chunk = kchunk // SLC_BLOCK
    blk_of_key = lax.broadcasted_iota(jnp.int32, (n_slc, kchunk), 1) // SLC_BLOCK
    blk_row = lax.broadcasted_iota(jnp.int32, (n_slc, kchunk), 0)
    key_lane = lax.broadcasted_iota(jnp.int32, (Q_BLOCK, kchunk), 1)
    qpos = pos_of((Q_BLOCK, kchunk))

    def sel_body(c, carry):
        ksl = pl.ds(pl.multiple_of(c * kchunk, kchunk), kchunk)
        k_c = ks_ref[0, 0, ksl, :]
        v_c = vs_ref[0, 0, ksl, :]
        expand = (blk_row == blk_of_key + c * per_chunk).astype(BF16)
        chosen = jnp.dot(sel, expand, preferred_element_type=F32)
        bias = jnp.where((chosen > 0.5) & (key_lane + c * kchunk <= qpos), 0.0, NEG)
        m_run, acc = carry
        s = lax.dot_general(qb_all, k_c, nt, preferred_element_type=F32) + jnp.concatenate([bias] * jn, axis=0)
        m_new = jnp.maximum(m_run, jnp.max(s, axis=-1, keepdims=True))
        alpha = jnp.exp(m_run - m_new)
        p = jnp.exp((s - m_new).astype(BF16))
        acc_new = acc * alpha + jnp.dot(p, v_c, preferred_element_type=F32)
        return m_new, acc_new

    rows = jn * Q_BLOCK
    qb_all = jnp.concatenate(qb, axis=0)
    n_chunks = (t0 + Q_BLOCK + kchunk - 1) // kchunk
    init = (jnp.full((rows, 1), NEG, F32), jnp.zeros((rows, NSA_VW), F32))
    _, acc_fin = lax.fori_loop(0, n_chunks, sel_body, init)
    o_sel = acc_fin[:, :NSA_HD] / acc_fin[:, NSA_HD:NSA_HD + 1]

    gates = _sigmoid(gate_ref[...])
    outs = []
    for j in range(jn):
        o_s = o_sel[j * Q_BLOCK:(j + 1) * Q_BLOCK]
        outs.append(gates[:, 3 * j:3 * j + 1] * o_c[j] + gates[:, 3 * j + 1:3 * j + 2] * o_s
                    + gates[:, 3 * j + 2:3 * j + 3] * o_w[j])
    o_ref[...] = jnp.concatenate(outs, axis=1)


def _selection_weights(seq):
    nc = seq // CMP_STRIDE
    n_slc = seq // SLC_BLOCK
    ratio = SLC_BLOCK // CMP_STRIDE
    n = jnp.arange(nc)[:, None]
    j = jnp.arange(n_slc)[None, :]
    o = n - ratio * j + (CMP_BLOCK // CMP_STRIDE - 1)
    w = jnp.asarray(SEL_AGG_W, F32)
    return jnp.where((o >= 0) & (o < len(SEL_AGG_W)), w[jnp.clip(o, 0, len(SEL_AGG_W) - 1)], 0.0)


def _nsa(z, batch, seq, w_ck, w_cv, cmp_pe, q_gain, k_gain):
    t = z.shape[0]
    kcr, vcr, ksn, vsb, kwn, vwb = _nsa_prep(z, batch, seq, k_gain)
    kc, vc = _nsa_compress(kcr, vcr, w_ck, w_cv, cmp_pe, k_gain)
    nq = seq // Q_BLOCK
    nc = seq // CMP_STRIDE
    n_slc = seq // SLC_BLOCK
    kchunk = min(NSA_KCHUNK, seq)
    qw = NSA_QPG * NSA_HD
    cmp_spec = pl.BlockSpec((1, 1, nc, NSA_HD), lambda b, g, i: (b, g, 0, 0))
    key_spec = pl.BlockSpec((1, 1, seq, NSA_HD), lambda b, g, i: (b, g, 0, 0))
    val_spec = pl.BlockSpec((1, 1, seq, NSA_VW), lambda b, g, i: (b, g, 0, 0))
    return pl.pallas_call(
        functools.partial(_nsa_kernel, seq, kchunk),
        grid=(batch, NSA_KV_GROUPS, nq),
        in_specs=[pl.BlockSpec((Q_BLOCK, qw), lambda b, g, i: (b * nq + i, g)),
                  pl.BlockSpec((Q_BLOCK, LANES), lambda b, g, i: (b * nq + i, Z_GATE // LANES + g)),
                  cmp_spec, cmp_spec, key_spec, val_spec, key_spec, val_spec,
                  pl.BlockSpec((1, NSA_HD), lambda b, g, i: (0, 0)),
                  pl.BlockSpec((nc, n_slc), lambda b, g, i: (0, 0))],
        out_specs=pl.BlockSpec((Q_BLOCK, qw), lambda b, g, i: (b * nq + i, g)),
        out_shape=jax.ShapeDtypeStruct((t, NSA_W), F32),
        compiler_params=_cparams(("arbitrary", "arbitrary", "arbitrary")),
        name="nsa_attention",
    )(z, z, kc, vc, ksn, vsb, kwn, vwb, q_gain.reshape(1, -1), _selection_weights(seq))


def _pad_in_proj(w_in):
    per_group = NSA_QPG * 3
    gate_lo = Z_GATE
    out = jnp.zeros((D_MODEL, Z_N), F32)
    out = out.at[:, :gate_lo].set(w_in[:, :gate_lo])
    for g in range(NSA_KV_GROUPS):
        out = out.at[:, gate_lo + g * LANES:gate_lo + g * LANES + per_group].set(
            w_in[:, gate_lo + g * per_group:gate_lo + (g + 1) * per_group])
    return out.at[:, Z_HQ:].set(w_in[:, gate_lo + NSA_HEADS * 3:])


def _even_mixer(x2, batch, seq, g, sc, sh, gm, w_in, w_out, w_ck, w_cv, cmp_pe, q_gain, k_gain, lb, o_gain):
    z = _in_proj(x2, seq, g, sc, sh, _pad_in_proj(w_in).astype(BF16))
    r = _hgrn(z, batch, seq, lb, o_gain)
    a = _nsa(z, batch, seq, w_ck, w_cv, cmp_pe, q_gain, k_gain)
    return _out_proj(x2, seq, a, r, gm, w_out.astype(BF16))


def _odd_mixer(x2, seq, g, sc, sh, gm, w_pw1, dw, dw_b, ln_g, ln_b, w_pw2):
    u = _glu_proj(x2, seq, g, sc, sh, w_pw1.astype(BF16))
    dw_pad = jnp.zeros((CONV_HALO, D_MODEL), F32).at[:CONV_W].set(dw)
    return _conv_module(x2, seq, u, dw_pad, dw_b.reshape(1, -1), ln_g.reshape(1, -1), ln_b.reshape(1, -1), gm,
                        w_pw2.astype(BF16))


def _router_weights(w_group, w_expert):
    wr = jnp.zeros((D_MODEL, LANES), F32)
    return wr.at[:, :MOE_GROUPS].set(w_group).at[:, MOE_GROUPS:MOE_GROUPS + MOE_EXPERTS].set(w_expert)


def _lower_bounds_kernel(l_ref, o_ref):
    logits = l_ref[...]
    e = jnp.exp(logits - jnp.max(logits, axis=0, keepdims=True))
    p = e / jnp.sum(e, axis=0, keepdims=True)
    n = logits.shape[0]
    acc = jnp.zeros_like(p[0:1])
    for i in range(n):
        acc = acc + p[i:i + 1]
        o_ref[i:i + 1, :] = acc - p[0:1]


def _lower_bounds(lb_logits):
    return pl.pallas_call(
        _lower_bounds_kernel,
        out_shape=jax.ShapeDtypeStruct(lb_logits.shape, F32),
        name="hgrn_lower_bounds",
    )(lb_logits)


def kernel(x, c, ada_w, ada_b, norm_mix, norm_ffn, mix_w_in, mix_w_out, nsa_cmp_wk, nsa_cmp_wv, nsa_cmp_pe, nsa_q_gain, nsa_k_gain, hgrn_lb_logits, hgrn_o_gain, conv_w_pw1, conv_dw, conv_dw_b, conv_ln_g, conv_ln_b, conv_w_pw2, moe_w_group, moe_w_expert, moe_w1, moe_w3, moe_w2):
    batch, seq, d = x.shape
    x2 = x.reshape(batch * seq, d)
    mod = _modulation(c, ada_w, ada_b)
    lower_bounds = _lower_bounds(hgrn_lb_logits)
    for layer in range(DEPTH):
        sh_m, sc_m, g_m, sh_f, sc_f, g_f = (mod[layer, :, k] for k in range(6))
        i = layer // 2
        gain = norm_mix[layer].reshape(1, d)
        if layer % 2 == 0:
            x2 = _even_mixer(x2, batch, seq, gain, sc_m, sh_m, g_m, mix_w_in[i], mix_w_out[i], nsa_cmp_wk[i],
                             nsa_cmp_wv[i], nsa_cmp_pe[i], nsa_q_gain[i], nsa_k_gain[i], lower_bounds[i],
                             hgrn_o_gain[i])
        else:
            x2 = _odd_mixer(x2, seq, gain, sc_m, sh_m, g_m, conv_w_pw1[i], conv_dw[i], conv_dw_b[i], conv_ln_g[i],
                            conv_ln_b[i], conv_w_pw2[i])
        x2 = _moe_layer(x2, seq, norm_ffn[layer].reshape(1, d), sc_f, sh_f, g_f,
                        _router_weights(moe_w_group[layer], moe_w_expert[layer]), layer, moe_w1, moe_w3, moe_w2)
    return x2.reshape(batch, seq, d)
```

```python
import functools

import jax
import jax.numpy as jnp
from jax import lax
from jax.experimental import pallas as pl
from jax.experimental.pallas import tpu as pltpu

F32 = jnp.float32
BF16 = jnp.bfloat16
HIGHEST = lax.Precision.HIGHEST

D_MODEL = 1024
DEPTH = 4

NSA_HEADS = 8
NSA_KV_GROUPS = 2
NSA_QPG = NSA_HEADS // NSA_KV_GROUPS
NSA_HD = 64
NSA_W = NSA_HEADS * NSA_HD
NSA_KV_W = NSA_KV_GROUPS * NSA_HD
NSA_KCHUNK = 512
NSA_VW = 128
CMP_BLOCK = 32
CMP_STRIDE = 16
SLC_BLOCK = 64
N_SELECT = 16
WINDOW = 512
Q_BLOCK = 256
SEL_AGG_W = (1.0, 2.0, 2.0, 2.0, 1.0)

HG_HEADS = 4
HG_DK = 128
HG_DV = 128
HG_W = HG_HEADS * HG_DV
HG_CHUNK = 64
HG_SUB = 8

CONV_W = 31
MOE_GROUPS = 4
MOE_EPG = 8
MOE_EXPERTS = MOE_GROUPS * MOE_EPG
EXPERT_FF = 512

EPS = 1e-6
NEG = -1e30
FORCE = 1e4

LANES = 128
VMEM_LIMIT = 56 * 1024 * 1024

Z_Q = 0
Z_KC = 512
Z_VC = 640
Z_KS = 768
Z_VS = 896
Z_KW = 1024
Z_VW = 1152
Z_GATE = 1280
Z_HQ = Z_GATE + NSA_KV_GROUPS * LANES
Z_HF = Z_HQ + HG_W
Z_HI = Z_HF + HG_W
Z_HG = Z_HI + HG_W
Z_N = Z_HG + HG_W

TM = 256
MOE_BM = 256


def _cparams(sem):
    return pltpu.CompilerParams(dimension_semantics=sem, vmem_limit_bytes=VMEM_LIMIT)


def _split_bf16(x, terms):
    parts = []
    for _ in range(terms):
        p = x.astype(BF16)
        parts.append(p)
        x = x - p.astype(F32)
    return parts


def _dot_bf16x3(a, b, dims):
    ah, al = _split_bf16(a, 2)
    bh, bl = _split_bf16(b, 2)

    def d(x, y):
        return lax.dot_general(x, y, dims, preferred_element_type=F32)
    return d(ah, bh) + (d(ah, bl) + d(al, bh))


def _sigmoid(x):
    return 1.0 / (1.0 + jnp.exp(-x))


def _silu(x):
    return x * _sigmoid(x)


def _mod_kernel(c_ref, w_ref, b_ref, o_ref):
    c = c_ref[...]
    o_ref[0] = jnp.dot(_silu(c), w_ref[0], precision=HIGHEST, preferred_element_type=F32) + b_ref[0]


def _modulation(c, ada_w, ada_b):
    b = c.shape[0]
    rows = 8
    c_pad = jnp.zeros((rows, D_MODEL), F32).at[:b].set(c)
    n = 6 * D_MODEL
    tn = 1536
    out = pl.pallas_call(
        _mod_kernel,
        grid=(DEPTH, n // tn),
        in_specs=[
            pl.BlockSpec((rows, D_MODEL), lambda l, j: (0, 0)),
            pl.BlockSpec((1, D_MODEL, tn), lambda l, j: (l, 0, j)),
            pl.BlockSpec((1, 1, tn), lambda l, j: (l, 0, j)),
        ],
        out_specs=pl.BlockSpec((1, rows, tn), lambda l, j: (l, 0, j)),
        out_shape=jax.ShapeDtypeStruct((DEPTH, rows, n), F32),
        compiler_params=_cparams(("arbitrary", "arbitrary")),
        name="adaln_mod",
    )(c_pad, ada_w, ada_b.reshape(DEPTH, 1, n))
    return out[:, :b].reshape(DEPTH, b, 6, 1, D_MODEL)


def _norm_mod(x, g, sc, sh):
    y = x * lax.rsqrt(jnp.mean(x * x, axis=-1, keepdims=True) + EPS)
    return (y * g) * (1.0 + sc) + sh


PROJ_TM = 512


def _row_spec(width, tm=TM):
    return pl.BlockSpec((tm, width), lambda i: (i, 0))


def _full_spec(shape):
    nd = len(shape)
    return pl.BlockSpec(shape, lambda i: (0,) * nd)


def _batch_spec(seq, tm=TM):
    return pl.BlockSpec((1, 1, D_MODEL), lambda i: ((i * tm) // seq, 0, 0))


def _in_proj_kernel(x_ref, g_ref, sc_ref, sh_ref, w_ref, o_ref):
    h = _norm_mod(x_ref[...], g_ref[...], sc_ref[0], sh_ref[0]).astype(BF16)
    tn = Z_N // 4
    for j in range(4):
        o_ref[:, j * tn:(j + 1) * tn] = jnp.dot(h, w_ref[:, j * tn:(j + 1) * tn], preferred_element_type=F32)


def _in_proj(x2, seq, g, sc, sh, w_bf):
    t = x2.shape[0]
    return pl.pallas_call(
        _in_proj_kernel,
        grid=(t // PROJ_TM,),
        in_specs=[_row_spec(D_MODEL, PROJ_TM), _full_spec((1, D_MODEL)), _batch_spec(seq, PROJ_TM),
                  _batch_spec(seq, PROJ_TM), _full_spec((D_MODEL, Z_N))],
        out_specs=_row_spec(Z_N, PROJ_TM),
        out_shape=jax.ShapeDtypeStruct((t, Z_N), F32),
        compiler_params=_cparams(("arbitrary",)),
        name="in_proj",
    )(x2, g, sc, sh, w_bf)


def _out_proj_kernel(x_ref, a_ref, r_ref, gm_ref, w_ref, o_ref):
    y = jnp.dot(a_ref[...].astype(BF16), w_ref[:NSA_W, :], preferred_element_type=F32)
    y = y + jnp.dot(r_ref[...].astype(BF16), w_ref[NSA_W:, :], preferred_element_type=F32)
    o_ref[...] = x_ref[...] + gm_ref[0] * y


def _out_proj(x2, seq, a, r, gm, w_bf):
    t = x2.shape[0]
    return pl.pallas_call(
        _out_proj_kernel,
        grid=(t // PROJ_TM,),
        in_specs=[_row_spec(D_MODEL, PROJ_TM), _row_spec(NSA_W, PROJ_TM), _row_spec(HG_W, PROJ_TM),
                  _batch_spec(seq, PROJ_TM), _full_spec((NSA_W + HG_W, D_MODEL))],
        out_specs=_row_spec(D_MODEL, PROJ_TM),
        out_shape=jax.ShapeDtypeStruct((t, D_MODEL), F32),
        compiler_params=_cparams(("arbitrary",)),
        name="out_proj",
    )(x2, a, r, gm, w_bf)


def _glu_kernel(x_ref, g_ref, sc_ref, sh_ref, w_ref, o_ref):
    h = _norm_mod(x_ref[...], g_ref[...], sc_ref[0], sh_ref[0]).astype(BF16)
    tn = 512
    for j in range(D_MODEL // tn):
        a = jnp.dot(h, w_ref[:, j * tn:(j + 1) * tn], preferred_element_type=F32)
        gate = jnp.dot(h, w_ref[:, D_MODEL + j * tn:D_MODEL + (j + 1) * tn], preferred_element_type=F32)
        o_ref[:, j * tn:(j + 1) * tn] = a * _sigmoid(gate)


def _glu_proj(x2, seq, g, sc, sh, w_bf):
    t = x2.shape[0]
    return pl.pallas_call(
        _glu_kernel,
        grid=(t // PROJ_TM,),
        in_specs=[_row_spec(D_MODEL, PROJ_TM), _full_spec((1, D_MODEL)), _batch_spec(seq, PROJ_TM),
                  _batch_spec(seq, PROJ_TM), _full_spec((D_MODEL, 2 * D_MODEL))],
        out_specs=_row_spec(D_MODEL, PROJ_TM),
        out_shape=jax.ShapeDtypeStruct((t, D_MODEL), F32),
        compiler_params=_cparams(("arbitrary",)),
        name="conv_glu",
    )(x2, g, sc, sh, w_bf)


CONV_HALO = 32
CONV_RB = 64
SUBLANES = 8
CONV_SHIFT_ROWS = TM + CONV_HALO - SUBLANES


def _conv_kernel(seq, x_ref, u_ref, halo_ref, dw_ref, dwb_ref, lng_ref, lnb_ref, gm_ref, w_ref, o_ref,
                 ext_ref, shift_ref, act_ref):
    i = pl.program_id(0)
    first = (i * TM) % seq == 0
    ext_ref[:CONV_HALO, :] = jnp.where(first, 0.0, halo_ref[...])
    ext_ref[CONV_HALO:, :] = u_ref[...]
    for s in range(1, SUBLANES):
        shift_ref[s - 1] = ext_ref[s:s + CONV_SHIFT_ROWS, :]
    off = CONV_HALO - (CONV_W - 1)
    for r in range(TM // CONV_RB):
        for c in range(D_MODEL // LANES):
            cs = slice(c * LANES, (c + 1) * LANES)
            acc = jnp.zeros((CONV_RB, LANES), F32)
            for k in range(CONV_W):
                s = (off + k) % SUBLANES
                lo = r * CONV_RB + off + k - s
                win = ext_ref[lo:lo + CONV_RB, cs] if s == 0 else shift_ref[s - 1, lo:lo + CONV_RB, cs]
                acc = acc + win * dw_ref[k:k + 1, cs]
            act_ref[r * CONV_RB:(r + 1) * CONV_RB, cs] = acc + dwb_ref[:, cs]
    v = act_ref[...]
    mu = jnp.mean(v, axis=-1, keepdims=True)
    var = jnp.mean(jnp.square(v - mu), axis=-1, keepdims=True)
    y = (v - mu) * lax.rsqrt(var + EPS) * lng_ref[...] + lnb_ref[...]
    y = _silu(y).astype(BF16)
    o_ref[...] = x_ref[...] + gm_ref[0] * jnp.dot(y, w_ref[...], preferred_element_type=F32)


def _conv_module(x2, seq, u, dw, dwb, lng, lnb, gm, w_bf):
    t = x2.shape[0]
    per = TM // CONV_HALO
    return pl.pallas_call(
        functools.partial(_conv_kernel, seq),
        grid=(t // TM,),
        in_specs=[_row_spec(D_MODEL), _row_spec(D_MODEL),
                  pl.BlockSpec((CONV_HALO, D_MODEL), lambda i: (jnp.maximum(i * per - 1, 0), 0)),
                  _full_spec((CONV_HALO, D_MODEL)), _full_spec((1, D_MODEL)), _full_spec((1, D_MODEL)),
                  _full_spec((1, D_MODEL)), _batch_spec(seq), _full_spec((D_MODEL, D_MODEL))],
        out_specs=_row_spec(D_MODEL),
        out_shape=jax.ShapeDtypeStruct((t, D_MODEL), F32),
        scratch_shapes=[pltpu.VMEM((TM + CONV_HALO, D_MODEL), F32),
                        pltpu.VMEM((SUBLANES - 1, CONV_SHIFT_ROWS, D_MODEL), F32),
                        pltpu.VMEM((TM, D_MODEL), F32)],
        compiler_params=_cparams(("arbitrary",)),
        name="conv_module",
    )(x2, u, u, dw, dwb, lng, lnb, gm, w_bf)


ROW_TILE = (SUBLANES, LANES)


def _store_tile_rows(ref, v):
    ref[...] = v.reshape((v.shape[0],) + ROW_TILE)


def _load_tile_rows(ref):
    return ref[...].reshape(ref.shape[0], D_MODEL)


def _first_index_of_max(v, lane):
    m = jnp.max(v, axis=-1, keepdims=True)
    idx = jnp.min(jnp.where(v == m, lane, float(LANES)), axis=-1, keepdims=True)
    return m, idx


def _router_kernel(x_ref, g_ref, sc_ref, sh_ref, wr_ref, h_ref, ids_ref, gates_ref, cnt_ref, carry_ref):
    i = pl.program_id(0)

    @pl.when(i == 0)
    def _():
        carry_ref[...] = jnp.zeros_like(carry_ref)

    h = _norm_mod(x_ref[...], g_ref[...], sc_ref[0], sh_ref[0])
    _store_tile_rows(h_ref, h)
    logits = _dot_bf16x3(h, wr_ref[...], (((1,), (0,)), ((), ())))
    lane_i = lax.broadcasted_iota(jnp.int32, logits.shape, 1)
    lane = lane_i.astype(F32)
    gl = jnp.where(lane < MOE_GROUPS, logits, -jnp.inf)
    gmax, gsel = _first_index_of_max(gl, lane)
    g_gate = 1.0 / jnp.sum(jnp.exp(gl - gmax), axis=-1, keepdims=True)
    lo = MOE_GROUPS + MOE_EPG * gsel
    in_grp = (lane >= lo) & (lane < lo + MOE_EPG)
    el = jnp.where(in_grp, logits, -jnp.inf)
    emax = jnp.max(el, axis=-1, keepdims=True)
    p = jnp.exp(el - emax)
    prob = p / jnp.sum(p, axis=-1, keepdims=True)
    cand = jnp.where(in_grp, prob, -1.0)
    v1, i1 = _first_index_of_max(cand, lane)
    cand2 = jnp.where(lane == i1, -1.0, cand)
    v2, i2 = _first_index_of_max(cand2, lane)
    den = v1 + v2
    w1 = g_gate * v1 / den
    w2 = g_gate * v2 / den
    e1 = i1 - MOE_GROUPS
    e2 = i2 - MOE_GROUPS
    oh1 = (lane == e1).astype(F32)
    oh2 = (lane == e2).astype(F32)
    rows = lax.broadcasted_iota(jnp.int32, (TM, TM), 0)
    cols = lax.broadcasted_iota(jnp.int32, (TM, TM), 1)
    before = (cols < rows).astype(BF16)
    prior = jnp.dot(before, (oh1 + oh2).astype(BF16), preferred_element_type=F32) + carry_ref[...]
    r1 = jnp.sum(oh1 * prior, axis=-1, keepdims=True)
    r2 = jnp.sum(oh2 * prior, axis=-1, keepdims=True)
    carry_ref[...] = carry_ref[...] + jnp.sum(oh1 + oh2, axis=0, keepdims=True)
    cnt_ref[...] = jnp.broadcast_to(carry_ref[...], cnt_ref.shape)
    packed = jnp.where(lane_i == 0, e1, jnp.where(lane_i == 1, e2, jnp.where(lane_i == 2, r1, jnp.where(lane_i == 3, r2, 0.0))))
    ids_ref[...] = packed.astype(jnp.int32)
    gates_ref[...] = jnp.where(lane_i == 0, w1, jnp.where(lane_i == 1, w2, 0.0))


def _router(x2, seq, g, sc, sh, wr):
    t = x2.shape[0]
    return pl.pallas_call(
        _router_kernel,
        grid=(t // TM,),
        in_specs=[_row_spec(D_MODEL), _full_spec((1, D_MODEL)), _batch_spec(seq), _batch_spec(seq),
                  _full_spec((D_MODEL, LANES))],
        out_specs=[pl.BlockSpec((TM,) + ROW_TILE, lambda i: (i, 0, 0)), _row_spec(LANES), _row_spec(LANES),
                   _full_spec((8, LANES))],
        out_shape=[jax.ShapeDtypeStruct((t,) + ROW_TILE, F32), jax.ShapeDtypeStruct((t, LANES), jnp.int32),
                   jax.ShapeDtypeStruct((t, LANES), F32), jax.ShapeDtypeStruct((8, LANES), F32)],
        scratch_shapes=[pltpu.VMEM((1, LANES), F32)],
        compiler_params=_cparams(("arbitrary",)),
        name="moe_router",
    )(x2, g, sc, sh, wr)


def _row_copy(src_ref, src_row, dst_ref, dst_row, sem):
    return pltpu.make_async_copy(src_ref.at[src_row], dst_ref.at[dst_row], sem)


def _dispatch_kernel(dest_ref, zfill_ref, h_ref, xs_ref, buf_ref, zero_ref, sem_ref, zsem_ref):
    i = pl.program_id(0)
    n = pl.num_programs(0)
    slot = i % 2

    def zero_copy(k):
        return pltpu.make_async_copy(zero_ref, xs_ref.at[pl.ds(zfill_ref[k], MOE_BM)], zsem_ref.at[0])

    @pl.when(i == 0)
    def _():
        zero_ref[...] = jnp.zeros_like(zero_ref)

        def start(k, c):
            @pl.when(zfill_ref[k] >= 0)
            def _():
                zero_copy(k).start()
            return c
        lax.fori_loop(0, 2 * MOE_EXPERTS, start, 0)

        def drain(k, c):
            @pl.when(zfill_ref[k] >= 0)
            def _():
                zero_copy(k).wait()
            return c
        lax.fori_loop(0, 2 * MOE_EXPERTS, drain, 0)

    def wait_slot(s):
        for _ in range(2):
            pltpu.make_async_copy(buf_ref.at[s], xs_ref.at[pl.ds(0, TM)], sem_ref.at[s]).wait()

    @pl.when(i >= 2)
    def _():
        wait_slot(slot)

    buf_ref[slot] = h_ref[...]

    n_tok = n * TM

    def issue(r, c):
        a = i * TM + r
        _row_copy(buf_ref.at[slot], r, xs_ref, dest_ref[a], sem_ref.at[slot]).start(priority=0)
        _row_copy(buf_ref.at[slot], r, xs_ref, dest_ref[n_tok + a], sem_ref.at[slot]).start(priority=1)
        return c
    lax.fori_loop(0, TM, issue, 0, unroll=8)

    @pl.when(i == n - 1)
    def _():
        wait_slot(slot)

        @pl.when(n >= 2)
        def _():
            wait_slot(1 - slot)


def _dispatch(dest, zfill, h, n_rows):
    t = h.shape[0]
    return pl.pallas_call(
        _dispatch_kernel,
        grid_spec=pltpu.PrefetchScalarGridSpec(
            num_scalar_prefetch=2,
            grid=(t // TM,),
            in_specs=[pl.BlockSpec((TM,) + ROW_TILE, lambda i, d, z: (i, 0, 0))],
            out_specs=pl.BlockSpec(memory_space=pl.ANY),
            scratch_shapes=[pltpu.VMEM((2, TM) + ROW_TILE, F32), pltpu.VMEM((MOE_BM,) + ROW_TILE, F32),
                            pltpu.SemaphoreType.DMA((2,)), pltpu.SemaphoreType.DMA((1,))],
        ),
        out_shape=jax.ShapeDtypeStruct((n_rows,) + ROW_TILE, F32),
        compiler_params=_cparams(("arbitrary",)),
        name="moe_dispatch",
    )(dest, zfill, h)


def _dest_kernel(ids_ref, pstart_ref, o_ref):
    ids = ids_ref[...]
    lane = lax.broadcasted_iota(jnp.int32, ids.shape, 1)
    pstart = pstart_ref[...]

    def row_of(k):
        base = jnp.sum(jnp.where(lane == ids[:, k:k + 1], pstart, 0.0), axis=-1, keepdims=True)
        return base + ids[:, 2 + k:3 + k].astype(F32)

    rows = jnp.where(lane == 0, row_of(0), jnp.where(lane == 1, row_of(1), 0.0))
    o_ref[...] = rows.T[:SUBLANES, :].astype(jnp.int32)


DEST_TM = 2048


def _dest_rows(ids, pstarts):
    t = ids.shape[0]
    pstart_row = jnp.zeros((1, LANES), F32).at[0, :MOE_EXPERTS].set(pstarts.astype(F32))
    tm = min(DEST_TM, t)
    out = pl.pallas_call(
        _dest_kernel,
        grid=(t // tm,),
        in_specs=[pl.BlockSpec((tm, LANES), lambda i: (i, 0)), _full_spec((1, LANES))],
        out_specs=pl.BlockSpec((SUBLANES, tm), lambda i: (0, i)),
        out_shape=jax.ShapeDtypeStruct((SUBLANES, t), jnp.int32),
        compiler_params=_cparams(("arbitrary",)),
        name="moe_dest",
    )(ids, pstart_row)
    return out[:2].reshape(2 * t)


def _expert_kernel(layer, be_ref, nb_ref, nxt_ref, xs_ref, w1_hbm, w3_hbm, w2_hbm, y_ref,
                   w1f_ref, w3f_ref, w2f_ref, w1b_ref, w3b_ref, w2b_ref, ord_ref, sem_ref):
    i = pl.program_id(0)
    used = i < nb_ref[0]
    prev = be_ref[jnp.maximum(i - 1, 0)]

    def fetch(e, s):
        return (pltpu.make_async_copy(w1_hbm.at[layer, e], w1f_ref.at[s], sem_ref.at[s]),
                pltpu.make_async_copy(w3_hbm.at[layer, e], w3f_ref.at[s], sem_ref.at[s]),
                pltpu.make_async_copy(w2_hbm.at[layer, e], w2f_ref.at[s], sem_ref.at[s]))

    @pl.when(i == 0)
    def _():
        ord_ref[0] = 0
        for c in fetch(be_ref[0], 0):
            c.start()

    @pl.when(used & ((i == 0) | (be_ref[i] != prev)))
    def _():
        s = ord_ref[0] % 2
        for c in fetch(be_ref[i], s):
            c.wait()
        w1b_ref[...] = w1f_ref[s].astype(BF16)
        w3b_ref[...] = w3f_ref[s].astype(BF16)
        w2b_ref[...] = w2f_ref[s].astype(BF16)

        @pl.when(nxt_ref[i] >= 0)
        def _():
            for c in fetch(nxt_ref[i], 1 - s):
                c.start()
        ord_ref[0] = ord_ref[0] + 1

    @pl.when(used)
    def _():
        xb = _load_tile_rows(xs_ref).astype(BF16)
        a = jnp.dot(xb, w1b_ref[...], preferred_element_type=F32)
        b = jnp.dot(xb, w3b_ref[...], preferred_element_type=F32)
        hmid = (_silu(a) * b).astype(BF16)
        _store_tile_rows(y_ref, jnp.dot(hmid, w2b_ref[...], preferred_element_type=F32))

    @pl.when(i >= nb_ref[0])
    def _():
        y_ref[...] = jnp.zeros_like(y_ref)


def _experts(blk_exp, n_used, blk_next, xs, layer, w1, w3, w2):
    n_rows = xs.shape[0]
    nb = n_rows // MOE_BM

    def row_map(i, be, nu, nx):
        return (jnp.minimum(i, nu[0] - 1), 0, 0)

    return pl.pallas_call(
        functools.partial(_expert_kernel, layer),
        grid_spec=pltpu.PrefetchScalarGridSpec(
            num_scalar_prefetch=3,
            grid=(nb,),
            in_specs=[pl.BlockSpec((MOE_BM,) + ROW_TILE, row_map),
                      pl.BlockSpec(memory_space=pl.ANY), pl.BlockSpec(memory_space=pl.ANY),
                      pl.BlockSpec(memory_space=pl.ANY)],
            out_specs=pl.BlockSpec((MOE_BM,) + ROW_TILE, lambda i, be, nu, nx: (i, 0, 0)),
            scratch_shapes=[pltpu.VMEM((2, D_MODEL, EXPERT_FF), F32), pltpu.VMEM((2, D_MODEL, EXPERT_FF), F32),
                            pltpu.VMEM((2, EXPERT_FF, D_MODEL), F32),
                            pltpu.VMEM((D_MODEL, EXPERT_FF), BF16), pltpu.VMEM((D_MODEL, EXPERT_FF), BF16),
                            pltpu.VMEM((EXPERT_FF, D_MODEL), BF16),
                            pltpu.SMEM((1,), jnp.int32), pltpu.SemaphoreType.DMA((2,))],
        ),
        out_shape=jax.ShapeDtypeStruct((n_rows,) + ROW_TILE, F32),
        compiler_params=_cparams(("arbitrary",)),
        name="moe_experts",
    )(blk_exp, n_used, blk_next, xs, w1, w3, w2)


def _combine_kernel(dest_ref, x_ref, gates_ref, gf_ref, y_ref, o_ref, buf_ref, sem_ref):
    i = pl.program_id(0)
    n = pl.num_programs(0)
    slot = i % 2

    n_tok = n * TM

    def issue(tile, s):
        def body(r, c):
            a = tile * TM + r
            _row_copy(y_ref, dest_ref[a], buf_ref.at[s, 0], r, sem_ref.at[s]).start(priority=0)
            _row_copy(y_ref, dest_ref[n_tok + a], buf_ref.at[s, 1], r, sem_ref.at[s]).start(priority=1)
            return c
        lax.fori_loop(0, TM, body, 0, unroll=8)

    @pl.when(i == 0)
    def _():
        issue(0, 0)

    @pl.when(i + 1 < n)
    def _():
        issue(i + 1, 1 - slot)

    for k in range(2):
        pltpu.make_async_copy(y_ref.at[pl.ds(0, TM)], buf_ref.at[slot, k], sem_ref.at[slot]).wait()

    w0 = gates_ref[:, 0:1]
    w1 = gates_ref[:, 1:2]
    mix = _load_tile_rows(buf_ref.at[slot, 0]) * w0 + _load_tile_rows(buf_ref.at[slot, 1]) * w1
    o_ref[...] = x_ref[...] + gf_ref[0] * mix


def _combine(dest, x2, seq, gates, gf, y):
    t = x2.shape[0]
    return pl.pallas_call(
        _combine_kernel,
        grid_spec=pltpu.PrefetchScalarGridSpec(
            num_scalar_prefetch=1,
            grid=(t // TM,),
            in_specs=[pl.BlockSpec((TM, D_MODEL), lambda i, d: (i, 0)),
                      pl.BlockSpec((TM, LANES), lambda i, d: (i, 0)),
                      pl.BlockSpec((1, 1, D_MODEL), lambda i, d: ((i * TM) // seq, 0, 0)),
                      pl.BlockSpec(memory_space=pl.ANY)],
            out_specs=pl.BlockSpec((TM, D_MODEL), lambda i, d: (i, 0)),
            scratch_shapes=[pltpu.VMEM((2, 2, TM) + ROW_TILE, F32), pltpu.SemaphoreType.DMA((2,))],
        ),
        out_shape=jax.ShapeDtypeStruct((t, D_MODEL), F32),
        compiler_params=_cparams(("arbitrary",)),
        name="moe_combine",
    )(dest, x2, gates, gf, y)


def _moe_layer(x2, seq, g, sc, sh, gf, wr, layer, w1, w3, w2):
    t = x2.shape[0]
    h, ids, gates, cnt = _router(x2, seq, g, sc, sh, wr)
    counts = cnt[0, :MOE_EXPERTS].astype(jnp.int32)
    padded = (counts + MOE_BM - 1) // MOE_BM * MOE_BM
    pends = jnp.cumsum(padded)
    pstarts = pends - padded
    dest = _dest_rows(ids, pstarts)
    n_rows = 2 * t + MOE_EXPERTS * MOE_BM
    nb = n_rows // MOE_BM
    blk_start = jnp.arange(nb, dtype=jnp.int32) * MOE_BM
    blk_exp = jnp.minimum(jnp.sum((pends[None, :] <= blk_start[:, None]).astype(jnp.int32), axis=1), MOE_EXPERTS - 1)
    n_used = (pends[-1:] // MOE_BM).astype(jnp.int32)
    tail = pends[-1] + jnp.arange(MOE_EXPERTS, dtype=jnp.int32) * MOE_BM
    zfill = jnp.concatenate([jnp.where(padded > 0, pends - MOE_BM, -1),
                             jnp.where(tail < n_rows, tail, -1)]).astype(jnp.int32)
    xs = _dispatch(dest, zfill, h, n_rows)
    eid = jnp.arange(MOE_EXPERTS, dtype=jnp.int32)
    later = (padded[None, :] > 0) & (eid[None, :] > eid[:, None])
    next_exp = jnp.min(jnp.where(later, eid[None, :], MOE_EXPERTS), axis=1)
    next_exp = jnp.where(next_exp < MOE_EXPERTS, next_exp, -1).astype(jnp.int32)
    y = _experts(blk_exp, n_used, next_exp[blk_exp], xs, layer, w1, w3, w2)
    return _combine(dest, x2, seq, gates, gf, y)


HG_LT = 512


def _sublane_roll(x, shift):
    return pltpu.roll(x, shift % x.shape[0], 0)


def _hgrn_chunk(hq, hf, hi, lb, state_t):
    c, sub = HG_CHUNK, HG_SUB
    q = _silu(hq)
    f = lb + (1.0 - lb) * _sigmoid(hf)
    lf = jnp.log(f)
    k = 1.0 - f
    v = hi
    row = lax.broadcasted_iota(jnp.int32, (c, c), 0)
    col = lax.broadcasted_iota(jnp.int32, (c, c), 1)
    start = (row // sub) * sub
    tri = jnp.concatenate([(col >= start) & (col <= row), col < start], axis=0).astype(BF16)
    sums = sum(jnp.dot(tri, part, preferred_element_type=F32) for part in _split_bf16(lf, 3))
    bloc = sums[:c]
    rref = sums[c:]
    b = rref + bloc
    qt = q * jnp.exp(bloc)
    ridx = lax.broadcasted_iota(jnp.int32, (c, HG_DK), 0)

    blocks = [jnp.zeros((sub, c), F32)]
    for i in range(1, c // sub):
        r_i = rref[i * sub:i * sub + 1, :]
        live = ridx < i * sub
        k_i = jnp.where(live, k * jnp.exp(jnp.where(live, r_i - b, 0.0)), 0.0)
        blocks.append(lax.dot_general(qt[i * sub:(i + 1) * sub].astype(BF16), k_i.astype(BF16),
                                      (((1,), (1,)), ((), ())), preferred_element_type=F32))
    attn_off = jnp.concatenate(blocks, axis=0)
    o = jnp.dot(attn_off.astype(BF16), v.astype(BF16), preferred_element_type=F32)

    pos = ridx % sub
    for d in range(sub):
        qd = q if d == 0 else _sublane_roll(q, -d)
        bd = bloc if d == 0 else _sublane_roll(bloc, -d)
        ok = pos + d < sub
        e = qd * k * jnp.exp(jnp.where(ok, bd - bloc, -jnp.inf))
        w = jnp.sum(e, axis=-1, keepdims=True) * v
        o = o + (w if d == 0 else _sublane_roll(w, d))

    qe = qt * jnp.exp(rref)
    o = o + lax.dot_general(qe.astype(BF16), state_t.astype(BF16), (((1,), (1,)), ((), ())),
                            preferred_element_type=F32)
    b_end = b[c - 1:c, :]
    kd = k * jnp.exp(b_end - b)
    new_state = state_t * jnp.exp(b_end) + lax.dot_general(v.astype(BF16), kd.astype(BF16), (((0,), (0,)), ((), ())),
                                                           preferred_element_type=F32)
    return o, new_state


def _hgrn_kernel(hq_ref, hf_ref, hi_ref, hg_ref, lb_ref, og_ref, o_ref, state_ref):
    @pl.when(pl.program_id(0) == 0)
    def _():
        state_ref[...] = jnp.zeros_like(state_ref)

    og = og_ref[...]

    def body(ci, carry):
        rows = pl.ds(pl.multiple_of(ci * HG_CHUNK, HG_CHUNK), HG_CHUNK)
        for b in range(hq_ref.shape[0]):
            for h in range(HG_HEADS):
                cs = slice(h * HG_DK, (h + 1) * HG_DK)
                o, st = _hgrn_chunk(hq_ref[b, rows, cs], hf_ref[b, rows, cs], hi_ref[b, rows, cs], lb_ref[:, cs],
                                    state_ref[b, h])
                state_ref[b, h] = st
                on = o * lax.rsqrt(jnp.mean(o * o, axis=-1, keepdims=True) + EPS) * og
                o_ref[b, rows, cs] = on * _silu(hg_ref[b, rows, cs])
        return carry
    lax.fori_loop(0, HG_LT // HG_CHUNK, body, 0)


def _hgrn(z, batch, seq, lb, o_gain):
    t = z.shape[0]
    z3 = z.reshape(batch, seq, z.shape[1])

    def col(base):
        return pl.BlockSpec((batch, HG_LT, HG_W), lambda l: (0, l, base // HG_W))

    out = pl.pallas_call(
        _hgrn_kernel,
        grid=(seq // HG_LT,),
        in_specs=[col(Z_HQ), col(Z_HF), col(Z_HI), col(Z_HG),
                  pl.BlockSpec((1, HG_W), lambda l: (0, 0)),
                  pl.BlockSpec((1, HG_DV), lambda l: (0, 0))],
        out_specs=pl.BlockSpec((batch, HG_LT, HG_W), lambda l: (0, l, 0)),
        out_shape=jax.ShapeDtypeStruct((batch, seq, HG_W), F32),
        scratch_shapes=[pltpu.VMEM((batch, HG_HEADS, HG_DV, HG_DK), F32)],
        compiler_params=_cparams(("arbitrary",)),
        name="hgrn2",
    )(z3, z3, z3, z3, lb.reshape(1, -1), o_gain.reshape(1, -1))
    return out.reshape(t, HG_W)


def _rms64(v, gain):
    return v * lax.rsqrt(jnp.mean(v * v, axis=-1, keepdims=True) + EPS) * gain


def _nsa_prep_kernel(kc_ref, vc_ref, ks_ref, vs_ref, kw_ref, vw_ref, kg_ref,
                     kcr_ref, vcr_ref, ksn_ref, vsb_ref, kwn_ref, vwb_ref):
    ones_col = (lax.broadcasted_iota(jnp.int32, (TM, NSA_VW - NSA_HD), 1) == 0).astype(F32)

    def with_ones(v):
        return jnp.concatenate([v, ones_col], axis=1).astype(BF16)

    for g in range(NSA_KV_GROUPS):
        cs = slice(g * NSA_HD, (g + 1) * NSA_HD)
        kcr_ref[0, g] = kc_ref[:, cs]
        vcr_ref[0, g] = vc_ref[:, cs]
        ksn_ref[0, g] = _rms64(ks_ref[:, cs], kg_ref[1:2, :]).astype(BF16)
        vsb_ref[0, g] = with_ones(vs_ref[:, cs])
        kwn_ref[0, g] = _rms64(kw_ref[:, cs], kg_ref[2:3, :]).astype(BF16)
        vwb_ref[0, g] = with_ones(vw_ref[:, cs])


def _nsa_prep(z, batch, seq, k_gain):
    per = seq // TM

    def col(base):
        return pl.BlockSpec((TM, NSA_KV_W), lambda i: (i, base // NSA_KV_W))

    def out(dtype, width=NSA_HD):
        return (pl.BlockSpec((1, NSA_KV_GROUPS, TM, width), lambda i: (i // per, 0, i % per, 0)),
                jax.ShapeDtypeStruct((batch, NSA_KV_GROUPS, seq, width), dtype))

    outs = [out(F32), out(F32), out(BF16), out(BF16, NSA_VW), out(BF16), out(BF16, NSA_VW)]
    return pl.pallas_call(
        _nsa_prep_kernel,
        grid=(batch * per,),
        in_specs=[col(Z_KC), col(Z_VC), col(Z_KS), col(Z_VS), col(Z_KW), col(Z_VW), _full_spec((3, NSA_HD))],
        out_specs=[o[0] for o in outs],
        out_shape=[o[1] for o in outs],
        compiler_params=_cparams(("arbitrary",)),
        name="nsa_prep",
    )(z, z, z, z, z, z, k_gain)


def _nsa_compress_kernel(ak_ref, av_ref, wk_ref, wv_ref, pe_ref, kg_ref, kc_ref, vc_ref):
    half = CMP_STRIDE * NSA_HD

    def compress(a, w_ref):
        top = jnp.dot(a, w_ref[:half, :], precision=HIGHEST, preferred_element_type=F32)
        bot = jnp.dot(a, w_ref[half:, :], precision=HIGHEST, preferred_element_type=F32)
        pe = jnp.dot(pe_ref[...], w_ref[...], precision=HIGHEST, preferred_element_type=F32)
        return top + _sublane_roll(bot, -1) + pe[0:1]

    kc_ref[0, 0] = _rms64(compress(ak_ref[0, 0], wk_ref), kg_ref[0:1, :])
    vc_ref[0, 0] = compress(av_ref[0, 0], wv_ref)


def _nsa_compress(kcr, vcr, w_ck, w_cv, cmp_pe, k_gain):
    batch, groups, seq, _ = kcr.shape
    nc = seq // CMP_STRIDE
    wide = CMP_STRIDE * NSA_HD
    spec = pl.BlockSpec((1, 1, nc, wide), lambda b, g: (b, g, 0, 0))
    ospec = pl.BlockSpec((1, 1, nc, NSA_HD), lambda b, g: (b, g, 0, 0))
    oshape = jax.ShapeDtypeStruct((batch, groups, nc, NSA_HD), F32)
    full2 = lambda shape: pl.BlockSpec(shape, lambda b, g: (0, 0))
    return pl.pallas_call(
        _nsa_compress_kernel,
        grid=(batch, groups),
        in_specs=[spec, spec, full2((2 * wide, NSA_HD)), full2((2 * wide, NSA_HD)), full2((8, 2 * wide)),
                  full2((3, NSA_HD))],
        out_specs=[ospec, ospec],
        out_shape=[oshape, oshape],
        compiler_params=_cparams(("arbitrary", "arbitrary")),
        name="nsa_compress",
    )(kcr.reshape(batch, groups, nc, wide), vcr.reshape(batch, groups, nc, wide), w_ck, w_cv,
      jnp.broadcast_to(cmp_pe.reshape(1, 2 * wide), (8, 2 * wide)), k_gain)


def _nsa_kernel(seq, kchunk, q_ref, gate_ref, kc_ref, vc_ref, ks_ref, vs_ref, kw_ref, vw_ref, qg_ref, agg_ref,
                o_ref):
    qi = pl.program_id(2)
    t0 = qi * Q_BLOCK
    jn = NSA_QPG
    nc = seq // CMP_STRIDE
    n_slc = seq // SLC_BLOCK
    n_sel = min(N_SELECT, n_slc)
    nt = (((1,), (1,)), ((), ()))

    qraw = q_ref[...]
    qs = [_rms64(qraw[:, j * NSA_HD:(j + 1) * NSA_HD], qg_ref[...]) * (NSA_HD ** -0.5) for j in range(jn)]
    qb = [q.astype(BF16) for q in qs]

    def pos_of(shape):
        return t0 + lax.broadcasted_iota(jnp.int32, shape, 0)

    cmp_end = lax.broadcasted_iota(jnp.int32, (Q_BLOCK, nc), 1) * CMP_STRIDE + (CMP_BLOCK - 1)
    valid_c = cmp_end <= pos_of((Q_BLOCK, nc))
    sees_any = pos_of((Q_BLOCK, 1)) >= CMP_BLOCK - 1
    kc_hi, kc_lo = _split_bf16(kc_ref[0, 0], 2)
    vcb = vc_ref[0, 0].astype(BF16)
    o_c = []
    imp = None
    for j in range(jn):
        qh, ql = _split_bf16(qs[j], 2)
        s = (lax.dot_general(qh, kc_hi, nt, preferred_element_type=F32)
             + (lax.dot_general(qh, kc_lo, nt, preferred_element_type=F32)
                + lax.dot_general(ql, kc_hi, nt, preferred_element_type=F32)))
        sm = jnp.where(valid_c, s, NEG)
        e = jnp.exp(sm - jnp.max(sm, axis=-1, keepdims=True))
        p = e * jnp.where(sees_any, 1.0 / jnp.sum(e, axis=-1, keepdims=True), 0.0)
        o_c.append(jnp.dot(p.astype(BF16), vcb, preferred_element_type=F32))
        imp = p if imp is None else imp + p

    agg = agg_ref[...].astype(BF16)
    imp_s = sum(jnp.dot(part, agg, preferred_element_type=F32) for part in _split_bf16(imp, 3))
    ids = lax.broadcasted_iota(jnp.int32, (Q_BLOCK, n_slc), 1)
    q_blk = pos_of((Q_BLOCK, n_slc)) // SLC_BLOCK
    forced = (ids == 0) | (ids == q_blk) | (ids == q_blk - 1)
    score_t = jnp.where(ids > q_blk, NEG, jnp.where(forced, FORCE, imp_s)).T
    ids_t = lax.broadcasted_iota(jnp.int32, (n_slc, Q_BLOCK), 0).astype(F32)
    cur = score_t
    sel_t = jnp.zeros((n_slc, Q_BLOCK), F32)
    for _ in range(n_sel):
        m = jnp.max(cur, axis=0, keepdims=True)
        first = jnp.min(jnp.where(cur == m, ids_t, float(n_slc)), axis=0, keepdims=True)
        hit = ids_t == first
        sel_t = jnp.where(hit, 1.0, sel_t)
        cur = jnp.where(hit, -jnp.inf, cur)
    sel = jnp.where(score_t > NEG / 2, sel_t, 0.0).T.astype(BF16)

    span = Q_BLOCK + WINDOW
    start = pl.multiple_of(jnp.maximum(t0 - WINDOW, 0), Q_BLOCK)
    wsl = pl.ds(start, span)
    kpos = start + lax.broadcasted_iota(jnp.int32, (Q_BLOCK, span), 1)
    wpos = pos_of((Q_BLOCK, span))
    bias_w = jnp.where((kpos <= wpos) & (kpos > wpos - WINDOW), 0.0, NEG)
    kw = kw_ref[0, 0, wsl, :]
    vw = vw_ref[0, 0, wsl, :]
    o_w = []
    for j in range(jn):
        s = lax.dot_general(qb[j], kw, nt, preferred_element_type=F32) + bias_w
        e = jnp.exp((s - jnp.max(s, axis=-1, keepdims=True)).astype(BF16))
        ow = jnp.dot(e, vw, preferred_element_type=F32)
        o_w.append(ow[:, :NSA_HD] / ow[:, NSA_HD:NSA_HD + 1])

    per_chunk = kchunk // SLC_BLOCK
    blk_of_key = lax.broadcasted_iota(jnp.int32, (n_slc, kchunk), 1) // SLC_BLOCK
    blk_row = lax.broadcasted_iota(jnp.int32, (n_slc, kchunk), 0)
    key_lane = lax.broadcasted_iota(jnp.int32, (Q_BLOCK, kchunk), 1)
    qpos = pos_of((Q_BLOCK, kchunk))

    def sel_body(c, carry):
        ksl = pl.ds(pl.multiple_of(c * kchunk, kchunk), kchunk)
        k_c = ks_ref[0, 0, ksl, :]
        v_c = vs_ref[0, 0, ksl, :]
        expand = (blk_row == blk_of_key + c * per_chunk).astype(BF16)
        chosen = jnp.dot(sel, expand, preferred_element_type=F32)
        bias = jnp.where((chosen > 0.5) & (key_lane + c * kchunk <= qpos), 0.0, NEG)
        m_run, acc = carry
        s = lax.dot_general(qb_all, k_c, nt, preferred_element_type=F32) + jnp.concatenate([bias] * jn, axis=0)
        m_new = jnp.maximum(m_run, jnp.max(s, axis=-1, keepdims=True))
        alpha = jnp.exp(m_run - m_new)
        p = jnp.exp((s - m_new).astype(BF16))
        acc_new = acc * alpha + jnp.dot(p, v_c, preferred_element_type=F32)
        return m_new, acc_new

    rows = jn * Q_BLOCK
    qb_all = jnp.concatenate(qb, axis=0)
    n_chunks = (t0 + Q_BLOCK + kchunk - 1) // kchunk
    init = (jnp.full((rows, 1), NEG, F32), jnp.zeros((rows, NSA_VW), F32))
    _, acc_fin = lax.fori_loop(0, n_chunks, sel_body, init)
    o_sel = acc_fin[:, :NSA_HD] / acc_fin[:, NSA_HD:NSA_HD + 1]

    gates = _sigmoid(gate_ref[...])
    outs = []
    for j in range(jn):
        o_s = o_sel[j * Q_BLOCK:(j + 1) * Q_BLOCK]
        outs.append(gates[:, 3 * j:3 * j + 1] * o_c[j] + gates[:, 3 * j + 1:3 * j + 2] * o_s
                    + gates[:, 3 * j + 2:3 * j + 3] * o_w[j])
    o_ref[...] = jnp.concatenate(outs, axis=1)


def _selection_weights(seq):
    nc = seq // CMP_STRIDE
    n_slc = seq // SLC_BLOCK
    ratio = SLC_BLOCK // CMP_STRIDE
    n = jnp.arange(nc)[:, None]
    j = jnp.arange(n_slc)[None, :]
    o = n - ratio * j + (CMP_BLOCK // CMP_STRIDE - 1)
    w = jnp.asarray(SEL_AGG_W, F32)
    return jnp.where((o >= 0) & (o < len(SEL_AGG_W)), w[jnp.clip(o, 0, len(SEL_AGG_W) - 1)], 0.0)


def _nsa(z, batch, seq, w_ck, w_cv, cmp_pe, q_gain, k_gain):
    t = z.shape[0]
    kcr, vcr, ksn, vsb, kwn, vwb = _nsa_prep(z, batch, seq, k_gain)
    kc, vc = _nsa_compress(kcr, vcr, w_ck, w_cv, cmp_pe, k_gain)
    nq = seq // Q_BLOCK
    nc = seq // CMP_STRIDE
    n_slc = seq // SLC_BLOCK
    kchunk = min(NSA_KCHUNK, seq)
    qw = NSA_QPG * NSA_HD
    cmp_spec = pl.BlockSpec((1, 1, nc, NSA_HD), lambda b, g, i: (b, g, 0, 0))
    key_spec = pl.BlockSpec((1, 1, seq, NSA_HD), lambda b, g, i: (b, g, 0, 0))
    val_spec = pl.BlockSpec((1, 1, seq, NSA_VW), lambda b, g, i: (b, g, 0, 0))
    return pl.pallas_call(
        functools.partial(_nsa_kernel, seq, kchunk),
        grid=(batch, NSA_KV_GROUPS, nq),
        in_specs=[pl.BlockSpec((Q_BLOCK, qw), lambda b, g, i: (b * nq + i, g)),
                  pl.BlockSpec((Q_BLOCK, LANES), lambda b, g, i: (b * nq + i, Z_GATE // LANES + g)),
                  cmp_spec, cmp_spec, key_spec, val_spec, key_spec, val_spec,
                  pl.BlockSpec((1, NSA_HD), lambda b, g, i: (0, 0)),
                  pl.BlockSpec((nc, n_slc), lambda b, g, i: (0, 0))],
        out_specs=pl.BlockSpec((Q_BLOCK, qw), lambda b, g, i: (b * nq + i, g)),
        out_shape=jax.ShapeDtypeStruct((t, NSA_W), F32),
        compiler_params=_cparams(("arbitrary", "arbitrary", "arbitrary")),
        name="nsa_attention",
    )(z, z, kc, vc, ksn, vsb, kwn, vwb, q_gain.reshape(1, -1), _selection_weights(seq))


def _pad_in_proj(w_in):
    per_group = NSA_QPG * 3
    gate_lo = Z_GATE
    out = jnp.zeros((D_MODEL, Z_N), F32)
    out = out.at[:, :gate_lo].set(w_in[:, :gate_lo])
    for g in range(NSA_KV_GROUPS):
        out = out.at[:, gate_lo + g * LANES:gate_lo + g * LANES + per_group].set(
            w_in[:, gate_lo + g * per_group:gate_lo + (g + 1) * per_group])
    return out.at[:, Z_HQ:].set(w_in[:, gate_lo + NSA_HEADS * 3:])


def _even_mixer(x2, batch, seq, g, sc, sh, gm, w_in, w_out, w_ck, w_cv, cmp_pe, q_gain, k_gain, lb, o_gain):
    z = _in_proj(x2, seq, g, sc, sh, _pad_in_proj(w_in).astype(BF16))
    r = _hgrn(z, batch, seq, lb, o_gain)
    a = _nsa(z, batch, seq, w_ck, w_cv, cmp_pe, q_gain, k_gain)
    return _out_proj(x2, seq, a, r, gm, w_out.astype(BF16))


def _odd_mixer(x2, seq, g, sc, sh, gm, w_pw1, dw, dw_b, ln_g, ln_b, w_pw2):
    u = _glu_proj(x2, seq, g, sc, sh, w_pw1.astype(BF16))
    dw_pad = jnp.zeros((CONV_HALO, D_MODEL), F32).at[:CONV_W].set(dw)
    return _conv_module(x2, seq, u, dw_pad, dw_b.reshape(1, -1), ln_g.reshape(1, -1), ln_b.reshape(1, -1), gm,
                        w_pw2.astype(BF16))


def _router_weights(w_group, w_expert):
    wr = jnp.zeros((D_MODEL, LANES), F32)
    return wr.at[:, :MOE_GROUPS].set(w_group).at[:, MOE_GROUPS:MOE_GROUPS + MOE_EXPERTS].set(w_expert)


def _lower_bounds_kernel(l_ref, o_ref):
    logits = l_ref[...]
    e = jnp.exp(logits - jnp.max(logits, axis=0, keepdims=True))
    p = e / jnp.sum(e, axis=0, keepdims=True)
    n = logits.shape[0]
    acc = jnp.zeros_like(p[0:1])
    for i in range(n):
        acc = acc + p[i:i + 1]
        o_ref[i:i + 1, :] = acc - p[0:1]


def _lower_bounds(lb_logits):
    return pl.pallas_call(
        _lower_bounds_kernel,
        out_shape=jax.ShapeDtypeStruct(lb_logits.shape, F32),
        name="hgrn_lower_bounds",
    )(lb_logits)


def kernel(x, c, ada_w, ada_b, norm_mix, norm_ffn, mix_w_in, mix_w_out, nsa_cmp_wk, nsa_cmp_wv, nsa_cmp_pe, nsa_q_gain, nsa_k_gain, hgrn_lb_logits, hgrn_o_gain, conv_w_pw1, conv_dw, conv_dw_b, conv_ln_g, conv_ln_b, conv_w_pw2, moe_w_group, moe_w_expert, moe_w1, moe_w3, moe_w2):
    batch, seq, d = x.shape
    x2 = x.reshape(batch * seq, d)
    mod = _modulation(c, ada_w, ada_b)
    lower_bounds = _lower_bounds(hgrn_lb_logits)
    for layer in range(DEPTH):
        sh_m, sc_m, g_m, sh_f, sc_f, g_f = (mod[layer, :, k] for k in range(6))
        i = layer // 2
        gain = norm_mix[layer].reshape(1, d)
        if layer % 2 == 0:
            x2 = _even_mixer(x2, batch, seq, gain, sc_m, sh_m, g_m, mix_w_in[i], mix_w_out[i], nsa_cmp_wk[i],
                             nsa_cmp_wv[i], nsa_cmp_pe[i], nsa_q_gain[i], nsa_k_gain[i], lower_bounds[i],
                             hgrn_o_gain[i])
        else:
            x2 = _odd_mixer(x2, seq, gain, sc_m, sh_m, g_m, conv_w_pw1[i], conv_dw[i], conv_dw_b[i], conv_ln_g[i],
                            conv_ln_b[i], conv_w_pw2[i])
        x2 = _moe_layer(x2, seq, norm_ffn[layer].reshape(1, d), sc_f, sh_f, g_f,
                        _router_weights(moe_w_group[layer], moe_w_expert[layer]), layer, moe_w1, moe_w3, moe_w2)
    return x2.reshape(batch, seq, d)
```

```python
import functools

import jax
import jax.numpy as jnp
from jax import lax
from jax.experimental import pallas as pl
from jax.experimental.pallas import tpu as pltpu

F32 = jnp.float32
BF16 = jnp.bfloat16
HIGHEST = lax.Precision.HIGHEST

D_MODEL = 1024
DEPTH = 4

NSA_HEADS = 8
NSA_KV_GROUPS = 2
NSA_QPG = NSA_HEADS // NSA_KV_GROUPS
NSA_HD = 64
NSA_W = NSA_HEADS * NSA_HD
NSA_KV_W = NSA_KV_GROUPS * NSA_HD
NSA_KCHUNK = 512
NSA_VW = 128
CMP_BLOCK = 32
CMP_STRIDE = 16
SLC_BLOCK = 64
N_SELECT = 16
WINDOW = 512
Q_BLOCK = 256
SEL_AGG_W = (1.0, 2.0, 2.0, 2.0, 1.0)

HG_HEADS = 4
HG_DK = 128
HG_DV = 128
HG_W = HG_HEADS * HG_DV
HG_CHUNK = 64
HG_SUB = 8

CONV_W = 31
MOE_GROUPS = 4
MOE_EPG = 8
MOE_EXPERTS = MOE_GROUPS * MOE_EPG
EXPERT_FF = 512

EPS = 1e-6
NEG = -1e30
FORCE = 1e4

LANES = 128
VMEM_LIMIT = 56 * 1024 * 1024

Z_Q = 0
Z_KC = 512
Z_VC = 640
Z_KS = 768
Z_VS = 896
Z_KW = 1024
Z_VW = 1152
Z_GATE = 1280
Z_HQ = Z_GATE + NSA_KV_GROUPS * LANES
Z_HF = Z_HQ + HG_W
Z_HI = Z_HF + HG_W
Z_HG = Z_HI + HG_W
Z_N = Z_HG + HG_W

TM = 256
MOE_BM = 512


def _cparams(sem):
    return pltpu.CompilerParams(dimension_semantics=sem, vmem_limit_bytes=VMEM_LIMIT)


def _split_bf16(x, terms):
    parts = []
    for _ in range(terms):
        p = x.astype(BF16)
        parts.append(p)
        x = x - p.astype(F32)
    return parts


def _dot_bf16x3(a, b, dims):
    ah, al = _split_bf16(a, 2)
    bh, bl = _split_bf16(b, 2)

    def d(x, y):
        return lax.dot_general(x, y, dims, preferred_element_type=F32)
    return d(ah, bh) + (d(ah, bl) + d(al, bh))


def _sigmoid(x):
    return 1.0 / (1.0 + jnp.exp(-x))


def _silu(x):
    return x * _sigmoid(x)


def _mod_kernel(c_ref, w_ref, b_ref, o_ref):
    c = c_ref[...]
    o_ref[0] = jnp.dot(_silu(c), w_ref[0], precision=HIGHEST, preferred_element_type=F32) + b_ref[0]


def _modulation(c, ada_w, ada_b):
    b = c.shape[0]
    rows = 8
    c_pad = jnp.zeros((rows, D_MODEL), F32).at[:b].set(c)
    n = 6 * D_MODEL
    tn = 1536
    out = pl.pallas_call(
        _mod_kernel,
        grid=(DEPTH, n // tn),
        in_specs=[
            pl.BlockSpec((rows, D_MODEL), lambda l, j: (0, 0)),
            pl.BlockSpec((1, D_MODEL, tn), lambda l, j: (l, 0, j)),
            pl.BlockSpec((1, 1, tn), lambda l, j: (l, 0, j)),
        ],
        out_specs=pl.BlockSpec((1, rows, tn), lambda l, j: (l, 0, j)),
        out_shape=jax.ShapeDtypeStruct((DEPTH, rows, n), F32),
        compiler_params=_cparams(("arbitrary", "arbitrary")),
        name="adaln_mod",
    )(c_pad, ada_w, ada_b.reshape(DEPTH, 1, n))
    return out[:, :b].reshape(DEPTH, b, 6, 1, D_MODEL)


def _norm_mod(x, g, sc, sh):
    y = x * lax.rsqrt(jnp.mean(x * x, axis=-1, keepdims=True) + EPS)
    return (y * g) * (1.0 + sc) + sh


PROJ_TM = 512


def _row_spec(width, tm=TM):
    return pl.BlockSpec((tm, width), lambda i: (i, 0))


def _full_spec(shape):
    nd = len(shape)
    return pl.BlockSpec(shape, lambda i: (0,) * nd)


def _batch_spec(seq, tm=TM):
    return pl.BlockSpec((1, 1, D_MODEL), lambda i: ((i * tm) // seq, 0, 0))


def _in_proj_kernel(x_ref, g_ref, sc_ref, sh_ref, w_ref, o_ref):
    h = _norm_mod(x_ref[...], g_ref[...], sc_ref[0], sh_ref[0]).astype(BF16)
    tn = Z_N // 4
    for j in range(4):
        o_ref[:, j * tn:(j + 1) * tn] = jnp.dot(h, w_ref[:, j * tn:(j + 1) * tn], preferred_element_type=F32)


def _in_proj(x2, seq, g, sc, sh, w_bf):
    t = x2.shape[0]
    return pl.pallas_call(
        _in_proj_kernel,
        grid=(t // PROJ_TM,),
        in_specs=[_row_spec(D_MODEL, PROJ_TM), _full_spec((1, D_MODEL)), _batch_spec(seq, PROJ_TM),
                  _batch_spec(seq, PROJ_TM), _full_spec((D_MODEL, Z_N))],
        out_specs=_row_spec(Z_N, PROJ_TM),
        out_shape=jax.ShapeDtypeStruct((t, Z_N), F32),
        compiler_params=_cparams(("arbitrary",)),
        name="in_proj",
    )(x2, g, sc, sh, w_bf)


def _out_proj_kernel(x_ref, a_ref, r_ref, gm_ref, w_ref, o_ref):
    y = jnp.dot(a_ref[...].astype(BF16), w_ref[:NSA_W, :], preferred_element_type=F32)
    y = y + jnp.dot(r_ref[...].astype(BF16), w_ref[NSA_W:, :], preferred_element_type=F32)
    o_ref[...] = x_ref[...] + gm_ref[0] * y


def _out_proj(x2, seq, a, r, gm, w_bf):
    t = x2.shape[0]
    return pl.pallas_call(
        _out_proj_kernel,
        grid=(t // PROJ_TM,),
        in_specs=[_row_spec(D_MODEL, PROJ_TM), _row_spec(NSA_W, PROJ_TM), _row_spec(HG_W, PROJ_TM),
                  _batch_spec(seq, PROJ_TM), _full_spec((NSA_W + HG_W, D_MODEL))],
        out_specs=_row_spec(D_MODEL, PROJ_TM),
        out_shape=jax.ShapeDtypeStruct((t, D_MODEL), F32),
        compiler_params=_cparams(("arbitrary",)),
        name="out_proj",
    )(x2, a, r, gm, w_bf)


CONV_HALO = 32
CONV_RB = 64
CONV_CW = 256
SUBLANES = 8
CONV_SHIFT_ROWS = TM + CONV_HALO - SUBLANES


def _conv_kernel(seq, x_ref, xh_ref, g_ref, sc_ref, sh_ref, w1_ref, dw_ref, dwb_ref, lng_ref, lnb_ref, gm_ref,
                 w2_ref, o_ref, ext_ref, shift_ref, act_ref):
    i = pl.program_id(0)
    first = (i * TM) % seq == 0
    x_ext = jnp.concatenate([xh_ref[...], x_ref[...]], axis=0)
    h = _norm_mod(x_ext, g_ref[...], sc_ref[0], sh_ref[0]).astype(BF16)
    in_halo = lax.broadcasted_iota(jnp.int32, (CONV_HALO + TM, CONV_CW), 0) < CONV_HALO
    off = CONV_HALO - (CONV_W - 1)
    for cb in range(D_MODEL // CONV_CW):
        cw = slice(cb * CONV_CW, (cb + 1) * CONV_CW)
        a = jnp.dot(h, w1_ref[:, cw], preferred_element_type=F32)
        gate = jnp.dot(h, w1_ref[:, D_MODEL + cb * CONV_CW:D_MODEL + (cb + 1) * CONV_CW], preferred_element_type=F32)
        ext_ref[:, cw] = jnp.where(in_halo & first, 0.0, a * _sigmoid(gate))
        for s in range(1, SUBLANES):
            shift_ref[s - 1, :, cw] = ext_ref[s:s + CONV_SHIFT_ROWS, cw]
        for r in range(TM // CONV_RB):
            for c in range(cb * CONV_CW // LANES, (cb + 1) * CONV_CW // LANES):
                cs = slice(c * LANES, (c + 1) * LANES)
                acc = jnp.zeros((CONV_RB, LANES), F32)
                for k in range(CONV_W):
                    s = (off + k) % SUBLANES
                    lo = r * CONV_RB + off + k - s
                    win = ext_ref[lo:lo + CONV_RB, cs] if s == 0 else shift_ref[s - 1, lo:lo + CONV_RB, cs]
                    acc = acc + win * dw_ref[k:k + 1, cs]
                act_ref[r * CONV_RB:(r + 1) * CONV_RB, cs] = acc + dwb_ref[:, cs]
    v = act_ref[...]
    mu = jnp.mean(v, axis=-1, keepdims=True)
    var = jnp.mean(jnp.square(v - mu), axis=-1, keepdims=True)
    y = (v - mu) * lax.rsqrt(var + EPS) * lng_ref[...] + lnb_ref[...]
    y = _silu(y).astype(BF16)
    o_ref[...] = x_ref[...] + gm_ref[0] * jnp.dot(y, w2_ref[...], preferred_element_type=F32)


def _conv_module(x2, seq, g, sc, sh, w1_bf, dw, dwb, lng, lnb, gm, w2_bf):
    t = x2.shape[0]
    per = TM // CONV_HALO
    return pl.pallas_call(
        functools.partial(_conv_kernel, seq),
        grid=(t // TM,),
        in_specs=[_row_spec(D_MODEL),
                  pl.BlockSpec((CONV_HALO, D_MODEL), lambda i: (jnp.maximum(i * per - 1, 0), 0)),
                  _full_spec((1, D_MODEL)), _batch_spec(seq), _batch_spec(seq), _full_spec((D_MODEL, 2 * D_MODEL)),
                  _full_spec((CONV_HALO, D_MODEL)), _full_spec((1, D_MODEL)), _full_spec((1, D_MODEL)),
                  _full_spec((1, D_MODEL)), _batch_spec(seq), _full_spec((D_MODEL, D_MODEL))],
        out_specs=_row_spec(D_MODEL),
        out_shape=jax.ShapeDtypeStruct((t, D_MODEL), F32),
        scratch_shapes=[pltpu.VMEM((TM + CONV_HALO, D_MODEL), F32),
                        pltpu.VMEM((SUBLANES - 1, CONV_SHIFT_ROWS, D_MODEL), F32),
                        pltpu.VMEM((TM, D_MODEL), F32)],
        compiler_params=_cparams(("arbitrary",)),
        name="conv_module",
    )(x2, x2, g, sc, sh, w1_bf, dw, dwb, lng, lnb, gm, w2_bf)


ROW_TILE = (SUBLANES, LANES)


def _store_tile_rows(ref, v):
    ref[...] = v.reshape((v.shape[0],) + ROW_TILE)


def _load_tile_rows(ref):
    return ref[...].reshape(ref.shape[0], D_MODEL)


def _first_index_of_max(v, lane):
    m = jnp.max(v, axis=-1, keepdims=True)
    idx = jnp.min(jnp.where(v == m, lane, float(LANES)), axis=-1, keepdims=True)
    return m, idx


def _router_kernel(x_ref, g_ref, sc_ref, sh_ref, wr_ref, h_ref, ids_ref, gates_ref, cnt_ref, carry_ref):
    i = pl.program_id(0)

    @pl.when(i == 0)
    def _():
        carry_ref[...] = jnp.zeros_like(carry_ref)

    h = _norm_mod(x_ref[...], g_ref[...], sc_ref[0], sh_ref[0])
    _store_tile_rows(h_ref, h)
    logits = _dot_bf16x3(h, wr_ref[...], (((1,), (0,)), ((), ())))
    lane_i = lax.broadcasted_iota(jnp.int32, logits.shape, 1)
    lane = lane_i.astype(F32)
    gl = jnp.where(lane < MOE_GROUPS, logits, -jnp.inf)
    gmax, gsel = _first_index_of_max(gl, lane)
    g_gate = 1.0 / jnp.sum(jnp.exp(gl - gmax), axis=-1, keepdims=True)
    lo = MOE_GROUPS + MOE_EPG * gsel
    in_grp = (lane >= lo) & (lane < lo + MOE_EPG)
    el = jnp.where(in_grp, logits, -jnp.inf)
    emax = jnp.max(el, axis=-1, keepdims=True)
    p = jnp.exp(el - emax)
    prob = p / jnp.sum(p, axis=-1, keepdims=True)
    cand = jnp.where(in_grp, prob, -1.0)
    v1, i1 = _first_index_of_max(cand, lane)
    cand2 = jnp.where(lane == i1, -1.0, cand)
    v2, i2 = _first_index_of_max(cand2, lane)
    den = v1 + v2
    w1 = g_gate * v1 / den
    w2 = g_gate * v2 / den
    e1 = i1 - MOE_GROUPS
    e2 = i2 - MOE_GROUPS
    oh1 = (lane == e1).astype(F32)
    oh2 = (lane == e2).astype(F32)
    rows = lax.broadcasted_iota(jnp.int32, (TM, TM), 0)
    cols = lax.broadcasted_iota(jnp.int32, (TM, TM), 1)
    before = (cols < rows).astype(BF16)
    prior = jnp.dot(before, (oh1 + oh2).astype(BF16), preferred_element_type=F32) + carry_ref[...]
    r1 = jnp.sum(oh1 * prior, axis=-1, keepdims=True)
    r2 = jnp.sum(oh2 * prior, axis=-1, keepdims=True)
    carry_ref[...] = carry_ref[...] + jnp.sum(oh1 + oh2, axis=0, keepdims=True)
    cnt_ref[...] = jnp.broadcast_to(carry_ref[...], cnt_ref.shape)
    packed = jnp.where(lane_i == 0, e1, jnp.where(lane_i == 1, e2, jnp.where(lane_i == 2, r1, jnp.where(lane_i == 3, r2, 0.0))))
    ids_ref[...] = packed.astype(jnp.int32)
    gates_ref[...] = jnp.where(lane_i == 0, w1, jnp.where(lane_i == 1, w2, 0.0))


def _router(x2, seq, g, sc, sh, wr):
    t = x2.shape[0]
    return pl.pallas_call(
        _router_kernel,
        grid=(t // TM,),
        in_specs=[_row_spec(D_MODEL), _full_spec((1, D_MODEL)), _batch_spec(seq), _batch_spec(seq),
                  _full_spec((D_MODEL, LANES))],
        out_specs=[pl.BlockSpec((TM,) + ROW_TILE, lambda i: (i, 0, 0)), _row_spec(LANES), _row_spec(LANES),
                   _full_spec((8, LANES))],
        out_shape=[jax.ShapeDtypeStruct((t,) + ROW_TILE, F32), jax.ShapeDtypeStruct((t, LANES), jnp.int32),
                   jax.ShapeDtypeStruct((t, LANES), F32), jax.ShapeDtypeStruct((8, LANES), F32)],
        scratch_shapes=[pltpu.VMEM((1, LANES), F32)],
        compiler_params=_cparams(("arbitrary",)),
        name="moe_router",
    )(x2, g, sc, sh, wr)


def _row_copy(src_ref, src_row, dst_ref, dst_row, sem):
    return pltpu.make_async_copy(src_ref.at[src_row], dst_ref.at[dst_row], sem)


def _dispatch_kernel(dest_ref, zfill_ref, h_ref, xs_ref, buf_ref, zero_ref, sem_ref, zsem_ref):
    i = pl.program_id(0)
    n = pl.num_programs(0)
    slot = i % 2

    def zero_copy(k):
        return pltpu.make_async_copy(zero_ref, xs_ref.at[pl.ds(zfill_ref[k], MOE_BM)], zsem_ref.at[0])

    @pl.when(i == 0)
    def _():
        zero_ref[...] = jnp.zeros_like(zero_ref)

        def start(k, c):
            @pl.when(zfill_ref[k] >= 0)
            def _():
                zero_copy(k).start()
            return c
        lax.fori_loop(0, 2 * MOE_EXPERTS, start, 0)

        def drain(k, c):
            @pl.when(zfill_ref[k] >= 0)
            def _():
                zero_copy(k).wait()
            return c
        lax.fori_loop(0, 2 * MOE_EXPERTS, drain, 0)

    def wait_slot(s):
        for _ in range(2):
            pltpu.make_async_copy(buf_ref.at[s], xs_ref.at[pl.ds(0, TM)], sem_ref.at[s]).wait()

    @pl.when(i >= 2)
    def _():
        wait_slot(slot)

    buf_ref[slot] = h_ref[...]

    n_tok = n * TM

    def issue(r, c):
        a = i * TM + r
        _row_copy(buf_ref.at[slot], r, xs_ref, dest_ref[a], sem_ref.at[slot]).start(priority=0)
        _row_copy(buf_ref.at[slot], r, xs_ref, dest_ref[n_tok + a], sem_ref.at[slot]).start(priority=1)
        return c
    lax.fori_loop(0, TM, issue, 0, unroll=8)

    @pl.when(i == n - 1)
    def _():
        wait_slot(slot)

        @pl.when(n >= 2)
        def _():
            wait_slot(1 - slot)


def _dispatch(dest, zfill, h, n_rows):
    t = h.shape[0]
    return pl.pallas_call(
        _dispatch_kernel,
        grid_spec=pltpu.PrefetchScalarGridSpec(
            num_scalar_prefetch=2,
            grid=(t // TM,),
            in_specs=[pl.BlockSpec((TM,) + ROW_TILE, lambda i, d, z: (i, 0, 0))],
            out_specs=pl.BlockSpec(memory_space=pl.ANY),
            scratch_shapes=[pltpu.VMEM((2, TM) + ROW_TILE, F32), pltpu.VMEM((MOE_BM,) + ROW_TILE, F32),
                            pltpu.SemaphoreType.DMA((2,)), pltpu.SemaphoreType.DMA((1,))],
        ),
        out_shape=jax.ShapeDtypeStruct((n_rows,) + ROW_TILE, F32),
        compiler_params=_cparams(("arbitrary",)),
        name="moe_dispatch",
    )(dest, zfill, h)


def _dest_kernel(ids_ref, pstart_ref, o_ref):
    ids = ids_ref[...]
    lane = lax.broadcasted_iota(jnp.int32, ids.shape, 1)
    pstart = pstart_ref[...]

    def row_of(k):
        base = jnp.sum(jnp.where(lane == ids[:, k:k + 1], pstart, 0.0), axis=-1, keepdims=True)
        return base + ids[:, 2 + k:3 + k].astype(F32)

    rows = jnp.where(lane == 0, row_of(0), jnp.where(lane == 1, row_of(1), 0.0))
    o_ref[...] = rows.T[:SUBLANES, :].astype(jnp.int32)


DEST_TM = 2048


def _dest_rows(ids, pstarts):
    t = ids.shape[0]
    pstart_row = jnp.zeros((1, LANES), F32).at[0, :MOE_EXPERTS].set(pstarts.astype(F32))
    tm = min(DEST_TM, t)
    out = pl.pallas_call(
        _dest_kernel,
        grid=(t // tm,),
        in_specs=[pl.BlockSpec((tm, LANES), lambda i: (i, 0)), _full_spec((1, LANES))],
        out_specs=pl.BlockSpec((SUBLANES, tm), lambda i: (0, i)),
        out_shape=jax.ShapeDtypeStruct((SUBLANES, t), jnp.int32),
        compiler_params=_cparams(("arbitrary",)),
        name="moe_dest",
    )(ids, pstart_row)
    return out[:2].reshape(2 * t)


def _expert_kernel(layer, be_ref, nb_ref, nxt_ref, xs_ref, w1_hbm, w3_hbm, w2_hbm, y_ref,
                   w1f_ref, w3f_ref, w2f_ref, w1b_ref, w3b_ref, w2b_ref, ord_ref, sem_ref):
    i = pl.program_id(0)
    used = i < nb_ref[0]
    prev = be_ref[jnp.maximum(i - 1, 0)]

    def fetch(e, s):
        return (pltpu.make_async_copy(w1_hbm.at[layer, e], w1f_ref.at[s], sem_ref.at[s]),
                pltpu.make_async_copy(w3_hbm.at[layer, e], w3f_ref.at[s], sem_ref.at[s]),
                pltpu.make_async_copy(w2_hbm.at[layer, e], w2f_ref.at[s], sem_ref.at[s]))

    @pl.when(i == 0)
    def _():
        ord_ref[0] = 0
        for c in fetch(be_ref[0], 0):
            c.start()

    @pl.when(used & ((i == 0) | (be_ref[i] != prev)))
    def _():
        s = ord_ref[0] % 2
        for c in fetch(be_ref[i], s):
            c.wait()
        w1b_ref[...] = w1f_ref[s].astype(BF16)
        w3b_ref[...] = w3f_ref[s].astype(BF16)
        w2b_ref[...] = w2f_ref[s].astype(BF16)

        @pl.when(nxt_ref[i] >= 0)
        def _():
            for c in fetch(nxt_ref[i], 1 - s):
                c.start()
        ord_ref[0] = ord_ref[0] + 1

    @pl.when(used)
    def _():
        xb = _load_tile_rows(xs_ref).astype(BF16)
        a = jnp.dot(xb, w1b_ref[...], preferred_element_type=F32)
        b = jnp.dot(xb, w3b_ref[...], preferred_element_type=F32)
        hmid = (_silu(a) * b).astype(BF16)
        _store_tile_rows(y_ref, jnp.dot(hmid, w2b_ref[...], preferred_element_type=F32))

    @pl.when(i >= nb_ref[0])
    def _():
        y_ref[...] = jnp.zeros_like(y_ref)


def _experts(blk_exp, n_used, blk_next, xs, layer, w1, w3, w2):
    n_rows = xs.shape[0]
    nb = n_rows // MOE_BM

    def row_map(i, be, nu, nx):
        return (jnp.minimum(i, nu[0] - 1), 0, 0)

    return pl.pallas_call(
        functools.partial(_expert_kernel, layer),
        grid_spec=pltpu.PrefetchScalarGridSpec(
            num_scalar_prefetch=3,
            grid=(nb,),
            in_specs=[pl.BlockSpec((MOE_BM,) + ROW_TILE, row_map),
                      pl.BlockSpec(memory_space=pl.ANY), pl.BlockSpec(memory_space=pl.ANY),
                      pl.BlockSpec(memory_space=pl.ANY)],
            out_specs=pl.BlockSpec((MOE_BM,) + ROW_TILE, lambda i, be, nu, nx: (i, 0, 0)),
            scratch_shapes=[pltpu.VMEM((2, D_MODEL, EXPERT_FF), F32), pltpu.VMEM((2, D_MODEL, EXPERT_FF), F32),
                            pltpu.VMEM((2, EXPERT_FF, D_MODEL), F32),
                            pltpu.VMEM((D_MODEL, EXPERT_FF), BF16), pltpu.VMEM((D_MODEL, EXPERT_FF), BF16),
                            pltpu.VMEM((EXPERT_FF, D_MODEL), BF16),
                            pltpu.SMEM((1,), jnp.int32), pltpu.SemaphoreType.DMA((2,))],
        ),
        out_shape=jax.ShapeDtypeStruct((n_rows,) + ROW_TILE, F32),
        compiler_params=_cparams(("arbitrary",)),
        name="moe_experts",
    )(blk_exp, n_used, blk_next, xs, w1, w3, w2)


def _combine_kernel(dest_ref, x_ref, gates_ref, gf_ref, y_ref, o_ref, buf_ref, sem_ref):
    i = pl.program_id(0)
    n = pl.num_programs(0)
    slot = i % 2

    n_tok = n * TM

    def issue(tile, s):
        def body(r, c):
            a = tile * TM + r
            _row_copy(y_ref, dest_ref[a], buf_ref.at[s, 0], r, sem_ref.at[s]).start(priority=0)
            _row_copy(y_ref, dest_ref[n_tok + a], buf_ref.at[s, 1], r, sem_ref.at[s]).start(priority=1)
            return c
        lax.fori_loop(0, TM, body, 0, unroll=8)

    @pl.when(i == 0)
    def _():
        issue(0, 0)

    @pl.when(i + 1 < n)
    def _():
        issue(i + 1, 1 - slot)

    for k in range(2):
        pltpu.make_async_copy(y_ref.at[pl.ds(0, TM)], buf_ref.at[slot, k], sem_ref.at[slot]).wait()

    w0 = gates_ref[:, 0:1]
    w1 = gates_ref[:, 1:2]
    mix = _load_tile_rows(buf_ref.at[slot, 0]) * w0 + _load_tile_rows(buf_ref.at[slot, 1]) * w1
    o_ref[...] = x_ref[...] + gf_ref[0] * mix


def _combine(dest, x2, seq, gates, gf, y):
    t = x2.shape[0]
    return pl.pallas_call(
        _combine_kernel,
        grid_spec=pltpu.PrefetchScalarGridSpec(
            num_scalar_prefetch=1,
            grid=(t // TM,),
            in_specs=[pl.BlockSpec((TM, D_MODEL), lambda i, d: (i, 0)),
                      pl.BlockSpec((TM, LANES), lambda i, d: (i, 0)),
                      pl.BlockSpec((1, 1, D_MODEL), lambda i, d: ((i * TM) // seq, 0, 0)),
                      pl.BlockSpec(memory_space=pl.ANY)],
            out_specs=pl.BlockSpec((TM, D_MODEL), lambda i, d: (i, 0)),
            scratch_shapes=[pltpu.VMEM((2, 2, TM) + ROW_TILE, F32), pltpu.SemaphoreType.DMA((2,))],
        ),
        out_shape=jax.ShapeDtypeStruct((t, D_MODEL), F32),
        compiler_params=_cparams(("arbitrary",)),
        name="moe_combine",
    )(dest, x2, gates, gf, y)


def _moe_layer(x2, seq, g, sc, sh, gf, wr, layer, w1, w3, w2):
    t = x2.shape[0]
    h, ids, gates, cnt = _router(x2, seq, g, sc, sh, wr)
    counts = cnt[0, :MOE_EXPERTS].astype(jnp.int32)
    padded = (counts + MOE_BM - 1) // MOE_BM * MOE_BM
    pends = jnp.cumsum(padded)
    pstarts = pends - padded
    dest = _dest_rows(ids, pstarts)
    n_rows = 2 * t + MOE_EXPERTS * MOE_BM
    nb = n_rows // MOE_BM
    blk_start = jnp.arange(nb, dtype=jnp.int32) * MOE_BM
    blk_exp = jnp.minimum(jnp.sum((pends[None, :] <= blk_start[:, None]).astype(jnp.int32), axis=1), MOE_EXPERTS - 1)
    n_used = (pends[-1:] // MOE_BM).astype(jnp.int32)
    tail = pends[-1] + jnp.arange(MOE_EXPERTS, dtype=jnp.int32) * MOE_BM
    zfill = jnp.concatenate([jnp.where(padded > 0, pends - MOE_BM, -1),
                             jnp.where(tail < n_rows, tail, -1)]).astype(jnp.int32)
    xs = _dispatch(dest, zfill, h, n_rows)
    eid = jnp.arange(MOE_EXPERTS, dtype=jnp.int32)
    later = (padded[None, :] > 0) & (eid[None, :] > eid[:, None])
    next_exp = jnp.min(jnp.where(later, eid[None, :], MOE_EXPERTS), axis=1)
    next_exp = jnp.where(next_exp < MOE_EXPERTS, next_exp, -1).astype(jnp.int32)
    y = _experts(blk_exp, n_used, next_exp[blk_exp], xs, layer, w1, w3, w2)
    return _combine(dest, x2, seq, gates, gf, y)


HG_LT = 512


def _sublane_roll(x, shift):
    return pltpu.roll(x, shift % x.shape[0], 0)


def _hgrn_chunk(hq, hf, hi, lb, state_t):
    c, sub = HG_CHUNK, HG_SUB
    q = _silu(hq)
    f = lb + (1.0 - lb) * _sigmoid(hf)
    lf = jnp.log(f)
    k = 1.0 - f
    v = hi
    row = lax.broadcasted_iota(jnp.int32, (c, c), 0)
    col = lax.broadcasted_iota(jnp.int32, (c, c), 1)
    start = (row // sub) * sub
    tri = jnp.concatenate([(col >= start) & (col <= row), col < start], axis=0).astype(BF16)
    sums = sum(jnp.dot(tri, part, preferred_element_type=F32) for part in _split_bf16(lf, 3))
    bloc = sums[:c]
    rref = sums[c:]
    b = rref + bloc
    qt = q * jnp.exp(bloc)
    ridx = lax.broadcasted_iota(jnp.int32, (c, HG_DK), 0)

    blocks = [jnp.zeros((sub, c), F32)]
    for i in range(1, c // sub):
        r_i = rref[i * sub:i * sub + 1, :]
        live = ridx < i * sub
        k_i = jnp.where(live, k * jnp.exp(jnp.where(live, r_i - b, 0.0)), 0.0)
        blocks.append(lax.dot_general(qt[i * sub:(i + 1) * sub].astype(BF16), k_i.astype(BF16),
                                      (((1,), (1,)), ((), ())), preferred_element_type=F32))
    attn_off = jnp.concatenate(blocks, axis=0)
    o = jnp.dot(attn_off.astype(BF16), v.astype(BF16), preferred_element_type=F32)

    pos = ridx % sub
    for d in range(sub):
        qd = q if d == 0 else _sublane_roll(q, -d)
        bd = bloc if d == 0 else _sublane_roll(bloc, -d)
        ok = pos + d < sub
        e = qd * k * jnp.exp(jnp.where(ok, bd - bloc, -jnp.inf))
        w = jnp.sum(e, axis=-1, keepdims=True) * v
        o = o + (w if d == 0 else _sublane_roll(w, d))

    qe = qt * jnp.exp(rref)
    o = o + lax.dot_general(qe.astype(BF16), state_t.astype(BF16), (((1,), (1,)), ((), ())),
                            preferred_element_type=F32)
    b_end = b[c - 1:c, :]
    kd = k * jnp.exp(b_end - b)
    new_state = state_t * jnp.exp(b_end) + lax.dot_general(v.astype(BF16), kd.astype(BF16), (((0,), (0,)), ((), ())),
                                                           preferred_element_type=F32)
    return o, new_state


def _hgrn_kernel(hq_ref, hf_ref, hi_ref, hg_ref, lb_ref, og_ref, o_ref, state_ref):
    @pl.when(pl.program_id(0) == 0)
    def _():
        state_ref[...] = jnp.zeros_like(state_ref)

    og = og_ref[...]

    def body(ci, carry):
        rows = pl.ds(pl.multiple_of(ci * HG_CHUNK, HG_CHUNK), HG_CHUNK)
        for b in range(hq_ref.shape[0]):
            for h in range(HG_HEADS):
                cs = slice(h * HG_DK, (h + 1) * HG_DK)
                o, st = _hgrn_chunk(hq_ref[b, rows, cs], hf_ref[b, rows, cs], hi_ref[b, rows, cs], lb_ref[:, cs],
                                    state_ref[b, h])
                state_ref[b, h] = st
                on = o * lax.rsqrt(jnp.mean(o * o, axis=-1, keepdims=True) + EPS) * og
                o_ref[b, rows, cs] = on * _silu(hg_ref[b, rows, cs])
        return carry
    lax.fori_loop(0, HG_LT // HG_CHUNK, body, 0)


def _hgrn(z, batch, seq, lb, o_gain):
    t = z.shape[0]
    z3 = z.reshape(batch, seq, z.shape[1])

    def col(base):
        return pl.BlockSpec((batch, HG_LT, HG_W), lambda l: (0, l, base // HG_W))

    out = pl.pallas_call(
        _hgrn_kernel,
        grid=(seq // HG_LT,),
        in_specs=[col(Z_HQ), col(Z_HF), col(Z_HI), col(Z_HG),
                  pl.BlockSpec((1, HG_W), lambda l: (0, 0)),
                  pl.BlockSpec((1, HG_DV), lambda l: (0, 0))],
        out_specs=pl.BlockSpec((batch, HG_LT, HG_W), lambda l: (0, l, 0)),
        out_shape=jax.ShapeDtypeStruct((batch, seq, HG_W), F32),
        scratch_shapes=[pltpu.VMEM((batch, HG_HEADS, HG_DV, HG_DK), F32)],
        compiler_params=_cparams(("arbitrary",)),
        name="hgrn2",
    )(z3, z3, z3, z3, lb.reshape(1, -1), o_gain.reshape(1, -1))
    return out.reshape(t, HG_W)


def _rms64(v, gain):
    return v * lax.rsqrt(jnp.mean(v * v, axis=-1, keepdims=True) + EPS) * gain


def _nsa_prep_kernel(kc_ref, vc_ref, ks_ref, vs_ref, kw_ref, vw_ref, kg_ref,
                     kcr_ref, vcr_ref, ksn_ref, vsb_ref, kwn_ref, vwb_ref):
    ones_col = (lax.broadcasted_iota(jnp.int32, (TM, NSA_VW - NSA_HD), 1) == 0).astype(F32)

    def with_ones(v):
        return jnp.concatenate([v, ones_col], axis=1).astype(BF16)

    for g in range(NSA_KV_GROUPS):
        cs = slice(g * NSA_HD, (g + 1) * NSA_HD)
        kcr_ref[0, g] = kc_ref[:, cs]
        vcr_ref[0, g] = vc_ref[:, cs]
        ksn_ref[0, g] = _rms64(ks_ref[:, cs], kg_ref[1:2, :]).astype(BF16)
        vsb_ref[0, g] = with_ones(vs_ref[:, cs])
        kwn_ref[0, g] = _rms64(kw_ref[:, cs], kg_ref[2:3, :]).astype(BF16)
        vwb_ref[0, g] = with_ones(vw_ref[:, cs])


def _nsa_prep(z, batch, seq, k_gain):
    per = seq // TM

    def col(base):
        return pl.BlockSpec((TM, NSA_KV_W), lambda i: (i, base // NSA_KV_W))

    def out(dtype, width=NSA_HD):
        return (pl.BlockSpec((1, NSA_KV_GROUPS, TM, width), lambda i: (i // per, 0, i % per, 0)),
                jax.ShapeDtypeStruct((batch, NSA_KV_GROUPS, seq, width), dtype))

    outs = [out(F32), out(F32), out(BF16), out(BF16, NSA_VW), out(BF16), out(BF16, NSA_VW)]
    return pl.pallas_call(
        _nsa_prep_kernel,
        grid=(batch * per,),
        in_specs=[col(Z_KC), col(Z_VC), col(Z_KS), col(Z_VS), col(Z_KW), col(Z_VW), _full_spec((3, NSA_HD))],
        out_specs=[o[0] for o in outs],
        out_shape=[o[1] for o in outs],
        compiler_params=_cparams(("arbitrary",)),
        name="nsa_prep",
    )(z, z, z, z, z, z, k_gain)


def _nsa_compress_kernel(ak_ref, av_ref, wk_ref, wv_ref, pe_ref, kg_ref, kc_ref, vc_ref):
    half = CMP_STRIDE * NSA_HD

    def compress(a, w_ref):
        top = jnp.dot(a, w_ref[:half, :], precision=HIGHEST, preferred_element_type=F32)
        bot = jnp.dot(a, w_ref[half:, :], precision=HIGHEST, preferred_element_type=F32)
        pe = jnp.dot(pe_ref[...], w_ref[...], precision=HIGHEST, preferred_element_type=F32)
        return top + _sublane_roll(bot, -1) + pe[0:1]

    kc_ref[0, 0] = _rms64(compress(ak_ref[0, 0], wk_ref), kg_ref[0:1, :])
    vc_ref[0, 0] = compress(av_ref[0, 0], wv_ref)


def _nsa_compress(kcr, vcr, w_ck, w_cv, cmp_pe, k_gain):
    batch, groups, seq, _ = kcr.shape
    nc = seq // CMP_STRIDE
    wide = CMP_STRIDE * NSA_HD
    spec = pl.BlockSpec((1, 1, nc, wide), lambda b, g: (b, g, 0, 0))
    ospec = pl.BlockSpec((1, 1, nc, NSA_HD), lambda b, g: (b, g, 0, 0))
    oshape = jax.ShapeDtypeStruct((batch, groups, nc, NSA_HD), F32)
    full2 = lambda shape: pl.BlockSpec(shape, lambda b, g: (0, 0))
    return pl.pallas_call(
        _nsa_compress_kernel,
        grid=(batch, groups),
        in_specs=[spec, spec, full2((2 * wide, NSA_HD)), full2((2 * wide, NSA_HD)), full2((8, 2 * wide)),
                  full2((3, NSA_HD))],
        out_specs=[ospec, ospec],
        out_shape=[oshape, oshape],
        compiler_params=_cparams(("arbitrary", "arbitrary")),
        name="nsa_compress",
    )(kcr.reshape(batch, groups, nc, wide), vcr.reshape(batch, groups, nc, wide), w_ck, w_cv,
      jnp.broadcast_to(cmp_pe.reshape(1, 2 * wide), (8, 2 * wide)), k_gain)


def _nsa_kernel(seq, kchunk, q_ref, gate_ref, kc_ref, vc_ref, ks_ref, vs_ref, kw_ref, vw_ref, qg_ref, agg_ref,
                o_ref):
    qi = pl.program_id(2)
    t0 = qi * Q_BLOCK
    jn = NSA_QPG
    nc = seq // CMP_STRIDE
    n_slc = seq // SLC_BLOCK
    n_sel = min(N_SELECT, n_slc)
    nt = (((1,), (1,)), ((), ()))

    qraw = q_ref[...]
    qs = [_rms64(qraw[:, j * NSA_HD:(j + 1) * NSA_HD], qg_ref[...]) * (NSA_HD ** -0.5) for j in range(jn)]
    qb = [q.astype(BF16) for q in qs]

    def pos_of(shape):
        return t0 + lax.broadcasted_iota(jnp.int32, shape, 0)

    cmp_end = lax.broadcasted_iota(jnp.int32, (Q_BLOCK, nc), 1) * CMP_STRIDE + (CMP_BLOCK - 1)
    valid_c = cmp_end <= pos_of((Q_BLOCK, nc))
    sees_any = pos_of((Q_BLOCK, 1)) >= CMP_BLOCK - 1
    kc_hi, kc_lo = _split_bf16(kc_ref[0, 0], 2)
    vcb = vc_ref[0, 0].astype(BF16)
    o_c = []
    imp = None
    for j in range(jn):
        qh, ql = _split_bf16(qs[j], 2)
        s = (lax.dot_general(qh, kc_hi, nt, preferred_element_type=F32)
             + (lax.dot_general(qh, kc_lo, nt, preferred_element_type=F32)
                + lax.dot_general(ql, kc_hi, nt, preferred_element_type=F32)))
        sm = jnp.where(valid_c, s, NEG)
        e = jnp.exp(sm - jnp.max(sm, axis=-1, keepdims=True))
        p = e * jnp.where(sees_any, 1.0 / jnp.sum(e, axis=-1, keepdims=True), 0.0)
        o_c.append(jnp.dot(p.astype(BF16), vcb, preferred_element_type=F32))
        imp = p if imp is None else imp + p

    agg = agg_ref[...].astype(BF16)
    imp_s = sum(jnp.dot(part, agg, preferred_element_type=F32) for part in _split_bf16(imp, 3))
    ids = lax.broadcasted_iota(jnp.int32, (Q_BLOCK, n_slc), 1)
    q_blk = pos_of((Q_BLOCK, n_slc)) // SLC_BLOCK
    forced = (ids == 0) | (ids == q_blk) | (ids == q_blk - 1)
    score_t = jnp.where(ids > q_blk, NEG, jnp.where(forced, FORCE, imp_s)).T
    ids_t = lax.broadcasted_iota(jnp.int32, (n_slc, Q_BLOCK), 0).astype(F32)
    cur = score_t
    sel_t = jnp.zeros((n_slc, Q_BLOCK), F32)
    for _ in range(n_sel):
        m = jnp.max(cur, axis=0, keepdims=True)
        first = jnp.min(jnp.where(cur == m, ids_t, float(n_slc)), axis=0, keepdims=True)
        hit = ids_t == first
        sel_t = jnp.where(hit, 1.0, sel_t)
        cur = jnp.where(hit, -jnp.inf, cur)
    sel = jnp.where(score_t > NEG / 2, sel_t, 0.0).T.astype(BF16)

    span = Q_BLOCK + WINDOW
    start = pl.multiple_of(jnp.maximum(t0 - WINDOW, 0), Q_BLOCK)
    wsl = pl.ds(start, span)
    kpos = start + lax.broadcasted_iota(jnp.int32, (Q_BLOCK, span), 1)
    wpos = pos_of((Q_BLOCK, span))
    bias_w = jnp.where((kpos <= wpos) & (kpos > wpos - WINDOW), 0.0, NEG)
    kw = kw_ref[0, 0, wsl, :]
    vw = vw_ref[0, 0, wsl, :]
    o_w = []
    for j in range(jn):
        s = lax.dot_general(qb[j], kw, nt, preferred_element_type=F32) + bias_w
        e = jnp.exp((s - jnp.max(s, axis=-1, keepdims=True)).astype(BF16))
        ow = jnp.dot(e, vw, preferred_element_type=F32)
        o_w.append(ow[:, :NSA_HD] / ow[:, NSA_HD:NSA_HD + 1])

    per_chunk = kchunk // SLC_BLOCK
    blk_of_key = lax.broadcasted_iota(jnp.int32, (n_slc, kchunk), 1) // SLC_BLOCK
    blk_row = lax.broadcasted_iota(jnp.int32, (n_slc, kchunk), 0)
    key_lane = lax.broadcasted_iota(jnp.int32, (Q_BLOCK, kchunk), 1)
    qpos = pos_of((Q_BLOCK, kchunk))

    def sel_body(c, carry):
        ksl = pl.ds(pl.multiple_of(c * kchunk, kchunk), kchunk)
        k_c = ks_ref[0, 0, ksl, :]
        v_c = vs_ref[0, 0, ksl, :]
        expand = (blk_row == blk_of_key + c * per_chunk).astype(BF16)
        chosen = jnp.dot(sel, expand, preferred_element_type=F32)
        bias = jnp.where((chosen > 0.5) & (key_lane + c * kchunk <= qpos), 0.0, NEG)
        m_run, acc = carry
        s = lax.dot_general(qb_all, k_c, nt, preferred_element_type=F32) + jnp.concatenate([bias] * jn, axis=0)
        m_new = jnp.maximum(m_run, jnp.max(s, axis=-1, keepdims=True))
        alpha = jnp.exp(m_run - m_new)
        p = jnp.exp((s - m_new).astype(BF16))
        acc_new = acc * alpha + jnp.dot(p, v_c, preferred_element_type=F32)
        return m_new, acc_new

    rows = jn * Q_BLOCK
    qb_all = jnp.concatenate(qb, axis=0)
    n_chunks = (t0 + Q_BLOCK + kchunk - 1) // kchunk
    init = (jnp.full((rows, 1), NEG, F32), jnp.zeros((rows, NSA_VW), F32))
    _, acc_fin = lax.fori_loop(0, n_chunks, sel_body, init)
    o_sel = acc_fin[:, :NSA_HD] / acc_fin[:, NSA_HD:NSA_HD + 1]

    gates = _sigmoid(gate_ref[...])
    outs = []
    for j in range(jn):
        o_s = o_sel[j * Q_BLOCK:(j + 1) * Q_BLOCK]
        outs.append(gates[:, 3 * j:3 * j + 1] * o_c[j] + gates[:, 3 * j + 1:3 * j + 2] * o_s
                    + gates[:, 3 * j + 2:3 * j + 3] * o_w[j])
    o_ref[...] = jnp.concatenate(outs, axis=1)


def _selection_weights(seq):
    nc = seq // CMP_STRIDE
    n_slc = seq // SLC_BLOCK
    ratio = SLC_BLOCK // CMP_STRIDE
    n = jnp.arange(nc)[:, None]
    j = jnp.arange(n_slc)[None, :]
    o = n - ratio * j + (CMP_BLOCK // CMP_STRIDE - 1)
    w = jnp.asarray(SEL_AGG_W, F32)
    return jnp.where((o >= 0) & (o < len(SEL_AGG_W)), w[jnp.clip(o, 0, len(SEL_AGG_W) - 1)], 0.0)


def _nsa(z, batch, seq, w_ck, w_cv, cmp_pe, q_gain, k_gain):
    t = z.shape[0]
    kcr, vcr, ksn, vsb, kwn, vwb = _nsa_prep(z, batch, seq, k_gain)
    kc, vc = _nsa_compress(kcr, vcr, w_ck, w_cv, cmp_pe, k_gain)
    nq = seq // Q_BLOCK
    nc = seq // CMP_STRIDE
    n_slc = seq // SLC_BLOCK
    kchunk = min(NSA_KCHUNK, seq)
    qw = NSA_QPG * NSA_HD
    cmp_spec = pl.BlockSpec((1, 1, nc, NSA_HD), lambda b, g, i: (b, g, 0, 0))
    key_spec = pl.BlockSpec((1, 1, seq, NSA_HD), lambda b, g, i: (b, g, 0, 0))
    val_spec = pl.BlockSpec((1, 1, seq, NSA_VW), lambda b, g, i: (b, g, 0, 0))
    return pl.pallas_call(
        functools.partial(_nsa_kernel, seq, kchunk),
        grid=(batch, NSA_KV_GROUPS, nq),
        in_specs=[pl.BlockSpec((Q_BLOCK, qw), lambda b, g, i: (b * nq + i, g)),
                  pl.BlockSpec((Q_BLOCK, LANES), lambda b, g, i: (b * nq + i, Z_GATE // LANES + g)),
                  cmp_spec, cmp_spec, key_spec, val_spec, key_spec, val_spec,
                  pl.BlockSpec((1, NSA_HD), lambda b, g, i: (0, 0)),
                  pl.BlockSpec((nc, n_slc), lambda b, g, i: (0, 0))],
        out_specs=pl.BlockSpec((Q_BLOCK, qw), lambda b, g, i: (b * nq + i, g)),
        out_shape=jax.ShapeDtypeStruct((t, NSA_W), F32),
        compiler_params=_cparams(("arbitrary", "arbitrary", "arbitrary")),
        name="nsa_attention",
    )(z, z, kc, vc, ksn, vsb, kwn, vwb, q_gain.reshape(1, -1), _selection_weights(seq))


def _pad_in_proj(w_in):
    per_group = NSA_QPG * 3
    gate_lo = Z_GATE
    out = jnp.zeros((D_MODEL, Z_N), F32)
    out = out.at[:, :gate_lo].set(w_in[:, :gate_lo])
    for g in range(NSA_KV_GROUPS):
        out = out.at[:, gate_lo + g * LANES:gate_lo + g * LANES + per_group].set(
            w_in[:, gate_lo + g * per_group:gate_lo + (g + 1) * per_group])
    return out.at[:, Z_HQ:].set(w_in[:, gate_lo + NSA_HEADS * 3:])


def _even_mixer(x2, batch, seq, g, sc, sh, gm, w_in, w_out, w_ck, w_cv, cmp_pe, q_gain, k_gain, lb, o_gain):
    z = _in_proj(x2, seq, g, sc, sh, _pad_in_proj(w_in).astype(BF16))
    r = _hgrn(z, batch, seq, lb, o_gain)
    a = _nsa(z, batch, seq, w_ck, w_cv, cmp_pe, q_gain, k_gain)
    return _out_proj(x2, seq, a, r, gm, w_out.astype(BF16))


def _odd_mixer(x2, seq, g, sc, sh, gm, w_pw1, dw, dw_b, ln_g, ln_b, w_pw2):
    dw_pad = jnp.zeros((CONV_HALO, D_MODEL), F32).at[:CONV_W].set(dw)
    return _conv_module(x2, seq, g, sc, sh, w_pw1.astype(BF16), dw_pad, dw_b.reshape(1, -1), ln_g.reshape(1, -1),
                        ln_b.reshape(1, -1), gm, w_pw2.astype(BF16))


def _router_weights(w_group, w_expert):
    wr = jnp.zeros((D_MODEL, LANES), F32)
    return wr.at[:, :MOE_GROUPS].set(w_group).at[:, MOE_GROUPS:MOE_GROUPS + MOE_EXPERTS].set(w_expert)


def _lower_bounds_kernel(l_ref, o_ref):
    logits = l_ref[...]
    e = jnp.exp(logits - jnp.max(logits, axis=0, keepdims=True))
    p = e / jnp.sum(e, axis=0, keepdims=True)
    n = logits.shape[0]
    acc = jnp.zeros_like(p[0:1])
    for i in range(n):
        acc = acc + p[i:i + 1]
        o_ref[i:i + 1, :] = acc - p[0:1]


def _lower_bounds(lb_logits):
    return pl.pallas_call(
        _lower_bounds_kernel,
        out_shape=jax.ShapeDtypeStruct(lb_logits.shape, F32),
        name="hgrn_lower_bounds",
    )(lb_logits)


def kernel(x, c, ada_w, ada_b, norm_mix, norm_ffn, mix_w_in, mix_w_out, nsa_cmp_wk, nsa_cmp_wv, nsa_cmp_pe, nsa_q_gain, nsa_k_gain, hgrn_lb_logits, hgrn_o_gain, conv_w_pw1, conv_dw, conv_dw_b, conv_ln_g, conv_ln_b, conv_w_pw2, moe_w_group, moe_w_expert, moe_w1, moe_w3, moe_w2):
    batch, seq, d = x.shape
    x2 = x.reshape(batch * seq, d)
    mod = _modulation(c, ada_w, ada_b)
    lower_bounds = _lower_bounds(hgrn_lb_logits)
    for layer in range(DEPTH):
        sh_m, sc_m, g_m, sh_f, sc_f, g_f = (mod[layer, :, k] for k in range(6))
        i = layer // 2
        gain = norm_mix[layer].reshape(1, d)
        if layer % 2 == 0:
            x2 = _even_mixer(x2, batch, seq, gain, sc_m, sh_m, g_m, mix_w_in[i], mix_w_out[i], nsa_cmp_wk[i],
                             nsa_cmp_wv[i], nsa_cmp_pe[i], nsa_q_gain[i], nsa_k_gain[i], lower_bounds[i],
                             hgrn_o_gain[i])
        else:
            x2 = _odd_mixer(x2, seq, gain, sc_m, sh_m, g_m, conv_w_pw1[i], conv_dw[i], conv_dw_b[i], conv_ln_g[i],
                            conv_ln_b[i], conv_w_pw2[i])
        x2 = _moe_layer(x2, seq, norm_ffn[layer].reshape(1, d), sc_f, sh_f, g_f,
                        _router_weights(moe_w_group[layer], moe_w_expert[layer]), layer, moe_w1, moe_w3, moe_w2)
    return x2.reshape(batch, seq, d)
```

```python
import functools

import jax
import jax.numpy as jnp
from jax import lax
from jax.experimental import pallas as pl
from jax.experimental.pallas import tpu as pltpu

F32 = jnp.float32
BF16 = jnp.bfloat16
HIGHEST = lax.Precision.HIGHEST

D_MODEL = 1024
DEPTH = 4

NSA_HEADS = 8
NSA_KV_GROUPS = 2
NSA_QPG = NSA_HEADS // NSA_KV_GROUPS
NSA_HD = 64
NSA_W = NSA_HEADS * NSA_HD
NSA_KV_W = NSA_KV_GROUPS * NSA_HD
NSA_KCHUNK = 512
NSA_VW = 128
CMP_BLOCK = 32
CMP_STRIDE = 16
SLC_BLOCK = 64
N_SELECT = 16
WINDOW = 512
Q_BLOCK = 256
SEL_AGG_W = (1.0, 2.0, 2.0, 2.0, 1.0)

HG_HEADS = 4
HG_DK = 128
HG_DV = 128
HG_W = HG_HEADS * HG_DV
HG_CHUNK = 64
HG_SUB = 8

CONV_W = 31
MOE_GROUPS = 4
MOE_EPG = 8
MOE_EXPERTS = MOE_GROUPS * MOE_EPG
EXPERT_FF = 512

EPS = 1e-6
NEG = -1e30
FORCE = 1e4

LANES = 128
VMEM_LIMIT = 56 * 1024 * 1024

Z_Q = 0
Z_KC = 512
Z_VC = 640
Z_KS = 768
Z_VS = 896
Z_KW = 1024
Z_VW = 1152
Z_GATE = 1280
Z_HQ = Z_GATE + NSA_KV_GROUPS * LANES
Z_HF = Z_HQ + HG_W
Z_HI = Z_HF + HG_W
Z_HG = Z_HI + HG_W
Z_N = Z_HG + HG_W

TM = 512
MOE_BM = 512


def _cparams(sem):
    return pltpu.CompilerParams(dimension_semantics=sem, vmem_limit_bytes=VMEM_LIMIT)


def _split_bf16(x, terms):
    parts = []
    for _ in range(terms):
        p = x.astype(BF16)
        parts.append(p)
        x = x - p.astype(F32)
    return parts


def _dot_bf16x3(a, b, dims):
    ah, al = _split_bf16(a, 2)
    bh, bl = _split_bf16(b, 2)

    def d(x, y):
        return lax.dot_general(x, y, dims, preferred_element_type=F32)
    return d(ah, bh) + (d(ah, bl) + d(al, bh))


def _sigmoid(x):
    return 1.0 / (1.0 + jnp.exp(-x))


def _silu(x):
    return x * _sigmoid(x)


def _mod_kernel(c_ref, w_ref, b_ref, o_ref):
    c = c_ref[...]
    o_ref[0] = jnp.dot(_silu(c), w_ref[0], precision=HIGHEST, preferred_element_type=F32) + b_ref[0]


def _modulation(c, ada_w, ada_b):
    b = c.shape[0]
    rows = 8
    c_pad = jnp.zeros((rows, D_MODEL), F32).at[:b].set(c)
    n = 6 * D_MODEL
    tn = 1536
    out = pl.pallas_call(
        _mod_kernel,
        grid=(DEPTH, n // tn),
        in_specs=[
            pl.BlockSpec((rows, D_MODEL), lambda l, j: (0, 0)),
            pl.BlockSpec((1, D_MODEL, tn), lambda l, j: (l, 0, j)),
            pl.BlockSpec((1, 1, tn), lambda l, j: (l, 0, j)),
        ],
        out_specs=pl.BlockSpec((1, rows, tn), lambda l, j: (l, 0, j)),
        out_shape=jax.ShapeDtypeStruct((DEPTH, rows, n), F32),
        compiler_params=_cparams(("arbitrary", "arbitrary")),
        name="adaln_mod",
    )(c_pad, ada_w, ada_b.reshape(DEPTH, 1, n))
    return out[:, :b].reshape(DEPTH, b, 6, 1, D_MODEL)


def _norm_mod(x, g, sc, sh):
    y = x * lax.rsqrt(jnp.mean(x * x, axis=-1, keepdims=True) + EPS)
    return (y * g) * (1.0 + sc) + sh


PROJ_TM = 512


def _row_spec(width, tm=TM):
    return pl.BlockSpec((tm, width), lambda i: (i, 0))


def _full_spec(shape):
    nd = len(shape)
    return pl.BlockSpec(shape, lambda i: (0,) * nd)


def _batch_spec(seq, tm=TM):
    return pl.BlockSpec((1, 1, D_MODEL), lambda i: ((i * tm) // seq, 0, 0))


def _in_proj_kernel(x_ref, g_ref, sc_ref, sh_ref, w_ref, o_ref):
    h = _norm_mod(x_ref[...], g_ref[...], sc_ref[0], sh_ref[0]).astype(BF16)
    tn = Z_N // 4
    for j in range(4):
        o_ref[:, j * tn:(j + 1) * tn] = jnp.dot(h, w_ref[:, j * tn:(j + 1) * tn], preferred_element_type=F32)


def _in_proj(x2, seq, g, sc, sh, w_bf):
    t = x2.shape[0]
    return pl.pallas_call(
        _in_proj_kernel,
        grid=(t // PROJ_TM,),
        in_specs=[_row_spec(D_MODEL, PROJ_TM), _full_spec((1, D_MODEL)), _batch_spec(seq, PROJ_TM),
                  _batch_spec(seq, PROJ_TM), _full_spec((D_MODEL, Z_N))],
        out_specs=_row_spec(Z_N, PROJ_TM),
        out_shape=jax.ShapeDtypeStruct((t, Z_N), F32),
        compiler_params=_cparams(("arbitrary",)),
        name="in_proj",
    )(x2, g, sc, sh, w_bf)


def _out_proj_kernel(x_ref, a_ref, r_ref, gm_ref, w_ref, o_ref):
    y = jnp.dot(a_ref[...].astype(BF16), w_ref[:NSA_W, :], preferred_element_type=F32)
    y = y + jnp.dot(r_ref[...].astype(BF16), w_ref[NSA_W:, :], preferred_element_type=F32)
    o_ref[...] = x_ref[...] + gm_ref[0] * y


def _out_proj(x2, seq, a, r, gm, w_bf):
    t = x2.shape[0]
    return pl.pallas_call(
        _out_proj_kernel,
        grid=(t // PROJ_TM,),
        in_specs=[_row_spec(D_MODEL, PROJ_TM), _row_spec(NSA_W, PROJ_TM), _row_spec(HG_W, PROJ_TM),
                  _batch_spec(seq, PROJ_TM), _full_spec((NSA_W + HG_W, D_MODEL))],
        out_specs=_row_spec(D_MODEL, PROJ_TM),
        out_shape=jax.ShapeDtypeStruct((t, D_MODEL), F32),
        compiler_params=_cparams(("arbitrary",)),
        name="out_proj",
    )(x2, a, r, gm, w_bf)


CONV_HALO = 32
CONV_RB = 64
CONV_CW = 256
SUBLANES = 8
CONV_SHIFT_ROWS = TM + CONV_HALO - SUBLANES


def _conv_kernel(seq, x_ref, xh_ref, g_ref, sc_ref, sh_ref, w1_ref, dw_ref, dwb_ref, lng_ref, lnb_ref, gm_ref,
                 w2_ref, o_ref, ext_ref, shift_ref, act_ref):
    i = pl.program_id(0)
    first = (i * TM) % seq == 0
    x_ext = jnp.concatenate([xh_ref[...], x_ref[...]], axis=0)
    h = _norm_mod(x_ext, g_ref[...], sc_ref[0], sh_ref[0]).astype(BF16)
    in_halo = lax.broadcasted_iota(jnp.int32, (CONV_HALO + TM, CONV_CW), 0) < CONV_HALO
    off = CONV_HALO - (CONV_W - 1)
    for cb in range(D_MODEL // CONV_CW):
        cw = slice(cb * CONV_CW, (cb + 1) * CONV_CW)
        a = jnp.dot(h, w1_ref[:, cw], preferred_element_type=F32)
        gate = jnp.dot(h, w1_ref[:, D_MODEL + cb * CONV_CW:D_MODEL + (cb + 1) * CONV_CW], preferred_element_type=F32)
        ext_ref[:, cw] = jnp.where(in_halo & first, 0.0, a * _sigmoid(gate))
        for s in range(1, SUBLANES):
            shift_ref[s - 1, :, cw] = ext_ref[s:s + CONV_SHIFT_ROWS, cw]
        for r in range(TM // CONV_RB):
            for c in range(cb * CONV_CW // LANES, (cb + 1) * CONV_CW // LANES):
                cs = slice(c * LANES, (c + 1) * LANES)
                acc = jnp.zeros((CONV_RB, LANES), F32)
                for k in range(CONV_W):
                    s = (off + k) % SUBLANES
                    lo = r * CONV_RB + off + k - s
                    win = ext_ref[lo:lo + CONV_RB, cs] if s == 0 else shift_ref[s - 1, lo:lo + CONV_RB, cs]
                    acc = acc + win * dw_ref[k:k + 1, cs]
                act_ref[r * CONV_RB:(r + 1) * CONV_RB, cs] = acc + dwb_ref[:, cs]
    v = act_ref[...]
    mu = jnp.mean(v, axis=-1, keepdims=True)
    var = jnp.mean(jnp.square(v - mu), axis=-1, keepdims=True)
    y = (v - mu) * lax.rsqrt(var + EPS) * lng_ref[...] + lnb_ref[...]
    y = _silu(y).astype(BF16)
    o_ref[...] = x_ref[...] + gm_ref[0] * jnp.dot(y, w2_ref[...], preferred_element_type=F32)


def _conv_module(x2, seq, g, sc, sh, w1_bf, dw, dwb, lng, lnb, gm, w2_bf):
    t = x2.shape[0]
    per = TM // CONV_HALO
    return pl.pallas_call(
        functools.partial(_conv_kernel, seq),
        grid=(t // TM,),
        in_specs=[_row_spec(D_MODEL),
                  pl.BlockSpec((CONV_HALO, D_MODEL), lambda i: (jnp.maximum(i * per - 1, 0), 0)),
                  _full_spec((1, D_MODEL)), _batch_spec(seq), _batch_spec(seq), _full_spec((D_MODEL, 2 * D_MODEL)),
                  _full_spec((CONV_HALO, D_MODEL)), _full_spec((1, D_MODEL)), _full_spec((1, D_MODEL)),
                  _full_spec((1, D_MODEL)), _batch_spec(seq), _full_spec((D_MODEL, D_MODEL))],
        out_specs=_row_spec(D_MODEL),
        out_shape=jax.ShapeDtypeStruct((t, D_MODEL), F32),
        scratch_shapes=[pltpu.VMEM((TM + CONV_HALO, D_MODEL), F32),
                        pltpu.VMEM((SUBLANES - 1, CONV_SHIFT_ROWS, D_MODEL), F32),
                        pltpu.VMEM((TM, D_MODEL), F32)],
        compiler_params=_cparams(("arbitrary",)),
        name="conv_module",
    )(x2, x2, g, sc, sh, w1_bf, dw, dwb, lng, lnb, gm, w2_bf)


ROW_TILE = (SUBLANES, LANES)


def _store_tile_rows(ref, v):
    ref[...] = v.reshape((v.shape[0],) + ROW_TILE)


def _load_tile_rows(ref):
    return ref[...].reshape(ref.shape[0], D_MODEL)


def _first_index_of_max(v, lane):
    m = jnp.max(v, axis=-1, keepdims=True)
    idx = jnp.min(jnp.where(v == m, lane, float(LANES)), axis=-1, keepdims=True)
    return m, idx


def _router_kernel(x_ref, g_ref, sc_ref, sh_ref, wr_ref, h_ref, ids_ref, gates_ref, cnt_ref, carry_ref):
    i = pl.program_id(0)

    @pl.when(i == 0)
    def _():
        carry_ref[...] = jnp.zeros_like(carry_ref)

    h = _norm_mod(x_ref[...], g_ref[...], sc_ref[0], sh_ref[0])
    _store_tile_rows(h_ref, h)
    logits = _dot_bf16x3(h, wr_ref[...], (((1,), (0,)), ((), ())))
    lane_i = lax.broadcasted_iota(jnp.int32, logits.shape, 1)
    lane = lane_i.astype(F32)
    gl = jnp.where(lane < MOE_GROUPS, logits, -jnp.inf)
    gmax, gsel = _first_index_of_max(gl, lane)
    g_gate = 1.0 / jnp.sum(jnp.exp(gl - gmax), axis=-1, keepdims=True)
    lo = MOE_GROUPS + MOE_EPG * gsel
    in_grp = (lane >= lo) & (lane < lo + MOE_EPG)
    el = jnp.where(in_grp, logits, -jnp.inf)
    emax = jnp.max(el, axis=-1, keepdims=True)
    p = jnp.exp(el - emax)
    prob = p / jnp.sum(p, axis=-1, keepdims=True)
    cand = jnp.where(in_grp, prob, -1.0)
    v1, i1 = _first_index_of_max(cand, lane)
    cand2 = jnp.where(lane == i1, -1.0, cand)
    v2, i2 = _first_index_of_max(cand2, lane)
    den = v1 + v2
    w1 = g_gate * v1 / den
    w2 = g_gate * v2 / den
    e1 = i1 - MOE_GROUPS
    e2 = i2 - MOE_GROUPS
    oh1 = (lane == e1).astype(F32)
    oh2 = (lane == e2).astype(F32)
    rows = lax.broadcasted_iota(jnp.int32, (TM, TM), 0)
    cols = lax.broadcasted_iota(jnp.int32, (TM, TM), 1)
    before = (cols < rows).astype(BF16)
    prior = jnp.dot(before, (oh1 + oh2).astype(BF16), preferred_element_type=F32) + carry_ref[...]
    r1 = jnp.sum(oh1 * prior, axis=-1, keepdims=True)
    r2 = jnp.sum(oh2 * prior, axis=-1, keepdims=True)
    carry_ref[...] = carry_ref[...] + jnp.sum(oh1 + oh2, axis=0, keepdims=True)
    cnt_ref[...] = jnp.broadcast_to(carry_ref[...], cnt_ref.shape)
    packed = jnp.where(lane_i == 0, e1, jnp.where(lane_i == 1, e2, jnp.where(lane_i == 2, r1, jnp.where(lane_i == 3, r2, 0.0))))
    ids_ref[...] = packed.astype(jnp.int32)
    gates_ref[...] = jnp.where(lane_i == 0, w1, jnp.where(lane_i == 1, w2, 0.0))


def _router(x2, seq, g, sc, sh, wr):
    t = x2.shape[0]
    return pl.pallas_call(
        _router_kernel,
        grid=(t // TM,),
        in_specs=[_row_spec(D_MODEL), _full_spec((1, D_MODEL)), _batch_spec(seq), _batch_spec(seq),
                  _full_spec((D_MODEL, LANES))],
        out_specs=[pl.BlockSpec((TM,) + ROW_TILE, lambda i: (i, 0, 0)), _row_spec(LANES), _row_spec(LANES),
                   _full_spec((8, LANES))],
        out_shape=[jax.ShapeDtypeStruct((t,) + ROW_TILE, F32), jax.ShapeDtypeStruct((t, LANES), jnp.int32),
                   jax.ShapeDtypeStruct((t, LANES), F32), jax.ShapeDtypeStruct((8, LANES), F32)],
        scratch_shapes=[pltpu.VMEM((1, LANES), F32)],
        compiler_params=_cparams(("arbitrary",)),
        name="moe_router",
    )(x2, g, sc, sh, wr)


def _row_copy(src_ref, src_row, dst_ref, dst_row, sem):
    return pltpu.make_async_copy(src_ref.at[src_row], dst_ref.at[dst_row], sem)


def _dispatch_kernel(dest_ref, zfill_ref, h_ref, xs_ref, buf_ref, zero_ref, sem_ref, zsem_ref):
    i = pl.program_id(0)
    n = pl.num_programs(0)
    slot = i % 2

    def zero_copy(k):
        return pltpu.make_async_copy(zero_ref, xs_ref.at[pl.ds(zfill_ref[k], MOE_BM)], zsem_ref.at[0])

    @pl.when(i == 0)
    def _():
        zero_ref[...] = jnp.zeros_like(zero_ref)

        def start(k, c):
            @pl.when(zfill_ref[k] >= 0)
            def _():
                zero_copy(k).start()
            return c
        lax.fori_loop(0, 2 * MOE_EXPERTS, start, 0)

        def drain(k, c):
            @pl.when(zfill_ref[k] >= 0)
            def _():
                zero_copy(k).wait()
            return c
        lax.fori_loop(0, 2 * MOE_EXPERTS, drain, 0)

    def wait_slot(s):
        for _ in range(2):
            pltpu.make_async_copy(buf_ref.at[s], xs_ref.at[pl.ds(0, TM)], sem_ref.at[s]).wait()

    @pl.when(i >= 2)
    def _():
        wait_slot(slot)

    buf_ref[slot] = h_ref[...]

    n_tok = n * TM

    def issue(r, c):
        a = i * TM + r
        _row_copy(buf_ref.at[slot], r, xs_ref, dest_ref[a], sem_ref.at[slot]).start(priority=0)
        _row_copy(buf_ref.at[slot], r, xs_ref, dest_ref[n_tok + a], sem_ref.at[slot]).start(priority=1)
        return c
    lax.fori_loop(0, TM, issue, 0, unroll=8)

    @pl.when(i == n - 1)
    def _():
        wait_slot(slot)

        @pl.when(n >= 2)
        def _():
            wait_slot(1 - slot)


def _dispatch(dest, zfill, h, n_rows):
    t = h.shape[0]
    return pl.pallas_call(
        _dispatch_kernel,
        grid_spec=pltpu.PrefetchScalarGridSpec(
            num_scalar_prefetch=2,
            grid=(t // TM,),
            in_specs=[pl.BlockSpec((TM,) + ROW_TILE, lambda i, d, z: (i, 0, 0))],
            out_specs=pl.BlockSpec(memory_space=pl.ANY),
            scratch_shapes=[pltpu.VMEM((2, TM) + ROW_TILE, F32), pltpu.VMEM((MOE_BM,) + ROW_TILE, F32),
                            pltpu.SemaphoreType.DMA((2,)), pltpu.SemaphoreType.DMA((1,))],
        ),
        out_shape=jax.ShapeDtypeStruct((n_rows,) + ROW_TILE, F32),
        compiler_params=_cparams(("arbitrary",)),
        name="moe_dispatch",
    )(dest, zfill, h)


def _dest_kernel(ids_ref, pstart_ref, o_ref):
    ids = ids_ref[...]
    lane = lax.broadcasted_iota(jnp.int32, ids.shape, 1)
    pstart = pstart_ref[...]

    def row_of(k):
        base = jnp.sum(jnp.where(lane == ids[:, k:k + 1], pstart, 0.0), axis=-1, keepdims=True)
        return base + ids[:, 2 + k:3 + k].astype(F32)

    rows = jnp.where(lane == 0, row_of(0), jnp.where(lane == 1, row_of(1), 0.0))
    o_ref[...] = rows.T[:SUBLANES, :].astype(jnp.int32)


DEST_TM = 2048


def _dest_rows(ids, pstarts):
    t = ids.shape[0]
    pstart_row = jnp.zeros((1, LANES), F32).at[0, :MOE_EXPERTS].set(pstarts.astype(F32))
    tm = min(DEST_TM, t)
    out = pl.pallas_call(
        _dest_kernel,
        grid=(t // tm,),
        in_specs=[pl.BlockSpec((tm, LANES), lambda i: (i, 0)), _full_spec((1, LANES))],
        out_specs=pl.BlockSpec((SUBLANES, tm), lambda i: (0, i)),
        out_shape=jax.ShapeDtypeStruct((SUBLANES, t), jnp.int32),
        compiler_params=_cparams(("arbitrary",)),
        name="moe_dest",
    )(ids, pstart_row)
    return out[:2].reshape(2 * t)


def _expert_kernel(layer, be_ref, nb_ref, nxt_ref, xs_ref, w1_hbm, w3_hbm, w2_hbm, y_ref,
                   w1f_ref, w3f_ref, w2f_ref, w1b_ref, w3b_ref, w2b_ref, ord_ref, sem_ref):
    i = pl.program_id(0)
    used = i < nb_ref[0]
    prev = be_ref[jnp.maximum(i - 1, 0)]

    def fetch(e, s):
        return (pltpu.make_async_copy(w1_hbm.at[layer, e], w1f_ref.at[s], sem_ref.at[s]),
                pltpu.make_async_copy(w3_hbm.at[layer, e], w3f_ref.at[s], sem_ref.at[s]),
                pltpu.make_async_copy(w2_hbm.at[layer, e], w2f_ref.at[s], sem_ref.at[s]))

    @pl.when(i == 0)
    def _():
        ord_ref[0] = 0
        for c in fetch(be_ref[0], 0):
            c.start()

    @pl.when(used & ((i == 0) | (be_ref[i] != prev)))
    def _():
        s = ord_ref[0] % 2
        for c in fetch(be_ref[i], s):
            c.wait()
        w1b_ref[...] = w1f_ref[s].astype(BF16)
        w3b_ref[...] = w3f_ref[s].astype(BF16)
        w2b_ref[...] = w2f_ref[s].astype(BF16)

        @pl.when(nxt_ref[i] >= 0)
        def _():
            for c in fetch(nxt_ref[i], 1 - s):
                c.start()
        ord_ref[0] = ord_ref[0] + 1

    @pl.when(used)
    def _():
        xb = _load_tile_rows(xs_ref).astype(BF16)
        a = jnp.dot(xb, w1b_ref[...], preferred_element_type=F32)
        b = jnp.dot(xb, w3b_ref[...], preferred_element_type=F32)
        hmid = (_silu(a) * b).astype(BF16)
        _store_tile_rows(y_ref, jnp.dot(hmid, w2b_ref[...], preferred_element_type=F32))

    @pl.when(i >= nb_ref[0])
    def _():
        y_ref[...] = jnp.zeros_like(y_ref)


def _experts(blk_exp, n_used, blk_next, xs, layer, w1, w3, w2):
    n_rows = xs.shape[0]
    nb = n_rows // MOE_BM

    def row_map(i, be, nu, nx):
        return (jnp.minimum(i, nu[0] - 1), 0, 0)

    return pl.pallas_call(
        functools.partial(_expert_kernel, layer),
        grid_spec=pltpu.PrefetchScalarGridSpec(
            num_scalar_prefetch=3,
            grid=(nb,),
            in_specs=[pl.BlockSpec((MOE_BM,) + ROW_TILE, row_map),
                      pl.BlockSpec(memory_space=pl.ANY), pl.BlockSpec(memory_space=pl.ANY),
                      pl.BlockSpec(memory_space=pl.ANY)],
            out_specs=pl.BlockSpec((MOE_BM,) + ROW_TILE, lambda i, be, nu, nx: (i, 0, 0)),
            scratch_shapes=[pltpu.VMEM((2, D_MODEL, EXPERT_FF), F32), pltpu.VMEM((2, D_MODEL, EXPERT_FF), F32),
                            pltpu.VMEM((2, EXPERT_FF, D_MODEL), F32),
                            pltpu.VMEM((D_MODEL, EXPERT_FF), BF16), pltpu.VMEM((D_MODEL, EXPERT_FF), BF16),
                            pltpu.VMEM((EXPERT_FF, D_MODEL), BF16),
                            pltpu.SMEM((1,), jnp.int32), pltpu.SemaphoreType.DMA((2,))],
        ),
        out_shape=jax.ShapeDtypeStruct((n_rows,) + ROW_TILE, F32),
        compiler_params=_cparams(("arbitrary",)),
        name="moe_experts",
    )(blk_exp, n_used, blk_next, xs, w1, w3, w2)


def _combine_kernel(dest_ref, x_ref, gates_ref, gf_ref, y_ref, o_ref, buf_ref, sem_ref):
    i = pl.program_id(0)
    n = pl.num_programs(0)
    slot = i % 2

    n_tok = n * TM

    def issue(tile, s):
        def body(r, c):
            a = tile * TM + r
            _row_copy(y_ref, dest_ref[a], buf_ref.at[s, 0], r, sem_ref.at[s]).start(priority=0)
            _row_copy(y_ref, dest_ref[n_tok + a], buf_ref.at[s, 1], r, sem_ref.at[s]).start(priority=1)
            return c
        lax.fori_loop(0, TM, body, 0, unroll=8)

    @pl.when(i == 0)
    def _():
        issue(0, 0)

    @pl.when(i + 1 < n)
    def _():
        issue(i + 1, 1 - slot)

    for k in range(2):
        pltpu.make_async_copy(y_ref.at[pl.ds(0, TM)], buf_ref.at[slot, k], sem_ref.at[slot]).wait()

    w0 = gates_ref[:, 0:1]
    w1 = gates_ref[:, 1:2]
    mix = _load_tile_rows(buf_ref.at[slot, 0]) * w0 + _load_tile_rows(buf_ref.at[slot, 1]) * w1
    o_ref[...] = x_ref[...] + gf_ref[0] * mix


def _combine(dest, x2, seq, gates, gf, y):
    t = x2.shape[0]
    return pl.pallas_call(
        _combine_kernel,
        grid_spec=pltpu.PrefetchScalarGridSpec(
            num_scalar_prefetch=1,
            grid=(t // TM,),
            in_specs=[pl.BlockSpec((TM, D_MODEL), lambda i, d: (i, 0)),
                      pl.BlockSpec((TM, LANES), lambda i, d: (i, 0)),
                      pl.BlockSpec((1, 1, D_MODEL), lambda i, d: ((i * TM) // seq, 0, 0)),
                      pl.BlockSpec(memory_space=pl.ANY)],
            out_specs=pl.BlockSpec((TM, D_MODEL), lambda i, d: (i, 0)),
            scratch_shapes=[pltpu.VMEM((2, 2, TM) + ROW_TILE, F32), pltpu.SemaphoreType.DMA((2,))],
        ),
        out_shape=jax.ShapeDtypeStruct((t, D_MODEL), F32),
        compiler_params=_cparams(("arbitrary",)),
        name="moe_combine",
    )(dest, x2, gates, gf, y)


def _moe_layer(x2, seq, g, sc, sh, gf, wr, layer, w1, w3, w2):
    t = x2.shape[0]
    h, ids, gates, cnt = _router(x2, seq, g, sc, sh, wr)
    counts = cnt[0, :MOE_EXPERTS].astype(jnp.int32)
    padded = (counts + MOE_BM - 1) // MOE_BM * MOE_BM
    pends = jnp.cumsum(padded)
    pstarts = pends - padded
    dest = _dest_rows(ids, pstarts)
    n_rows = 2 * t + MOE_EXPERTS * MOE_BM
    nb = n_rows // MOE_BM
    blk_start = jnp.arange(nb, dtype=jnp.int32) * MOE_BM
    blk_exp = jnp.minimum(jnp.sum((pends[None, :] <= blk_start[:, None]).astype(jnp.int32), axis=1), MOE_EXPERTS - 1)
    n_used = (pends[-1:] // MOE_BM).astype(jnp.int32)
    tail = pends[-1] + jnp.arange(MOE_EXPERTS, dtype=jnp.int32) * MOE_BM
    zfill = jnp.concatenate([jnp.where(padded > 0, pends - MOE_BM, -1),
                             jnp.where(tail < n_rows, tail, -1)]).astype(jnp.int32)
    xs = _dispatch(dest, zfill, h, n_rows)
    eid = jnp.arange(MOE_EXPERTS, dtype=jnp.int32)
    later = (padded[None, :] > 0) & (eid[None, :] > eid[:, None])
    next_exp = jnp.min(jnp.where(later, eid[None, :], MOE_EXPERTS), axis=1)
    next_exp = jnp.where(next_exp < MOE_EXPERTS, next_exp, -1).astype(jnp.int32)
    y = _experts(blk_exp, n_used, next_exp[blk_exp], xs, layer, w1, w3, w2)
    return _combine(dest, x2, seq, gates, gf, y)


HG_LT = 512


def _sublane_roll(x, shift):
    return pltpu.roll(x, shift % x.shape[0], 0)


def _hgrn_chunk(hq, hf, hi, lb, state_t):
    c, sub = HG_CHUNK, HG_SUB
    q = _silu(hq)
    f = lb + (1.0 - lb) * _sigmoid(hf)
    lf = jnp.log(f)
    k = 1.0 - f
    v = hi
    row = lax.broadcasted_iota(jnp.int32, (c, c), 0)
    col = lax.broadcasted_iota(jnp.int32, (c, c), 1)
    start = (row // sub) * sub
    tri = jnp.concatenate([(col >= start) & (col <= row), col < start], axis=0).astype(BF16)
    sums = sum(jnp.dot(tri, part, preferred_element_type=F32) for part in _split_bf16(lf, 3))
    bloc = sums[:c]
    rref = sums[c:]
    b = rref + bloc
    qt = q * jnp.exp(bloc)
    ridx = lax.broadcasted_iota(jnp.int32, (c, HG_DK), 0)

    blocks = [jnp.zeros((sub, c), F32)]
    for i in range(1, c // sub):
        r_i = rref[i * sub:i * sub + 1, :]
        live = ridx < i * sub
        k_i = jnp.where(live, k * jnp.exp(jnp.where(live, r_i - b, 0.0)), 0.0)
        blocks.append(lax.dot_general(qt[i * sub:(i + 1) * sub].astype(BF16), k_i.astype(BF16),
                                      (((1,), (1,)), ((), ())), preferred_element_type=F32))
    attn_off = jnp.concatenate(blocks, axis=0)
    o = jnp.dot(attn_off.astype(BF16), v.astype(BF16), preferred_element_type=F32)

    pos = ridx % sub
    for d in range(sub):
        qd = q if d == 0 else _sublane_roll(q, -d)
        bd = bloc if d == 0 else _sublane_roll(bloc, -d)
        ok = pos + d < sub
        e = qd * k * jnp.exp(jnp.where(ok, bd - bloc, -jnp.inf))
        w = jnp.sum(e, axis=-1, keepdims=True) * v
        o = o + (w if d == 0 else _sublane_roll(w, d))

    qe = qt * jnp.exp(rref)
    o = o + lax.dot_general(qe.astype(BF16), state_t.astype(BF16), (((1,), (1,)), ((), ())),
                            preferred_element_type=F32)
    b_end = b[c - 1:c, :]
    kd = k * jnp.exp(b_end - b)
    new_state = state_t * jnp.exp(b_end) + lax.dot_general(v.astype(BF16), kd.astype(BF16), (((0,), (0,)), ((), ())),
                                                           preferred_element_type=F32)
    return o, new_state


def _hgrn_kernel(hq_ref, hf_ref, hi_ref, hg_ref, lb_ref, og_ref, o_ref, state_ref):
    @pl.when(pl.program_id(0) == 0)
    def _():
        state_ref[...] = jnp.zeros_like(state_ref)

    og = og_ref[...]

    def body(ci, carry):
        rows = pl.ds(pl.multiple_of(ci * HG_CHUNK, HG_CHUNK), HG_CHUNK)
        for b in range(hq_ref.shape[0]):
            for h in range(HG_HEADS):
                cs = slice(h * HG_DK, (h + 1) * HG_DK)
                o, st = _hgrn_chunk(hq_ref[b, rows, cs], hf_ref[b, rows, cs], hi_ref[b, rows, cs], lb_ref[:, cs],
                                    state_ref[b, h])
                state_ref[b, h] = st
                on = o * lax.rsqrt(jnp.mean(o * o, axis=-1, keepdims=True) + EPS) * og
                o_ref[b, rows, cs] = on * _silu(hg_ref[b, rows, cs])
        return carry
    lax.fori_loop(0, HG_LT // HG_CHUNK, body, 0)


def _hgrn(z, batch, seq, lb, o_gain):
    t = z.shape[0]
    z3 = z.reshape(batch, seq, z.shape[1])

    def col(base):
        return pl.BlockSpec((batch, HG_LT, HG_W), lambda l: (0, l, base // HG_W))

    out = pl.pallas_call(
        _hgrn_kernel,
        grid=(seq // HG_LT,),
        in_specs=[col(Z_HQ), col(Z_HF), col(Z_HI), col(Z_HG),
                  pl.BlockSpec((1, HG_W), lambda l: (0, 0)),
                  pl.BlockSpec((1, HG_DV), lambda l: (0, 0))],
        out_specs=pl.BlockSpec((batch, HG_LT, HG_W), lambda l: (0, l, 0)),
        out_shape=jax.ShapeDtypeStruct((batch, seq, HG_W), F32),
        scratch_shapes=[pltpu.VMEM((batch, HG_HEADS, HG_DV, HG_DK), F32)],
        compiler_params=_cparams(("arbitrary",)),
        name="hgrn2",
    )(z3, z3, z3, z3, lb.reshape(1, -1), o_gain.reshape(1, -1))
    return out.reshape(t, HG_W)


def _rms64(v, gain):
    return v * lax.rsqrt(jnp.mean(v * v, axis=-1, keepdims=True) + EPS) * gain


def _nsa_prep_kernel(kc_ref, vc_ref, ks_ref, vs_ref, kw_ref, vw_ref, kg_ref,
                     kcr_ref, vcr_ref, ksn_ref, vsb_ref, kwn_ref, vwb_ref):
    ones_col = (lax.broadcasted_iota(jnp.int32, (TM, NSA_VW - NSA_HD), 1) == 0).astype(F32)

    def with_ones(v):
        return jnp.concatenate([v, ones_col], axis=1).astype(BF16)

    for g in range(NSA_KV_GROUPS):
        cs = slice(g * NSA_HD, (g + 1) * NSA_HD)
        kcr_ref[0, g] = kc_ref[:, cs]
        vcr_ref[0, g] = vc_ref[:, cs]
        ksn_ref[0, g] = _rms64(ks_ref[:, cs], kg_ref[1:2, :]).astype(BF16)
        vsb_ref[0, g] = with_ones(vs_ref[:, cs])
        kwn_ref[0, g] = _rms64(kw_ref[:, cs], kg_ref[2:3, :]).astype(BF16)
        vwb_ref[0, g] = with_ones(vw_ref[:, cs])


def _nsa_prep(z, batch, seq, k_gain):
    per = seq // TM

    def col(base):
        return pl.BlockSpec((TM, NSA_KV_W), lambda i: (i, base // NSA_KV_W))

    def out(dtype, width=NSA_HD):
        return (pl.BlockSpec((1, NSA_KV_GROUPS, TM, width), lambda i: (i // per, 0, i % per, 0)),
                jax.ShapeDtypeStruct((batch, NSA_KV_GROUPS, seq, width), dtype))

    outs = [out(F32), out(F32), out(BF16), out(BF16, NSA_VW), out(BF16), out(BF16, NSA_VW)]
    return pl.pallas_call(
        _nsa_prep_kernel,
        grid=(batch * per,),
        in_specs=[col(Z_KC), col(Z_VC), col(Z_KS), col(Z_VS), col(Z_KW), col(Z_VW), _full_spec((3, NSA_HD))],
        out_specs=[o[0] for o in outs],
        out_shape=[o[1] for o in outs],
        compiler_params=_cparams(("arbitrary",)),
        name="nsa_prep",
    )(z, z, z, z, z, z, k_gain)


def _nsa_compress_kernel(ak_ref, av_ref, wk_ref, wv_ref, pe_ref, kg_ref, kc_ref, vc_ref):
    half = CMP_STRIDE * NSA_HD

    def compress(a, w_ref):
        top = jnp.dot(a, w_ref[:half, :], precision=HIGHEST, preferred_element_type=F32)
        bot = jnp.dot(a, w_ref[half:, :], precision=HIGHEST, preferred_element_type=F32)
        pe = jnp.dot(pe_ref[...], w_ref[...], precision=HIGHEST, preferred_element_type=F32)
        return top + _sublane_roll(bot, -1) + pe[0:1]

    kc_ref[0, 0] = _rms64(compress(ak_ref[0, 0], wk_ref), kg_ref[0:1, :])
    vc_ref[0, 0] = compress(av_ref[0, 0], wv_ref)


def _nsa_compress(kcr, vcr, w_ck, w_cv, cmp_pe, k_gain):
    batch, groups, seq, _ = kcr.shape
    nc = seq // CMP_STRIDE
    wide = CMP_STRIDE * NSA_HD
    spec = pl.BlockSpec((1, 1, nc, wide), lambda b, g: (b, g, 0, 0))
    ospec = pl.BlockSpec((1, 1, nc, NSA_HD), lambda b, g: (b, g, 0, 0))
    oshape = jax.ShapeDtypeStruct((batch, groups, nc, NSA_HD), F32)
    full2 = lambda shape: pl.BlockSpec(shape, lambda b, g: (0, 0))
    return pl.pallas_call(
        _nsa_compress_kernel,
        grid=(batch, groups),
        in_specs=[spec, spec, full2((2 * wide, NSA_HD)), full2((2 * wide, NSA_HD)), full2((8, 2 * wide)),
                  full2((3, NSA_HD))],
        out_specs=[ospec, ospec],
        out_shape=[oshape, oshape],
        compiler_params=_cparams(("arbitrary", "arbitrary")),
        name="nsa_compress",
    )(kcr.reshape(batch, groups, nc, wide), vcr.reshape(batch, groups, nc, wide), w_ck, w_cv,
      jnp.broadcast_to(cmp_pe.reshape(1, 2 * wide), (8, 2 * wide)), k_gain)


def _nsa_kernel(seq, kchunk, q_ref, gate_ref, kc_ref, vc_ref, ks_ref, vs_ref, kw_ref, vw_ref, qg_ref, agg_ref,
                o_ref):
    qi = pl.program_id(2)
    t0 = qi * Q_BLOCK
    jn = NSA_QPG
    nc = seq // CMP_STRIDE
    n_slc = seq // SLC_BLOCK
    n_sel = min(N_SELECT, n_slc)
    nt = (((1,), (1,)), ((), ()))

    qraw = q_ref[...]
    qs = [_rms64(qraw[:, j * NSA_HD:(j + 1) * NSA_HD], qg_ref[...]) * (NSA_HD ** -0.5) for j in range(jn)]
    qb = [q.astype(BF16) for q in qs]

    def pos_of(shape):
        return t0 + lax.broadcasted_iota(jnp.int32, shape, 0)

    cmp_end = lax.broadcasted_iota(jnp.int32, (Q_BLOCK, nc), 1) * CMP_STRIDE + (CMP_BLOCK - 1)
    valid_c = cmp_end <= pos_of((Q_BLOCK, nc))
    sees_any = pos_of((Q_BLOCK, 1)) >= CMP_BLOCK - 1
    kc_hi, kc_lo = _split_bf16(kc_ref[0, 0], 2)
    vcb = vc_ref[0, 0].astype(BF16)
    o_c = []
    imp = None
    for j in range(jn):
        qh, ql = _split_bf16(qs[j], 2)
        s = (lax.dot_general(qh, kc_hi, nt, preferred_element_type=F32)
             + (lax.dot_general(qh, kc_lo, nt, preferred_element_type=F32)
                + lax.dot_general(ql, kc_hi, nt, preferred_element_type=F32)))
        sm = jnp.where(valid_c, s, NEG)
        e = jnp.exp(sm - jnp.max(sm, axis=-1, keepdims=True))
        p = e * jnp.where(sees_any, 1.0 / jnp.sum(e, axis=-1, keepdims=True), 0.0)
        o_c.append(jnp.dot(p.astype(BF16), vcb, preferred_element_type=F32))
        imp = p if imp is None else imp + p

    agg = agg_ref[...].astype(BF16)
    imp_s = sum(jnp.dot(part, agg, preferred_element_type=F32) for part in _split_bf16(imp, 3))
    ids = lax.broadcasted_iota(jnp.int32, (Q_BLOCK, n_slc), 1)
    q_blk = pos_of((Q_BLOCK, n_slc)) // SLC_BLOCK
    forced = (ids == 0) | (ids == q_blk) | (ids == q_blk - 1)
    score_t = jnp.where(ids > q_blk, NEG, jnp.where(forced, FORCE, imp_s)).T
    ids_t = lax.broadcasted_iota(jnp.int32, (n_slc, Q_BLOCK), 0).astype(F32)
    cur = score_t
    sel_t = jnp.zeros((n_slc, Q_BLOCK), F32)
    for _ in range(n_sel):
        m = jnp.max(cur, axis=0, keepdims=True)
        first = jnp.min(jnp.where(cur == m, ids_t, float(n_slc)), axis=0, keepdims=True)
        hit = ids_t == first
        sel_t = jnp.where(hit, 1.0, sel_t)
        cur = jnp.where(hit, -jnp.inf, cur)
    sel = jnp.where(score_t > NEG / 2, sel_t, 0.0).T.astype(BF16)

    span = Q_BLOCK + WINDOW
    start = pl.multiple_of(jnp.maximum(t0 - WINDOW, 0), Q_BLOCK)
    wsl = pl.ds(start, span)
    kpos = start + lax.broadcasted_iota(jnp.int32, (Q_BLOCK, span), 1)
    wpos = pos_of((Q_BLOCK, span))
    bias_w = jnp.where((kpos <= wpos) & (kpos > wpos - WINDOW), 0.0, NEG)
    kw = kw_ref[0, 0, wsl, :]
    vw = vw_ref[0, 0, wsl, :]
    o_w = []
    for j in range(jn):
        s = lax.dot_general(qb[j], kw, nt, preferred_element_type=F32) + bias_w
        e = jnp.exp((s - jnp.max(s, axis=-1, keepdims=True)).astype(BF16))
        ow = jnp.dot(e, vw, preferred_element_type=F32)
        o_w.append(ow[:, :NSA_HD] / ow[:, NSA_HD:NSA_HD + 1])

    per_chunk = kchunk // SLC_BLOCK
    blk_of_key = lax.broadcasted_iota(jnp.int32, (n_slc, kchunk), 1) // SLC_BLOCK
    blk_row = lax.broadcasted_iota(jnp.int32, (n_slc, kchunk), 0)
    key_lane = lax.broadcasted_iota(jnp.int32, (Q_BLOCK, kchunk), 1)
    qpos = pos_of((Q_BLOCK, kchunk))

    def sel_body(c, carry):
        ksl = pl.ds(pl.multiple_of(c * kchunk, kchunk), kchunk)
        k_c = ks_ref[0, 0, ksl, :]
        v_c = vs_ref[0, 0, ksl, :]
        expand = (blk_row == blk_of_key + c * per_chunk).astype(BF16)
        chosen = jnp.dot(sel, expand, preferred_element_type=F32)
        bias = jnp.where((chosen > 0.5) & (key_lane + c * kchunk <= qpos), 0.0, NEG)
        m_run, acc = carry
        s = lax.dot_general(qb_all, k_c, nt, preferred_element_type=F32) + jnp.concatenate([bias] * jn, axis=0)
        m_new = jnp.maximum(m_run, jnp.max(s, axis=-1, keepdims=True))
        alpha = jnp.exp(m_run - m_new)
        p = jnp.exp((s - m_new).astype(BF16))
        acc_new = acc * alpha + jnp.dot(p, v_c, preferred_element_type=F32)
        return m_new, acc_new

    rows = jn * Q_BLOCK
    qb_all = jnp.concatenate(qb, axis=0)
    n_chunks = (t0 + Q_BLOCK + kchunk - 1) // kchunk
    init = (jnp.full((rows, 1), NEG, F32), jnp.zeros((rows, NSA_VW), F32))
    _, acc_fin = lax.fori_loop(0, n_chunks, sel_body, init)
    o_sel = acc_fin[:, :NSA_HD] / acc_fin[:, NSA_HD:NSA_HD + 1]

    gates = _sigmoid(gate_ref[...])
    outs = []
    for j in range(jn):
        o_s = o_sel[j * Q_BLOCK:(j + 1) * Q_BLOCK]
        outs.append(gates[:, 3 * j:3 * j + 1] * o_c[j] + gates[:, 3 * j + 1:3 * j + 2] * o_s
                    + gates[:, 3 * j + 2:3 * j + 3] * o_w[j])
    o_ref[...] = jnp.concatenate(outs, axis=1)


def _selection_weights(seq):
    nc = seq // CMP_STRIDE
    n_slc = seq // SLC_BLOCK
    ratio = SLC_BLOCK // CMP_STRIDE
    n = jnp.arange(nc)[:, None]
    j = jnp.arange(n_slc)[None, :]
    o = n - ratio * j + (CMP_BLOCK // CMP_STRIDE - 1)
    w = jnp.asarray(SEL_AGG_W, F32)
    return jnp.where((o >= 0) & (o < len(SEL_AGG_W)), w[jnp.clip(o, 0, len(SEL_AGG_W) - 1)], 0.0)


def _nsa(z, batch, seq, w_ck, w_cv, cmp_pe, q_gain, k_gain):
    t = z.shape[0]
    kcr, vcr, ksn, vsb, kwn, vwb = _nsa_prep(z, batch, seq, k_gain)
    kc, vc = _nsa_compress(kcr, vcr, w_ck, w_cv, cmp_pe, k_gain)
    nq = seq // Q_BLOCK
    nc = seq // CMP_STRIDE
    n_slc = seq // SLC_BLOCK
    kchunk = min(NSA_KCHUNK, seq)
    qw = NSA_QPG * NSA_HD
    cmp_spec = pl.BlockSpec((1, 1, nc, NSA_HD), lambda b, g, i: (b, g, 0, 0))
    key_spec = pl.BlockSpec((1, 1, seq, NSA_HD), lambda b, g, i: (b, g, 0, 0))
    val_spec = pl.BlockSpec((1, 1, seq, NSA_VW), lambda b, g, i: (b, g, 0, 0))
    return pl.pallas_call(
        functools.partial(_nsa_kernel, seq, kchunk),
        grid=(batch, NSA_KV_GROUPS, nq),
        in_specs=[pl.BlockSpec((Q_BLOCK, qw), lambda b, g, i: (b * nq + i, g)),
                  pl.BlockSpec((Q_BLOCK, LANES), lambda b, g, i: (b * nq + i, Z_GATE // LANES + g)),
                  cmp_spec, cmp_spec, key_spec, val_spec, key_spec, val_spec,
                  pl.BlockSpec((1, NSA_HD), lambda b, g, i: (0, 0)),
                  pl.BlockSpec((nc, n_slc), lambda b, g, i: (0, 0))],
        out_specs=pl.BlockSpec((Q_BLOCK, qw), lambda b, g, i: (b * nq + i, g)),
        out_shape=jax.ShapeDtypeStruct((t, NSA_W), F32),
        compiler_params=_cparams(("arbitrary", "arbitrary", "arbitrary")),
        name="nsa_attention",
    )(z, z, kc, vc, ksn, vsb, kwn, vwb, q_gain.reshape(1, -1), _selection_weights(seq))


def _pad_in_proj(w_in):
    per_group = NSA_QPG * 3
    gate_lo = Z_GATE
    out = jnp.zeros((D_MODEL, Z_N), F32)
    out = out.at[:, :gate_lo].set(w_in[:, :gate_lo])
    for g in range(NSA_KV_GROUPS):
        out = out.at[:, gate_lo + g * LANES:gate_lo + g * LANES + per_group].set(
            w_in[:, gate_lo + g * per_group:gate_lo + (g + 1) * per_group])
    return out.at[:, Z_HQ:].set(w_in[:, gate_lo + NSA_HEADS * 3:])


def _even_mixer(x2, batch, seq, g, sc, sh, gm, w_in, w_out, w_ck, w_cv, cmp_pe, q_gain, k_gain, lb, o_gain):
    z = _in_proj(x2, seq, g, sc, sh, _pad_in_proj(w_in).astype(BF16))
    r = _hgrn(z, batch, seq, lb, o_gain)
    a = _nsa(z, batch, seq, w_ck, w_cv, cmp_pe, q_gain, k_gain)
    return _out_proj(x2, seq, a, r, gm, w_out.astype(BF16))


def _odd_mixer(x2, seq, g, sc, sh, gm, w_pw1, dw, dw_b, ln_g, ln_b, w_pw2):
    dw_pad = jnp.zeros((CONV_HALO, D_MODEL), F32).at[:CONV_W].set(dw)
    return _conv_module(x2, seq, g, sc, sh, w_pw1.astype(BF16), dw_pad, dw_b.reshape(1, -1), ln_g.reshape(1, -1),
                        ln_b.reshape(1, -1), gm, w_pw2.astype(BF16))


def _router_weights(w_group, w_expert):
    wr = jnp.zeros((D_MODEL, LANES), F32)
    return wr.at[:, :MOE_GROUPS].set(w_group).at[:, MOE_GROUPS:MOE_GROUPS + MOE_EXPERTS].set(w_expert)


def _lower_bounds_kernel(l_ref, o_ref):
    logits = l_ref[...]
    e = jnp.exp(logits - jnp.max(logits, axis=0, keepdims=True))
    p = e / jnp.sum(e, axis=0, keepdims=True)
    n = logits.shape[0]
    acc = jnp.zeros_like(p[0:1])
    for i in range(n):
        acc = acc + p[i:i + 1]
        o_ref[i:i + 1, :] = acc - p[0:1]


def _lower_bounds(lb_logits):
    return pl.pallas_call(
        _lower_bounds_kernel,
        out_shape=jax.ShapeDtypeStruct(lb_logits.shape, F32),
        name="hgrn_lower_bounds",
    )(lb_logits)


def kernel(x, c, ada_w, ada_b, norm_mix, norm_ffn, mix_w_in, mix_w_out, nsa_cmp_wk, nsa_cmp_wv, nsa_cmp_pe, nsa_q_gain, nsa_k_gain, hgrn_lb_logits, hgrn_o_gain, conv_w_pw1, conv_dw, conv_dw_b, conv_ln_g, conv_ln_b, conv_w_pw2, moe_w_group, moe_w_expert, moe_w1, moe_w3, moe_w2):
    batch, seq, d = x.shape
    x2 = x.reshape(batch * seq, d)
    mod = _modulation(c, ada_w, ada_b)
    lower_bounds = _lower_bounds(hgrn_lb_logits)
    for layer in range(DEPTH):
        sh_m, sc_m, g_m, sh_f, sc_f, g_f = (mod[layer, :, k] for k in range(6))
        i = layer // 2
        gain = norm_mix[layer].reshape(1, d)
        if layer % 2 == 0:
            x2 = _even_mixer(x2, batch, seq, gain, sc_m, sh_m, g_m, mix_w_in[i], mix_w_out[i], nsa_cmp_wk[i],
                             nsa_cmp_wv[i], nsa_cmp_pe[i], nsa_q_gain[i], nsa_k_gain[i], lower_bounds[i],
                             hgrn_o_gain[i])
        else:
            x2 = _odd_mixer(x2, seq, gain, sc_m, sh_m, g_m, conv_w_pw1[i], conv_dw[i], conv_dw_b[i], conv_ln_g[i],
                            conv_ln_b[i], conv_w_pw2[i])
        x2 = _moe_layer(x2, seq, norm_ffn[layer].reshape(1, d), sc_f, sh_f, g_f,
                        _router_weights(moe_w_group[layer], moe_w_expert[layer]), layer, moe_w1, moe_w3, moe_w2)
    return x2.reshape(batch, seq, d)
```
